```python
import math
import jax, jax.numpy as jnp
from jax import lax
import numpy as np

D_MODEL = 1024
BATCH = 8
SEQ = 2048
DEPTH = 4
DEC_BATCH = 128
DEC_SEQ = 8
PAST_LEN = 8192
PAGE_SIZE = 128

N_EVEN = (DEPTH + 1) // 2
N_ODD = DEPTH // 2
HEAD_DIM = 64
BLK = 128
RMS_EPS = 1e-6
NEG = -1e30

A_Q_HEADS = D_MODEL // 128
A_KV_HEADS = A_Q_HEADS // 4
A_GQA = A_Q_HEADS // A_KV_HEADS
A_WINDOW = 128
A_Q_W = A_Q_HEADS * HEAD_DIM
A_KV_W = A_KV_HEADS * HEAD_DIM

B_WIDTH = D_MODEL // 2
B_GROUP_CH = 16
B_GROUPS = B_WIDTH // B_GROUP_CH
B_STATE = 64

C_WIDTH = D_MODEL // 2
C_BLOCKS = 8
C_BLOCK_W = C_WIDTH // C_BLOCKS
C_CONV = 4
C_POWER = 8.0

D_PAIRS = ((128, 1), (512, 4), (2048, 16))
D_N_GROUPS = 3
D_HEADS_PER = 4
D_QKV_W = D_N_GROUPS * D_HEADS_PER * HEAD_DIM
D_OUT_W = D_HEADS_PER * HEAD_DIM

T5_BUCKETS = 32
T5_MAX_DIST = 2048
T5_HEADS = A_Q_HEADS + D_N_GROUPS * D_HEADS_PER

MOE_GROUPS = 4
MOE_PER_GROUP = 4
MOE_EXPERTS = MOE_GROUPS * MOE_PER_GROUP
MOE_TOPK = 2
MOE_FF = 256

EVEN_IN = A_Q_W + 2 * A_KV_W + B_WIDTH
EVEN_OUT = A_Q_W + B_WIDTH
ODD_IN = 2 * C_WIDTH + 3 * D_QKV_W
ODD_OUT = C_WIDTH + D_OUT_W

kernel_name = 'hybrid_swa_s5_rglru_dilated_hmoe_step'

F32 = jnp.float32


def rmsnorm(x, g):
    xf = x.astype(F32)
    y = xf * lax.rsqrt(jnp.mean(xf * xf, axis=-1, keepdims=True) + RMS_EPS)
    return (y * g.astype(F32)).astype(x.dtype)


def t5_bucket(dist):
    n = jnp.maximum(dist, 0)
    max_exact = T5_BUCKETS // 2
    nf = jnp.maximum(n, 1).astype(F32)
    large = max_exact + (jnp.log(nf / max_exact) / math.log(T5_MAX_DIST / max_exact)
                         * (T5_BUCKETS - max_exact)).astype(jnp.int32)
    return jnp.where(n < max_exact, n, jnp.minimum(large, T5_BUCKETS - 1))


def rel_bias(table, dist):
    return jnp.moveaxis(jnp.take(table.astype(F32), t5_bucket(dist), axis=0), -1, 0)


def seq_blocks(x, blk):
    bt, length = x.shape[:2]
    nb = -(-length // blk)
    x = jnp.pad(x, [(0, 0), (0, nb * blk - length)] + [(0, 0)] * (x.ndim - 2))
    return x.reshape(bt, nb, blk, *x.shape[2:])


def band_keys(xb):
    prev = jnp.pad(xb, [(0, 0), (1, 0)] + [(0, 0)] * (xb.ndim - 2))[:, :-1]
    return jnp.concatenate([prev, xb], axis=2)


def band_geometry(nb, blk, max_dist):
    r = jnp.arange(blk)[:, None]
    s = jnp.arange(2 * blk)[None, :]
    dist = blk + r - s
    k_idx = (jnp.arange(nb)[:, None, None] - 1) * blk + s[None]
    valid = (dist >= 0) & (dist <= max_dist) & (k_idx >= 0)
    return dist, valid


def gqa_sink_attend(q, k, v, bias, valid, sinks):
    s = jnp.einsum('...qhgd,...khd->...hgqk', q, k).astype(F32) * HEAD_DIM ** -0.5 + bias
    s = jnp.where(valid[..., None, None, :, :], s, NEG)
    sink = sinks.astype(F32).reshape(A_KV_HEADS, A_GQA)[:, :, None]
    m = jnp.maximum(jnp.max(s, axis=-1), sink)
    p = jnp.exp(s - m[..., None])
    denom = jnp.sum(p, axis=-1) + jnp.exp(sink - m)
    o = jnp.einsum('...hgqk,...khd->...qhgd', p / denom[..., None], v.astype(F32))
    return o.astype(q.dtype)


def mha_lse_attend(q, k, v, bias, valid):
    s = jnp.einsum('...qhd,...khd->...hqk', q, k).astype(F32) * HEAD_DIM ** -0.5 + bias
    s = jnp.where(valid[..., None, :, :], s, NEG)
    m = jnp.max(s, axis=-1)
    p = jnp.exp(s - m[..., None])
    denom = jnp.sum(p, axis=-1)
    o = jnp.einsum('...hqk,...khd->...qhd', p / denom[..., None], v.astype(F32))
    return o, m + jnp.log(denom)


def linear_scan(a, b, h0):
    b = b.at[:, 0].add(a[:, 0] * h0)

    def combine(left, right):
        return right[0] * left[0], right[0] * left[1] + right[1]

    _, h = lax.associative_scan(combine, (a, b), axis=1)
    return h


def residue_split(x, d):
    bt, t = x.shape[:2]
    x = jnp.moveaxis(x.reshape(bt, t // d, d, *x.shape[2:]), 2, 1)
    return x.reshape(bt * d, t // d, *x.shape[3:])


def residue_merge(x, bt, d):
    length = x.shape[1]
    x = jnp.moveaxis(x.reshape(bt, d, length, *x.shape[2:]), 1, 2)
    return x.reshape(bt, length * d, *x.shape[3:])


def swa_prompt(q, k, v, sinks, table):
    bt, t = q.shape[:2]
    qb = seq_blocks(q.reshape(bt, t, A_KV_HEADS, A_GQA, HEAD_DIM), BLK)
    kb = band_keys(seq_blocks(k, BLK))
    vb = band_keys(seq_blocks(v, BLK))
    nb = qb.shape[1]
    dist, valid = band_geometry(nb, BLK, A_WINDOW - 1)
    bias = rel_bias(table[:, :A_Q_HEADS], dist).reshape(A_KV_HEADS, A_GQA, BLK, 2 * BLK)
    o = gqa_sink_attend(qb, kb, vb, bias, valid, sinks)
    o = o.reshape(bt, nb * BLK, A_Q_W)[:, :t]
    w = min(A_WINDOW, t)
    return o, jnp.stack([k[:, t - w:], v[:, t - w:]], axis=2)


def swa_sample(q, k, v, cache, sinks, table):
    bt, s_len = q.shape[:2]
    w = cache.shape[1]
    k_all = jnp.concatenate([cache[:, :, 0].astype(k.dtype), k], axis=1)
    v_all = jnp.concatenate([cache[:, :, 1].astype(v.dtype), v], axis=1)
    dist = w + jnp.arange(s_len)[:, None] - jnp.arange(w + s_len)[None, :]
    valid = (dist >= 0) & (dist < A_WINDOW)
    bias = rel_bias(table[:, :A_Q_HEADS], dist).reshape(A_KV_HEADS, A_GQA, s_len, w + s_len)
    o = gqa_sink_attend(q.reshape(bt, s_len, A_KV_HEADS, A_GQA, HEAD_DIM), k_all, v_all, bias, valid, sinks)
    new = jnp.concatenate([cache.astype(k.dtype), jnp.stack([k, v], axis=2)], axis=1)[:, -w:]
    return o.reshape(bt, s_len, A_Q_W), new


def s5_mixer(u, h0, p):
    bt, t, _ = u.shape
    uf = u.astype(F32).reshape(bt, t, B_GROUPS, B_GROUP_CH)
    lam = lax.complex(p['s5_a_re'].astype(F32), p['s5_a_im'].astype(F32))
    dt = jnp.exp(p['s5_log_dt'].astype(F32))[:, None]
    lam_bar = jnp.exp(lam * dt)
    b_mat = lax.complex(p['s5_b_re'].astype(F32), p['s5_b_im'].astype(F32))
    b_bar = ((lam_bar - 1.0) / lam)[..., None] * b_mat
    bu = jnp.einsum('gnc,btgc->btgn', b_bar, uf.astype(jnp.complex64))
    h = linear_scan(jnp.broadcast_to(lam_bar, bu.shape), bu, h0)
    c_mat = lax.complex(p['s5_c_re'].astype(F32), p['s5_c_im'].astype(F32))
    y = jnp.einsum('gcn,btgn->btgc', c_mat, h).real \
        + p['s5_d'].astype(F32).reshape(B_GROUPS, B_GROUP_CH) * uf
    y = jax.nn.gelu(y.reshape(bt, t, B_WIDTH))
    out = y * jax.nn.sigmoid(y @ p['s5_glu_w'].astype(F32) + p['s5_glu_b'].astype(F32))
    return out.astype(u.dtype), h[:, -1]


def rglru_mixer(xr, gate, conv_buf, h0, p):
    bt, t, _ = xr.shape
    xp = jnp.concatenate([conv_buf.astype(xr.dtype), xr], axis=1)
    w = p['conv_w']
    xc = xp[:, 0:t] * w[0]
    for tap in range(1, C_CONV):
        xc = xc + xp[:, tap:tap + t] * w[tap]
    xcf = (xc + p['conv_b']).astype(F32)
    xblk = xcf.reshape(bt, t, C_BLOCKS, C_BLOCK_W)
    r = jax.nn.sigmoid(jnp.einsum('btnj,njk->btnk', xblk, p['gate_a_w'].astype(F32)).reshape(bt, t, C_WIDTH)
                       + p['gate_a_b'].astype(F32))
    i = jax.nn.sigmoid(jnp.einsum('btnj,njk->btnk', xblk, p['gate_x_w'].astype(F32)).reshape(bt, t, C_WIDTH)
                       + p['gate_x_b'].astype(F32))
    log_a = -C_POWER * r * jax.nn.softplus(-p['lru_lambda'].astype(F32))
    a = jnp.exp(log_a)
    b = jnp.sqrt(-jnp.expm1(2.0 * log_a)) * (i * xcf)
    h = linear_scan(a, b, h0.astype(F32))
    y = h * jax.nn.gelu(gate.astype(F32))
    return y.astype(xr.dtype), h[:, -1], xp[:, t:]


def combine_dilations(outs, lses):
    o = jnp.stack(outs, axis=2)
    wts = jax.nn.softmax(jnp.stack(lses, axis=2), axis=2)
    y = jnp.sum(o * wts[..., None], axis=2)
    return y.reshape(y.shape[0], y.shape[1], D_OUT_W)


def dilated_prompt(q, k, v, table):
    bt, t = q.shape[:2]
    outs, lses, caches = [], [], []
    for g, (win, dil) in enumerate(D_PAIRS):
        max_d = win // dil
        blk = max(BLK, max_d)
        lo = A_Q_HEADS + g * D_HEADS_PER
        qs = residue_split(q[:, :, g], dil)
        ks = residue_split(k[:, :, g], dil)
        vs = residue_split(v[:, :, g], dil)
        length = qs.shape[1]
        qb = seq_blocks(qs, blk)
        kb = band_keys(seq_blocks(ks, blk))
        vb = band_keys(seq_blocks(vs, blk))
        nb = qb.shape[1]
        dist, valid = band_geometry(nb, blk, max_d)
        bias = rel_bias(table[:, lo:lo + D_HEADS_PER], dist * dil)
        o, lse = mha_lse_attend(qb, kb, vb, bias, valid)
        o = o.reshape(bt * dil, nb * blk, D_HEADS_PER, HEAD_DIM)[:, :length]
        lse = jnp.swapaxes(lse, -1, -2).reshape(bt * dil, nb * blk, D_HEADS_PER)[:, :length]
        outs.append(residue_merge(o, bt, dil))
        lses.append(residue_merge(lse, bt, dil))
        wc = min(win, t)
        caches.append(jnp.stack([k[:, t - wc:, g], v[:, t - wc:, g]], axis=2))
    return combine_dilations(outs, lses), caches


def dilated_sample(q, k, v, caches, table):
    s_len = q.shape[1]
    outs, lses, news = [], [], []
    for g, (win, dil) in enumerate(D_PAIRS):
        cache = caches[g].astype(k.dtype)
        wc = cache.shape[1]
        lo = A_Q_HEADS + g * D_HEADS_PER
        k_all = jnp.concatenate([cache[:, :, 0], k[:, :, g]], axis=1)
        v_all = jnp.concatenate([cache[:, :, 1], v[:, :, g]], axis=1)
        m = jnp.arange(win // dil + 1)
        idx = wc + jnp.arange(s_len)[:, None] - m[None, :] * dil
        valid = idx >= 0
        kg = jnp.take(k_all, jnp.maximum(idx, 0), axis=1)
        vg = jnp.take(v_all, jnp.maximum(idx, 0), axis=1)
        bias = rel_bias(table[:, lo:lo + D_HEADS_PER], m * dil)[:, None, :]
        o, lse = mha_lse_attend(q[:, :, g][:, :, None], kg, vg, bias, valid[:, None, :])
        outs.append(o[:, :, 0])
        lses.append(lse[..., 0])
        new_rows = jnp.stack([k[:, :, g], v[:, :, g]], axis=2)
        news.append(jnp.concatenate([cache, new_rows], axis=1)[:, -wc:])
    return combine_dilations(outs, lses), news


def even_mixer(h, kv_cache, s5_state, p, table):
    bt, t, _ = h.shape
    proj = h @ p['w_in']
    q, k, v, u = jnp.split(proj, [A_Q_W, A_Q_W + A_KV_W, A_Q_W + 2 * A_KV_W], axis=-1)
    q = q.reshape(bt, t, A_Q_HEADS, HEAD_DIM)
    k = k.reshape(bt, t, A_KV_HEADS, HEAD_DIM)
    v = v.reshape(bt, t, A_KV_HEADS, HEAD_DIM)
    if kv_cache is None:
        oa, new_kv = swa_prompt(q, k, v, p['sinks'], table)
        h0 = jnp.zeros((bt, B_GROUPS, B_STATE), jnp.complex64)
    else:
        oa, new_kv = swa_sample(q, k, v, kv_cache, p['sinks'], table)
        h0 = lax.complex(s5_state[..., 0].astype(F32), s5_state[..., 1].astype(F32))
    ob, h_last = s5_mixer(u, h0, p)
    out = jnp.concatenate([oa, ob.astype(oa.dtype)], axis=-1) @ p['w_out']
    return out, new_kv, jnp.stack([h_last.real, h_last.imag], axis=-1)


def odd_mixer(h, conv_buf, lru_h, d_caches, p, table):
    bt, t, _ = h.shape
    proj = h @ p['w_in']
    xr, gate, q, k, v = jnp.split(
        proj, [C_WIDTH, 2 * C_WIDTH, 2 * C_WIDTH + D_QKV_W, 2 * C_WIDTH + 2 * D_QKV_W], axis=-1)
    q = q.reshape(bt, t, D_N_GROUPS, D_HEADS_PER, HEAD_DIM)
    k = k.reshape(bt, t, D_N_GROUPS, D_HEADS_PER, HEAD_DIM)
    v = v.reshape(bt, t, D_N_GROUPS, D_HEADS_PER, HEAD_DIM)
    if d_caches is None:
        conv_buf = jnp.zeros((bt, C_CONV - 1, C_WIDTH), h.dtype)
        lru_h = jnp.zeros((bt, C_WIDTH), F32)
        od, new_d = dilated_prompt(q, k, v, table)
    else:
        od, new_d = dilated_sample(q, k, v, d_caches, table)
    oc, h_last, new_conv = rglru_mixer(xr, gate, conv_buf, lru_h, p)
    out = jnp.concatenate([oc, od.astype(oc.dtype)], axis=-1) @ p['w_out']
    return out, h_last, new_conv, new_d


def hier_moe(x, p):
    bt, t, dm = x.shape
    xf = x.reshape(bt * t, dm)
    n = xf.shape[0]
    g_logits = (xf @ p['router_g']).astype(F32) + p['router_g_b'].astype(F32)
    g_idx = jnp.argmax(g_logits, axis=-1)
    g_w = jnp.take_along_axis(jax.nn.softmax(g_logits, axis=-1), g_idx[:, None], axis=-1)
    e_logits = ((xf @ p['router_e']).astype(F32) + p['router_e_b'].astype(F32)).reshape(
        n, MOE_GROUPS, MOE_PER_GROUP)
    e_in = jnp.take_along_axis(e_logits, g_idx[:, None, None], axis=1)[:, 0]
    top_v, top_i = lax.top_k(e_in, MOE_TOPK)
    top_w = jax.nn.softmax(top_v, axis=-1) * g_w
    eid = g_idx[:, None] * MOE_PER_GROUP + top_i
    gates = jnp.sum(jax.nn.one_hot(eid, MOE_EXPERTS, dtype=F32) * top_w[..., None], axis=1)
    hid = jax.nn.silu(jnp.einsum('nd,edf->nef', xf, p['w_gate'])) * jnp.einsum('nd,edf->nef', xf, p['w_up'])
    y = jnp.einsum('nef,efd->nd', hid * gates[..., None].astype(hid.dtype), p['w_down'])
    return y.reshape(bt, t, dm)


def setup_inputs(seed: int = 0) -> dict:
    key = jax.random.key(seed)
    ks = iter(jax.random.split(key, 64))

    def nrm(shape, scale):
        return jax.random.normal(next(ks), shape, F32) * scale

    w_a = min(A_WINDOW, PAST_LEN)
    w_d = [min(w, PAST_LEN) for (w, _) in D_PAIRS]
    a0 = jax.random.uniform(next(ks), (N_ODD, C_WIDTH), F32, 0.9, 0.999)
    sig = a0 ** (1.0 / C_POWER)
    return {
        'x_prompt': nrm((BATCH, SEQ, D_MODEL), 1.0),
        'x_sample': nrm((DEC_BATCH, DEC_SEQ, D_MODEL), 1.0),
        'cache_a_kv': nrm((N_EVEN, DEC_BATCH, w_a, 2, A_KV_HEADS, HEAD_DIM), 1.0),
        'state_b': nrm((N_EVEN, DEC_BATCH, B_GROUPS, B_STATE, 2), 0.1),
        'state_c_h': nrm((N_ODD, DEC_BATCH, C_WIDTH), 0.5),
        'state_c_conv': nrm((N_ODD, DEC_BATCH, C_CONV - 1, C_WIDTH), 1.0),
        'cache_d_g0': nrm((N_ODD, DEC_BATCH, w_d[0], 2, D_HEADS_PER, HEAD_DIM), 1.0),
        'cache_d_g1': nrm((N_ODD, DEC_BATCH, w_d[1], 2, D_HEADS_PER, HEAD_DIM), 1.0),
        'cache_d_g2': nrm((N_ODD, DEC_BATCH, w_d[2], 2, D_HEADS_PER, HEAD_DIM), 1.0),
        'rel_table': nrm((T5_BUCKETS, T5_HEADS), 0.5),
        'norm_mix': 1.0 + nrm((DEPTH, D_MODEL), 0.05),
        'norm_ffn': 1.0 + nrm((DEPTH, D_MODEL), 0.05),
        'norm_final': 1.0 + nrm((D_MODEL,), 0.05),
        'w_in_even': nrm((N_EVEN, D_MODEL, EVEN_IN), D_MODEL ** -0.5),
        'w_out_even': nrm((N_EVEN, EVEN_OUT, D_MODEL), EVEN_OUT ** -0.5),
        'sinks_a': nrm((N_EVEN, A_Q_HEADS), 0.5),
        's5_a_re': -0.5 + nrm((N_EVEN, B_GROUPS, B_STATE), 0.01),
        's5_a_im': jnp.pi * jnp.arange(B_STATE, dtype=F32) + nrm((N_EVEN, B_GROUPS, B_STATE), 0.01),
        's5_log_dt': jax.random.uniform(next(ks), (N_EVEN, B_GROUPS), F32, math.log(1e-3), math.log(1e-1)),
        's5_b_re': nrm((N_EVEN, B_GROUPS, B_STATE, B_GROUP_CH), (2 * B_GROUP_CH) ** -0.5),
        's5_b_im': nrm((N_EVEN, B_GROUPS, B_STATE, B_GROUP_CH), (2 * B_GROUP_CH) ** -0.5),
        's5_c_re': nrm((N_EVEN, B_GROUPS, B_GROUP_CH, B_STATE), B_STATE ** -0.5),
        's5_c_im': nrm((N_EVEN, B_GROUPS, B_GROUP_CH, B_STATE), B_STATE ** -0.5),
        's5_d': nrm((N_EVEN, B_WIDTH), 1.0),
        's5_glu_w': nrm((N_EVEN, B_WIDTH, B_WIDTH), B_WIDTH ** -0.5),
        's5_glu_b': nrm((N_EVEN, B_WIDTH), 0.01),
        'w_in_odd': nrm((N_ODD, D_MODEL, ODD_IN), D_MODEL ** -0.5),
        'w_out_odd': nrm((N_ODD, ODD_OUT, D_MODEL), ODD_OUT ** -0.5),
        'conv_w': nrm((N_ODD, C_CONV, C_WIDTH), C_CONV ** -0.5),
        'conv_b': nrm((N_ODD, C_WIDTH), 0.01),
        'gate_a_w': nrm((N_ODD, C_BLOCKS, C_BLOCK_W, C_BLOCK_W), C_BLOCK_W ** -0.5),
        'gate_a_b': nrm((N_ODD, C_WIDTH), 0.01),
        'gate_x_w': nrm((N_ODD, C_BLOCKS, C_BLOCK_W, C_BLOCK_W), C_BLOCK_W ** -0.5),
        'gate_x_b': nrm((N_ODD, C_WIDTH), 0.01),
        'lru_lambda': jnp.log(sig) - jnp.log1p(-sig),
        'moe_router_g': nrm((DEPTH, D_MODEL, MOE_GROUPS), D_MODEL ** -0.5),
        'moe_router_g_b': nrm((DEPTH, MOE_GROUPS), 0.01),
        'moe_router_e': nrm((DEPTH, D_MODEL, MOE_EXPERTS), D_MODEL ** -0.5),
        'moe_router_e_b': nrm((DEPTH, MOE_EXPERTS), 0.01),
        'moe_w_gate': nrm((DEPTH, MOE_EXPERTS, D_MODEL, MOE_FF), D_MODEL ** -0.5),
        'moe_w_up': nrm((DEPTH, MOE_EXPERTS, D_MODEL, MOE_FF), D_MODEL ** -0.5),
        'moe_w_down': nrm((DEPTH, MOE_EXPERTS, MOE_FF, D_MODEL), MOE_FF ** -0.5),
    }


def reference(x_prompt, x_sample, cache_a_kv, state_b, state_c_h, state_c_conv,
              cache_d_g0, cache_d_g1, cache_d_g2,
              rel_table, norm_mix, norm_ffn, norm_final,
              w_in_even, w_out_even, sinks_a, s5_a_re, s5_a_im, s5_log_dt,
              s5_b_re, s5_b_im, s5_c_re, s5_c_im, s5_d, s5_glu_w, s5_glu_b,
              w_in_odd, w_out_odd, conv_w, conv_b, gate_a_w, gate_a_b, gate_x_w, gate_x_b, lru_lambda,
              moe_router_g, moe_router_g_b, moe_router_e, moe_router_e_b,
              moe_w_gate, moe_w_up, moe_w_down):
    yp, ys = x_prompt, x_sample
    a_p, a_s, b_p, b_s = [], [], [], []
    ch_p, ch_s, cc_p, cc_s = [], [], [], []
    d0_p, d0_s, d1_p, d1_s, d2_p, d2_s = [], [], [], [], [], []
    for layer in range(DEPTH):
        hp = rmsnorm(yp, norm_mix[layer])
        hs = rmsnorm(ys, norm_mix[layer])
        if layer % 2 == 0:
            e = layer // 2
            ep = {'w_in': w_in_even[e], 'w_out': w_out_even[e], 'sinks': sinks_a[e],
                  's5_a_re': s5_a_re[e], 's5_a_im': s5_a_im[e], 's5_log_dt': s5_log_dt[e],
                  's5_b_re': s5_b_re[e], 's5_b_im': s5_b_im[e], 's5_c_re': s5_c_re[e], 's5_c_im': s5_c_im[e],
                  's5_d': s5_d[e], 's5_glu_w': s5_glu_w[e], 's5_glu_b': s5_glu_b[e]}
            op, kv_np, sb_np = even_mixer(hp, None, None, ep, rel_table)
            os_, kv_ns, sb_ns = even_mixer(hs, cache_a_kv[e], state_b[e], ep, rel_table)
            a_p.append(kv_np)
            a_s.append(kv_ns)
            b_p.append(sb_np)
            b_s.append(sb_ns)
        else:
            o = layer // 2
            od = {'w_in': w_in_odd[o], 'w_out': w_out_odd[o], 'conv_w': conv_w[o], 'conv_b': conv_b[o],
                  'gate_a_w': gate_a_w[o], 'gate_a_b': gate_a_b[o], 'gate_x_w': gate_x_w[o],
                  'gate_x_b': gate_x_b[o], 'lru_lambda': lru_lambda[o]}
            op, h_np, conv_np, d_np = odd_mixer(hp, None, None, None, od, rel_table)
            os_, h_ns, conv_ns, d_ns = odd_mixer(hs, state_c_conv[o], state_c_h[o],
                                                 (cache_d_g0[o], cache_d_g1[o], cache_d_g2[o]), od, rel_table)
            ch_p.append(h_np)
            ch_s.append(h_ns)
            cc_p.append(conv_np)
            cc_s.append(conv_ns)
            d0_p.append(d_np[0])
            d0_s.append(d_ns[0])
            d1_p.append(d_np[1])
            d1_s.append(d_ns[1])
            d2_p.append(d_np[2])
            d2_s.append(d_ns[2])
        yp = yp + op.astype(yp.dtype)
        ys = ys + os_.astype(ys.dtype)
        mp = {'router_g': moe_router_g[layer], 'router_g_b': moe_router_g_b[layer],
              'router_e': moe_router_e[layer], 'router_e_b': moe_router_e_b[layer],
              'w_gate': moe_w_gate[layer], 'w_up': moe_w_up[layer], 'w_down': moe_w_down[layer]}
        yp = yp + hier_moe(rmsnorm(yp, norm_ffn[layer]), mp).astype(yp.dtype)
        ys = ys + hier_moe(rmsnorm(ys, norm_ffn[layer]), mp).astype(ys.dtype)
    y_prompt = rmsnorm(yp, norm_final)
    y_sample = rmsnorm(ys, norm_final)
    return (y_prompt, y_sample,
            jnp.stack(a_p), jnp.stack(a_s), jnp.stack(b_p), jnp.stack(b_s),
            jnp.stack(ch_p), jnp.stack(ch_s), jnp.stack(cc_p), jnp.stack(cc_s),
            jnp.stack(d0_p), jnp.stack(d0_s), jnp.stack(d1_p), jnp.stack(d1_s),
            jnp.stack(d2_p), jnp.stack(d2_s))
```

```python
import functools
import math

import jax
import jax.numpy as jnp
from jax import lax
from jax.experimental import pallas as pl
from jax.experimental.pallas import tpu as pltpu

F32 = jnp.float32
BF16 = jnp.bfloat16

D_MODEL = 1024
BATCH = 8
SEQ = 2048
DEPTH = 4
DEC_BATCH = 128
DEC_SEQ = 8
HEAD_DIM = 64
BLK = 128
RMS_EPS = 1e-6
NEG = -1e30

A_Q_HEADS = 8
A_KV_HEADS = 2
A_GQA = 4
A_WINDOW = 128
A_Q_W = A_Q_HEADS * HEAD_DIM
A_KV_W = A_KV_HEADS * HEAD_DIM

B_WIDTH = 512
B_GROUP_CH = 16
B_GROUPS = 32
B_STATE = 64
B_LANE_BLOCKS = 4
B_GROUPS_PER_BLOCK = B_GROUPS // B_LANE_BLOCKS
B_HALF = B_GROUPS_PER_BLOCK * B_STATE
B_STATE_W = B_LANE_BLOCKS * 2 * B_HALF

C_WIDTH = 512
C_BLOCKS = 8
C_BLOCK_W = 64
C_CONV = 4
C_POWER = 8.0

D_PAIRS = ((128, 1), (512, 4), (2048, 16))
D_N_GROUPS = 3
D_HEADS_PER = 4
D_GROUP_W = D_HEADS_PER * HEAD_DIM
D_QKV_W = D_N_GROUPS * D_GROUP_W

T5_BUCKETS = 32
T5_MAX_DIST = 2048

MOE_GROUPS = 4
MOE_PER_GROUP = 4
MOE_EXPERTS = 16
MOE_FF = 256
ROUTER_LANES = 128

NP_TOK = BATCH * SEQ
NS_TOK = DEC_BATCH * DEC_SEQ

VMEM_LIMIT_BYTES = 52 * 2 ** 20


def _cparams(*sem):
    return pltpu.CompilerParams(dimension_semantics=sem, vmem_limit_bytes=VMEM_LIMIT_BYTES)


def _nt_dot(a, b):
    return lax.dot_general(a, b, (((1,), (1,)), ((), ())), preferred_element_type=F32)


def _norm_proj_kernel(x_ref, g_ref, w_ref, *out_refs, splits):
    x = x_ref[...]
    y = x * lax.rsqrt(jnp.mean(x * x, axis=-1, keepdims=True) + RMS_EPS)
    xn = (y * g_ref[...]).astype(BF16)
    off = 0
    for o_ref, width in zip(out_refs, splits):
        o_ref[...] = jnp.dot(xn, w_ref[:, off:off + width], preferred_element_type=F32)
        off += width


def _norm_proj(x, g, w, splits, name):
    n = x.shape[0]
    tm = 512
    return pl.pallas_call(
        functools.partial(_norm_proj_kernel, splits=splits),
        grid=(n // tm,),
        in_specs=[pl.BlockSpec((tm, D_MODEL), lambda i: (i, 0)),
                  pl.BlockSpec((1, D_MODEL), lambda i: (0, 0)),
                  pl.BlockSpec(w.shape, lambda i: (0, 0))],
        out_specs=[pl.BlockSpec((tm, s), lambda i: (i, 0)) for s in splits],
        out_shape=[jax.ShapeDtypeStruct((n, s), F32) for s in splits],
        compiler_params=_cparams("parallel"),
        name=name,
    )(x, g.reshape(1, D_MODEL), w)


def _softmax_parts(scores, sink):
    m = jnp.max(scores[0], axis=1, keepdims=True)
    for s in scores[1:]:
        m = jnp.maximum(m, jnp.max(s, axis=1, keepdims=True))
    if sink is not None:
        m = jnp.maximum(m, sink)
    ps = [jnp.exp(s - m) for s in scores]
    den = jnp.sum(ps[0], axis=1, keepdims=True)
    for p in ps[1:]:
        den = den + jnp.sum(p, axis=1, keepdims=True)
    if sink is not None:
        den = den + jnp.exp(sink - m)
    return ps, den, m


def _band_attn_kernel(*refs, n_heads, gqa, with_sink):
    if with_sink:
        sink_ref, q_ref, kc_ref, kp_ref, vc_ref, vp_ref, bm_ref, o_ref = refs
        lse_ref = None
    else:
        q_ref, kc_ref, kp_ref, vc_ref, vp_ref, bm_ref, o_ref, lse_ref = refs
    q = (q_ref[...] * HEAD_DIM ** -0.5).astype(BF16)
    kc = kc_ref[...].astype(BF16)
    kp = kp_ref[...].astype(BF16)
    vc = vc_ref[...].astype(BF16)
    vp = vp_ref[...].astype(BF16)
    outs, lses = [], []
    for h in range(n_heads):
        hk = h // gqa
        ksl = slice(hk * HEAD_DIM, (hk + 1) * HEAD_DIM)
        qh = q[:, h * HEAD_DIM:(h + 1) * HEAD_DIM]
        sp = _nt_dot(qh, kp[:, ksl]) + bm_ref[0, h, :, 0:BLK]
        sc = _nt_dot(qh, kc[:, ksl]) + bm_ref[0, h, :, BLK:2 * BLK]
        sink = sink_ref[h] if with_sink else None
        (pp, pc), den, m = _softmax_parts([sp, sc], sink)
        o = (jnp.dot(pp.astype(BF16), vp[:, ksl], preferred_element_type=F32)
             + jnp.dot(pc.astype(BF16), vc[:, ksl], preferred_element_type=F32))
        outs.append(o / den)
        if lse_ref is not None:
            lses.append(jnp.broadcast_to(m + jnp.log(den), (BLK, HEAD_DIM)))
    o_ref[...] = jnp.concatenate(outs, axis=1)
    if lse_ref is not None:
        lse_ref[...] = jnp.concatenate(lses, axis=1)


def _swa_prompt(q, kv, sinks, bm):
    qv = q.reshape(SEQ, BATCH * A_Q_W)
    kvv = kv.reshape(SEQ, BATCH * 2 * A_KV_W)
    nblk = SEQ // BLK
    prev = lambda i: jnp.maximum(i - 1, 0)
    out = pl.pallas_call(
        functools.partial(_band_attn_kernel, n_heads=A_Q_HEADS, gqa=A_GQA, with_sink=True),
        grid=(BATCH, nblk),
        in_specs=[pl.BlockSpec(memory_space=pltpu.SMEM),
                  pl.BlockSpec((BLK, A_Q_W), lambda b, i: (i, b)),
                  pl.BlockSpec((BLK, A_KV_W), lambda b, i: (i, 2 * b)),
                  pl.BlockSpec((BLK, A_KV_W), lambda b, i: (prev(i), 2 * b)),
                  pl.BlockSpec((BLK, A_KV_W), lambda b, i: (i, 2 * b + 1)),
                  pl.BlockSpec((BLK, A_KV_W), lambda b, i: (prev(i), 2 * b + 1)),
                  pl.BlockSpec((1, A_Q_HEADS, BLK, 2 * BLK), lambda b, i: (jnp.minimum(i, 1), 0, 0, 0))],
        out_specs=pl.BlockSpec((BLK, A_Q_W), lambda b, i: (i, b)),
        out_shape=jax.ShapeDtypeStruct((SEQ, BATCH * A_Q_W), F32),
        compiler_params=_cparams("parallel", "parallel"),
        name="swa_prompt",
    )(sinks, qv, kvv, kvv, kvv, kvv, bm)
    return out.reshape(NP_TOK, A_Q_W)


def _dilated_prompt_group(q, k, v, bm, g, dil):
    rows = SEQ // dil
    ncol = dil * BATCH
    qv = q.reshape(rows, ncol * D_QKV_W)
    kv_ = k.reshape(rows, ncol * D_QKV_W)
    vv = v.reshape(rows, ncol * D_QKV_W)
    nblk = rows // BLK
    prev = lambda i: jnp.maximum(i - 1, 0)
    col = lambda c: c * D_N_GROUPS + g
    cur_spec = pl.BlockSpec((BLK, D_GROUP_W), lambda c, i: (i, col(c)))
    prev_spec = pl.BlockSpec((BLK, D_GROUP_W), lambda c, i: (prev(i), col(c)))
    out_spec = pl.BlockSpec((BLK, D_GROUP_W), lambda c, i: (i, c))
    o, lse = pl.pallas_call(
        functools.partial(_band_attn_kernel, n_heads=D_HEADS_PER, gqa=1, with_sink=False),
        grid=(ncol, nblk),
        in_specs=[cur_spec, cur_spec, prev_spec, cur_spec, prev_spec,
                  pl.BlockSpec((1, D_HEADS_PER, BLK, 2 * BLK), lambda c, i: (jnp.minimum(i, 1), 0, 0, 0))],
        out_specs=[out_spec, out_spec],
        out_shape=[jax.ShapeDtypeStruct((rows, ncol * D_GROUP_W), F32)] * 2,
        compiler_params=_cparams("parallel", "parallel"),
        name=f"dilated_prompt_g{g}",
    )(qv, kv_, kv_, vv, vv, bm)
    return o.reshape(NP_TOK, D_GROUP_W), lse.reshape(NP_TOK, D_GROUP_W)


SWA_SAMPLE_BB = 8


def _pad_rows(x, rows):
    return jnp.concatenate([x, jnp.zeros((rows - x.shape[0], x.shape[1]), x.dtype)], axis=0)


def _swa_sample_kernel(sink_ref, q_ref, kv_ref, c_ref, bmc_ref, bmn_ref, o_ref):
    for bb in range(SWA_SAMPLE_BB):
        rs = slice(bb * DEC_SEQ, (bb + 1) * DEC_SEQ)
        q = (q_ref[rs, :] * HEAD_DIM ** -0.5).astype(BF16)
        kvn = _pad_rows(kv_ref[rs, :], BLK).astype(BF16)
        cache = c_ref[0, bb].astype(BF16)
        outs = [None] * A_Q_HEADS
        for hk in range(A_KV_HEADS):
            heads = range(hk * A_GQA, (hk + 1) * A_GQA)
            qs = jnp.concatenate([q[:, h * HEAD_DIM:(h + 1) * HEAD_DIM] for h in heads], axis=0)
            ksl = slice(hk * HEAD_DIM, (hk + 1) * HEAD_DIM)
            vsl = slice(A_KV_W + hk * HEAD_DIM, A_KV_W + (hk + 1) * HEAD_DIM)
            brow = slice(hk * A_GQA * DEC_SEQ, (hk + 1) * A_GQA * DEC_SEQ)
            s1 = _nt_dot(qs, cache[:, ksl]) + bmc_ref[brow, :]
            s2 = _nt_dot(qs, kvn[:, ksl]) + bmn_ref[brow, :]
            sink = jnp.concatenate(
                [jnp.full((DEC_SEQ, 1), sink_ref[h], F32) for h in heads], axis=0)
            (p1, p2), den, _ = _softmax_parts([s1, s2], sink)
            o = (jnp.dot(p1.astype(BF16), cache[:, vsl], preferred_element_type=F32)
                 + jnp.dot(p2.astype(BF16), kvn[:, vsl], preferred_element_type=F32)) / den
            for j, h in enumerate(heads):
                outs[h] = o[j * DEC_SEQ:(j + 1) * DEC_SEQ, :]
        o_ref[rs, :] = jnp.concatenate(outs, axis=1)


def _swa_sample(q, kv, cache_all, layer, sinks, bmc, bmn):
    bb = SWA_SAMPLE_BB
    return pl.pallas_call(
        _swa_sample_kernel,
        grid=(DEC_BATCH // bb,),
        in_specs=[pl.BlockSpec(memory_space=pltpu.SMEM),
                  pl.BlockSpec((bb * DEC_SEQ, A_Q_W), lambda i: (i, 0)),
                  pl.BlockSpec((bb * DEC_SEQ, 2 * A_KV_W), lambda i: (i, 0)),
                  pl.BlockSpec((1, bb, A_WINDOW, 2 * A_KV_W), lambda i: (layer, i, 0, 0)),
                  pl.BlockSpec(bmc.shape, lambda i: (0, 0)),
                  pl.BlockSpec(bmn.shape, lambda i: (0, 0))],
        out_specs=pl.BlockSpec((bb * DEC_SEQ, A_Q_W), lambda i: (i, 0)),
        out_shape=jax.ShapeDtypeStruct((NS_TOK, A_Q_W), F32),
        compiler_params=_cparams("parallel"),
        name="swa_sample",
    )(sinks, q, kv, cache_all, bmc, bmn)


D_G2_RES = 8


def _dil_sample_kernel(q_ref, k_ref, v_ref, c0_ref, c1_ref, c2_ref,
                       bc0_ref, bc1_ref, bc2_ref, bn_ref,
                       o0_ref, o1_ref, o2_ref, l0_ref, l1_ref, l2_ref):
    q = q_ref[...] * HEAD_DIM ** -0.5
    k = k_ref[...]
    v = v_ref[...]
    nrow = D_HEADS_PER * DEC_SEQ
    row_head = lax.broadcasted_iota(jnp.int32, (nrow, D_GROUP_W), 0) // DEC_SEQ
    lane_head = lax.broadcasted_iota(jnp.int32, (nrow, D_GROUP_W), 1) // HEAD_DIM
    head_mask = row_head == lane_head
    out_lane_head = lax.broadcasted_iota(jnp.int32, (DEC_SEQ, D_GROUP_W), 1) // HEAD_DIM

    c2 = c2_ref[0, 0]
    k2 = jnp.concatenate([c2[:, r * 2 * D_GROUP_W: r * 2 * D_GROUP_W + D_GROUP_W]
                          for r in range(D_G2_RES)], axis=0)
    v2 = jnp.concatenate([c2[:, r * 2 * D_GROUP_W + D_GROUP_W:(r + 1) * 2 * D_GROUP_W]
                          for r in range(D_G2_RES)], axis=0)
    c0 = c0_ref[0, 0]
    c1 = c1_ref[0, 0]
    groups = ((c0[:, :D_GROUP_W], c0[:, D_GROUP_W:], bc0_ref, o0_ref, l0_ref),
              (c1[:, :D_GROUP_W], c1[:, D_GROUP_W:], bc1_ref, o1_ref, l1_ref),
              (k2, v2, bc2_ref, o2_ref, l2_ref))
    for g, (kc, vc, bc_ref, o_ref, l_ref) in enumerate(groups):
        gsl = slice(g * D_GROUP_W, (g + 1) * D_GROUP_W)
        qg = q[:, gsl]
        qbd = jnp.where(head_mask, jnp.concatenate([qg] * D_HEADS_PER, axis=0), 0.0).astype(BF16)
        kn = _pad_rows(k[:, gsl], BLK).astype(BF16)
        vn = _pad_rows(v[:, gsl], BLK).astype(BF16)
        s1 = _nt_dot(qbd, kc.astype(BF16)) + bc_ref[0]
        s2 = _nt_dot(qbd, kn) + bn_ref[g]
        (p1, p2), den, m = _softmax_parts([s1, s2], None)
        of = (jnp.dot(p1.astype(BF16), vc.astype(BF16), preferred_element_type=F32)
              + jnp.dot(p2.astype(BF16), vn, preferred_element_type=F32)) / den
        lse = m + jnp.log(den)
        og = jnp.zeros((DEC_SEQ, D_GROUP_W), F32)
        lg = jnp.zeros((DEC_SEQ, D_GROUP_W), F32)
        for h in range(D_HEADS_PER):
            rs = slice(h * DEC_SEQ, (h + 1) * DEC_SEQ)
            sel = out_lane_head == h
            og = og + jnp.where(sel, of[rs, :], 0.0)
            lg = lg + jnp.where(sel, lse[rs, :], 0.0)
        o_ref[...] = og
        l_ref[...] = lg


def _dilated_sample(q, k, v, c0, c1, c2, layer, bcs, bn):
    c2v = c2.reshape(c2.shape[0], DEC_BATCH, D_PAIRS[2][0] // 16, 16 * 2 * D_GROUP_W)
    tok_spec = pl.BlockSpec((DEC_SEQ, D_QKV_W), lambda b: (b, 0))
    out_spec = pl.BlockSpec((DEC_SEQ, D_GROUP_W), lambda b: (b, 0))
    outs = pl.pallas_call(
        _dil_sample_kernel,
        grid=(DEC_BATCH,),
        in_specs=[tok_spec, tok_spec, tok_spec,
                  pl.BlockSpec((1, 1, D_PAIRS[0][0], 2 * D_GROUP_W), lambda b: (layer, b, 0, 0)),
                  pl.BlockSpec((1, 1, D_PAIRS[1][0], 2 * D_GROUP_W), lambda b: (layer, b, 0, 0)),
                  pl.BlockSpec((1, 1, D_PAIRS[2][0] // 16, D_G2_RES * 2 * D_GROUP_W),
                               lambda b: (layer, b, 0, 0)),
                  pl.BlockSpec(bcs[0].shape, lambda b: (0, 0, 0)),
                  pl.BlockSpec(bcs[1].shape, lambda b: (0, 0, 0)),
                  pl.BlockSpec(bcs[2].shape, lambda b: (0, 0, 0)),
                  pl.BlockSpec(bn.shape, lambda b: (0, 0, 0))],
        out_specs=[out_spec] * 6,
        out_shape=[jax.ShapeDtypeStruct((NS_TOK, D_GROUP_W), F32)] * 6,
        compiler_params=_cparams("parallel"),
        name="dilated_sample",
    )(q, k, v, c0, c1, c2v, bcs[0], bcs[1], bcs[2], bn)
    return outs[:3], outs[3:]


def _shift_kernel(c_ref, n_ref, o_ref, *, wc):
    o_ref[:, :, 0:wc - DEC_SEQ, :] = c_ref[:, :, DEC_SEQ:wc, :]
    o_ref[:, :, wc - DEC_SEQ:wc, :] = n_ref[...]


def _shift_cache(cache, new_rows, name):
    n_layers, _, wc, width = cache.shape
    bb = max(1, (4 * 2 ** 20) // (wc * width * 4))
    return pl.pallas_call(
        functools.partial(_shift_kernel, wc=wc),
        grid=(n_layers, DEC_BATCH // bb),
        in_specs=[pl.BlockSpec((1, bb, wc, width), lambda l, i: (l, i, 0, 0)),
                  pl.BlockSpec((1, bb, DEC_SEQ, width), lambda l, i: (l, i, 0, 0))],
        out_specs=pl.BlockSpec((1, bb, wc, width), lambda l, i: (l, i, 0, 0)),
        out_shape=jax.ShapeDtypeStruct(cache.shape, cache.dtype),
        compiler_params=_cparams("parallel", "parallel"),
        name=name,
    )(cache, new_rows)


def _s5_kernel(u_ref, h0_ref, bm_ref, cm_ref, lam_ref, d_ref, gw_ref, gb_ref, o_ref, hl_ref, hs_ref,
               *, bt, lc):
    rows = lc * bt
    blk_w = 2 * B_HALF

    @pl.when(pl.program_id(0) == 0)
    def _():
        hs_ref[0:bt, :] = h0_ref[...]

    u = u_ref[...]
    ub = u.astype(BF16)
    for j in range(B_LANE_BLOCKS):
        hs_ref[bt:, j * blk_w:(j + 1) * blk_w] = jnp.dot(
            ub[:, j * 128:(j + 1) * 128], bm_ref[j], preferred_element_type=F32)

    def step(t, carry):
        r0 = pl.multiple_of(t * bt, bt)
        for j in range(B_LANE_BLOCKS):
            re = slice(j * blk_w, j * blk_w + B_HALF)
            im = slice(j * blk_w + B_HALF, (j + 1) * blk_w)
            lr = lam_ref[2 * j:2 * j + 1, :]
            li = lam_ref[2 * j + 1:2 * j + 2, :]
            pr = hs_ref[pl.ds(r0, bt), re]
            pi = hs_ref[pl.ds(r0, bt), im]
            hs_ref[pl.ds(r0 + bt, bt), re] = lr * pr - li * pi + hs_ref[pl.ds(r0 + bt, bt), re]
            hs_ref[pl.ds(r0 + bt, bt), im] = lr * pi + li * pr + hs_ref[pl.ds(r0 + bt, bt), im]
        return carry

    lax.fori_loop(0, lc, step, 0)

    ys = [jnp.dot(hs_ref[bt:, j * blk_w:(j + 1) * blk_w].astype(BF16), cm_ref[j],
                  preferred_element_type=F32) for j in range(B_LANE_BLOCKS)]
    y = jax.nn.gelu(jnp.concatenate(ys, axis=1) + d_ref[...] * u)
    z = jnp.dot(y.astype(BF16), gw_ref[...], preferred_element_type=F32) + gb_ref[...]
    o_ref[...] = y * jax.nn.sigmoid(z)
    last = hs_ref[rows:rows + bt, :]
    hl_ref[...] = last
    hs_ref[0:bt, :] = last


def _s5_mixer(u, h0, prm, bt, lc, name):
    n = u.shape[0]
    rows = lc * bt
    full = lambda a: pl.BlockSpec(a.shape, lambda c: (0,) * a.ndim)
    return pl.pallas_call(
        functools.partial(_s5_kernel, bt=bt, lc=lc),
        grid=(n // rows,),
        in_specs=[pl.BlockSpec((rows, B_WIDTH), lambda c: (c, 0)), full(h0),
                  full(prm["bm"]), full(prm["cm"]), full(prm["lam"]), full(prm["d"]),
                  full(prm["glu_w"]), full(prm["glu_b"])],
        out_specs=[pl.BlockSpec((rows, B_WIDTH), lambda c: (c, 0)),
                   pl.BlockSpec((bt, B_STATE_W), lambda c: (0, 0))],
        out_shape=[jax.ShapeDtypeStruct((n, B_WIDTH), F32), jax.ShapeDtypeStruct((bt, B_STATE_W), F32)],
        scratch_shapes=[pltpu.VMEM((rows + bt, B_STATE_W), F32)],
        compiler_params=_cparams("arbitrary"),
        name=name,
    )(u, h0, prm["bm"], prm["cm"], prm["lam"], prm["d"], prm["glu_w"], prm["glu_b"])


def _s5_params(a_re, a_im, log_dt, b_re, b_im, c_re, c_im, d, glu_w, glu_b):
    lam = lax.complex(a_re.astype(F32), a_im.astype(F32))
    dt = jnp.exp(log_dt.astype(F32))[:, None]
    lam_bar = jnp.exp(lam * dt)
    b_bar = ((lam_bar - 1.0) / lam)[..., None] * lax.complex(b_re.astype(F32), b_im.astype(F32))
    nb, gb = B_LANE_BLOCKS, B_GROUPS_PER_BLOCK
    eye = jnp.eye(gb, dtype=F32)

    def in_mat(part):
        p = part.reshape(nb, gb, B_STATE, B_GROUP_CH)
        return jnp.einsum("jgnc,gh->jgchn", p, eye).reshape(nb, gb * B_GROUP_CH, gb * B_STATE)

    def out_mat(part):
        p = part.reshape(nb, gb, B_GROUP_CH, B_STATE)
        return jnp.einsum("jgcn,gh->jgnhc", p, eye).reshape(nb, gb * B_STATE, gb * B_GROUP_CH)

    bm = jnp.concatenate([in_mat(b_bar.real), in_mat(b_bar.imag)], axis=2).astype(BF16)
    cm = jnp.concatenate([out_mat(c_re.astype(F32)), out_mat(-c_im.astype(F32))], axis=1).astype(BF16)
    lam_rows = jnp.stack([lam_bar.real.reshape(nb, B_HALF), lam_bar.imag.reshape(nb, B_HALF)],
                         axis=1).reshape(2 * nb, B_HALF)
    return {"bm": bm, "cm": cm, "lam": lam_rows, "d": d.astype(F32).reshape(1, B_WIDTH),
            "glu_w": glu_w.astype(BF16), "glu_b": glu_b.astype(F32).reshape(1, B_WIDTH)}


def _s5_state_to_cols(state):
    bt = state.shape[0]
    s = state.astype(F32).reshape(bt, B_LANE_BLOCKS, B_GROUPS_PER_BLOCK, B_STATE, 2)
    return s.transpose(0, 1, 4, 2, 3).reshape(bt, B_STATE_W)


def _s5_cols_to_state(cols):
    bt = cols.shape[0]
    s = cols.reshape(bt, B_LANE_BLOCKS, 2, B_GROUPS_PER_BLOCK, B_STATE)
    return s.transpose(0, 1, 3, 4, 2).reshape(bt, B_GROUPS, B_STATE, 2)


def _rglru_kernel(xr_ref, gate_ref, cb_ref, h0_ref, cw_ref, cbias_ref, wa_ref, ba_ref, wx_ref, bx_ref,
                  nsp_ref, o_ref, hl_ref, xp_ref, a_ref, hs_ref, *, bt, lc):
    rows = lc * bt
    pad = (C_CONV - 1) * bt

    @pl.when(pl.program_id(0) == 0)
    def _():
        xp_ref[0:pad, :] = cb_ref[...]
        hs_ref[0:bt, :] = h0_ref[...]

    xp_ref[pad:, :] = xr_ref[...]
    xc = xp_ref[0:rows, :] * cw_ref[0:1, :]
    for tap in range(1, C_CONV):
        xc = xc + xp_ref[tap * bt:tap * bt + rows, :] * cw_ref[tap:tap + 1, :]
    xcf = xc + cbias_ref[...]
    xb = xcf.astype(BF16)
    r = jax.nn.sigmoid(jnp.dot(xb, wa_ref[...], preferred_element_type=F32) + ba_ref[...])
    i = jax.nn.sigmoid(jnp.dot(xb, wx_ref[...], preferred_element_type=F32) + bx_ref[...])
    log_a = nsp_ref[...] * r
    a = jnp.exp(log_a)
    a_ref[...] = a
    hs_ref[bt:, :] = jnp.sqrt(-jnp.tanh(log_a) * (a * a + 1.0)) * (i * xcf)

    def step(t, carry):
        r0 = pl.multiple_of(t * bt, bt)
        hs_ref[pl.ds(r0 + bt, bt), :] = (a_ref[pl.ds(r0, bt), :] * hs_ref[pl.ds(r0, bt), :]
                                         + hs_ref[pl.ds(r0 + bt, bt), :])
        return carry

    lax.fori_loop(0, lc, step, 0)

    o_ref[...] = hs_ref[bt:, :] * jax.nn.gelu(gate_ref[...])
    last = hs_ref[rows:rows + bt, :]
    hl_ref[...] = last
    hs_ref[0:bt, :] = last
    xp_ref[0:pad, :] = xp_ref[rows:rows + pad, :]


def _rglru_mixer(xr, gate, conv_buf, h0, prm, bt, lc, name):
    n = xr.shape[0]
    rows = lc * bt
    full = lambda a: pl.BlockSpec(a.shape, lambda c: (0,) * a.ndim)
    tok = pl.BlockSpec((rows, C_WIDTH), lambda c: (c, 0))
    names = ("conv_w", "conv_b", "wa", "ba", "wx", "bx", "nsp")
    return pl.pallas_call(
        functools.partial(_rglru_kernel, bt=bt, lc=lc),
        grid=(n // rows,),
        in_specs=[tok, tok, full(conv_buf), full(h0)] + [full(prm[k]) for k in names],
        out_specs=[tok, pl.BlockSpec((bt, C_WIDTH), lambda c: (0, 0))],
        out_shape=[jax.ShapeDtypeStruct((n, C_WIDTH), F32), jax.ShapeDtypeStruct((bt, C_WIDTH), F32)],
        scratch_shapes=[pltpu.VMEM((rows + (C_CONV - 1) * bt, C_WIDTH), F32),
                        pltpu.VMEM((rows, C_WIDTH), F32),
                        pltpu.VMEM((rows + bt, C_WIDTH), F32)],
        compiler_params=_cparams("arbitrary"),
        name=name,
    )(xr, gate, conv_buf, h0, *[prm[k] for k in names])


def _rglru_params(conv_w, conv_b, gate_a_w, gate_a_b, gate_x_w, gate_x_b, lru_lambda):
    eye = jnp.eye(C_BLOCKS, dtype=F32)

    def block_diag(w):
        return jnp.einsum("njk,nm->njmk", w.astype(F32), eye).reshape(C_WIDTH, C_WIDTH).astype(BF16)

    row = lambda x: x.astype(F32).reshape(1, C_WIDTH)
    return {"conv_w": conv_w.astype(F32), "conv_b": row(conv_b),
            "wa": block_diag(gate_a_w), "ba": row(gate_a_b),
            "wx": block_diag(gate_x_w), "bx": row(gate_x_b),
            "nsp": row(-C_POWER * jax.nn.softplus(-lru_lambda.astype(F32)))}


def _out_proj_even_kernel(x_ref, oa_ref, ob_ref, w_ref, o_ref):
    acc = jnp.dot(oa_ref[...].astype(BF16), w_ref[0:A_Q_W, :], preferred_element_type=F32)
    acc = acc + jnp.dot(ob_ref[...].astype(BF16), w_ref[A_Q_W:, :], preferred_element_type=F32)
    o_ref[...] = x_ref[...] + acc


def _out_proj_odd_kernel(x_ref, oc_ref, o0_ref, o1_ref, o2_ref, l0_ref, l1_ref, l2_ref, w_ref, o_ref):
    l0, l1, l2 = l0_ref[...], l1_ref[...], l2_ref[...]
    m = jnp.maximum(jnp.maximum(l0, l1), l2)
    e0, e1, e2 = jnp.exp(l0 - m), jnp.exp(l1 - m), jnp.exp(l2 - m)
    od = (o0_ref[...] * e0 + o1_ref[...] * e1 + o2_ref[...] * e2) / (e0 + e1 + e2)
    acc = jnp.dot(oc_ref[...].astype(BF16), w_ref[0:C_WIDTH, :], preferred_element_type=F32)
    acc = acc + jnp.dot(od.astype(BF16), w_ref[C_WIDTH:, :], preferred_element_type=F32)
    o_ref[...] = x_ref[...] + acc


def _out_proj(kernel, x, parts, w, name):
    n = x.shape[0]
    tm = 512
    return pl.pallas_call(
        kernel,
        grid=(n // tm,),
        in_specs=[pl.BlockSpec((tm, D_MODEL), lambda i: (i, 0))]
        + [pl.BlockSpec((tm, p.shape[1]), lambda i: (i, 0)) for p in parts]
        + [pl.BlockSpec(w.shape, lambda i: (0, 0))],
        out_specs=pl.BlockSpec((tm, D_MODEL), lambda i: (i, 0)),
        out_shape=jax.ShapeDtypeStruct((n, D_MODEL), F32),
        input_output_aliases={0: 0},
        compiler_params=_cparams("parallel"),
        name=name,
    )(x, *parts, w)


MOE_TM = 512


def _moe_kernel(x_ref, g_ref, rw_ref, rb_ref, wg_ref, wu_ref, wd_ref, o_ref, xn_ref, gates_ref, acc_ref):
    e = pl.program_id(1)

    @pl.when(e == 0)
    def _():
        x = x_ref[...]
        y = x * lax.rsqrt(jnp.mean(x * x, axis=-1, keepdims=True) + RMS_EPS)
        xn = y * g_ref[...]
        xn_ref[...] = xn.astype(BF16)
        logits = lax.dot_general(xn, rw_ref[...], (((1,), (0,)), ((), ())),
                                 precision=lax.Precision.HIGHEST, preferred_element_type=F32) + rb_ref[...]
        lane = lax.broadcasted_iota(jnp.int32, logits.shape, 1).astype(F32)
        ninf = float("-inf")
        far = float(ROUTER_LANES)
        lg = jnp.where(lane < MOE_GROUPS, logits, ninf)
        gmax = jnp.max(lg, axis=1, keepdims=True)
        g_idx = jnp.min(jnp.where(lg == gmax, lane, far), axis=1, keepdims=True)
        g_w = 1.0 / jnp.sum(jnp.exp(lg - gmax), axis=1, keepdims=True)
        lane_grp = jnp.floor((lane - MOE_GROUPS) * (1.0 / MOE_PER_GROUP))
        in_grp = (lane >= MOE_GROUPS) & (lane < MOE_GROUPS + MOE_EXPERTS) & (lane_grp == g_idx)
        le = jnp.where(in_grp, logits, ninf)
        v1 = jnp.max(le, axis=1, keepdims=True)
        i1 = jnp.min(jnp.where(le == v1, lane, far), axis=1, keepdims=True)
        le2 = jnp.where(lane == i1, ninf, le)
        v2 = jnp.max(le2, axis=1, keepdims=True)
        i2 = jnp.min(jnp.where(le2 == v2, lane, far), axis=1, keepdims=True)
        e2 = jnp.exp(v2 - v1)
        w1 = g_w / (1.0 + e2)
        w2 = g_w * e2 / (1.0 + e2)
        gates_ref[...] = (jnp.where(lane == i1 - MOE_GROUPS, w1, 0.0)
                          + jnp.where(lane == i2 - MOE_GROUPS, w2, 0.0))
        acc_ref[...] = jnp.zeros_like(acc_ref)

    xb = xn_ref[...]
    gates = gates_ref[...]
    lane_i = lax.broadcasted_iota(jnp.int32, gates.shape, 1)
    gate = jnp.sum(jnp.where(lane_i == e, gates, 0.0), axis=1, keepdims=True)
    hg = jnp.dot(xb, wg_ref[0], preferred_element_type=F32)
    hu = jnp.dot(xb, wu_ref[0], preferred_element_type=F32)
    hid = jax.nn.silu(hg) * hu * gate
    acc_ref[...] += jnp.dot(hid.astype(BF16), wd_ref[0], preferred_element_type=F32)

    @pl.when(e == MOE_EXPERTS - 1)
    def _():
        o_ref[...] = x_ref[...] + acc_ref[...]


def _moe(x, prm, name):
    n = x.shape[0]
    tm = MOE_TM
    return pl.pallas_call(
        _moe_kernel,
        grid=(n // tm, MOE_EXPERTS),
        in_specs=[pl.BlockSpec((tm, D_MODEL), lambda i, e: (i, 0)),
                  pl.BlockSpec((1, D_MODEL), lambda i, e: (0, 0)),
                  pl.BlockSpec((D_MODEL, ROUTER_LANES), lambda i, e: (0, 0)),
                  pl.BlockSpec((1, ROUTER_LANES), lambda i, e: (0, 0)),
                  pl.BlockSpec((1, D_MODEL, MOE_FF), lambda i, e: (e, 0, 0)),
                  pl.BlockSpec((1, D_MODEL, MOE_FF), lambda i, e: (e, 0, 0)),
                  pl.BlockSpec((1, MOE_FF, D_MODEL), lambda i, e: (e, 0, 0))],
        out_specs=pl.BlockSpec((tm, D_MODEL), lambda i, e: (i, 0)),
        out_shape=jax.ShapeDtypeStruct((n, D_MODEL), F32),
        scratch_shapes=[pltpu.VMEM((tm, D_MODEL), BF16),
                        pltpu.VMEM((tm, ROUTER_LANES), F32),
                        pltpu.VMEM((tm, D_MODEL), F32)],
        input_output_aliases={0: 0},
        compiler_params=_cparams("parallel", "arbitrary"),
        name=name,
    )(x, prm["g"], prm["rw"], prm["rb"], prm["wg"], prm["wu"], prm["wd"])


def _moe_params(norm_g, router_g, router_g_b, router_e, router_e_b, w_gate, w_up, w_down):
    used = MOE_GROUPS + MOE_EXPERTS
    rw = jnp.concatenate([router_g.astype(F32), router_e.astype(F32),
                          jnp.zeros((D_MODEL, ROUTER_LANES - used), F32)], axis=1)
    rb = jnp.concatenate([router_g_b.astype(F32), router_e_b.astype(F32),
                          jnp.zeros((ROUTER_LANES - used,), F32)]).reshape(1, ROUTER_LANES)
    return {"g": norm_g.astype(F32).reshape(1, D_MODEL), "rw": rw, "rb": rb,
            "wg": w_gate.astype(BF16), "wu": w_up.astype(BF16), "wd": w_down.astype(BF16)}


def _final_norm_kernel(x_ref, g_ref, o_ref):
    x = x_ref[...]
    y = x * lax.rsqrt(jnp.mean(x * x, axis=-1, keepdims=True) + RMS_EPS)
    o_ref[...] = y * g_ref[...]


def _final_norm(x, g, name):
    n = x.shape[0]
    tm = 512
    return pl.pallas_call(
        _final_norm_kernel,
        grid=(n // tm,),
        in_specs=[pl.BlockSpec((tm, D_MODEL), lambda i: (i, 0)),
                  pl.BlockSpec((1, D_MODEL), lambda i: (0, 0))],
        out_specs=pl.BlockSpec((tm, D_MODEL), lambda i: (i, 0)),
        out_shape=jax.ShapeDtypeStruct((n, D_MODEL), F32),
        compiler_params=_cparams("parallel"),
        name=name,
    )(x, g.astype(F32).reshape(1, D_MODEL))


def _t5_bucket(dist):
    n = jnp.maximum(dist, 0)
    max_exact = T5_BUCKETS // 2
    nf = jnp.maximum(n, 1).astype(F32)
    large = max_exact + (jnp.log(nf / max_exact) / math.log(T5_MAX_DIST / max_exact)
                         * (T5_BUCKETS - max_exact)).astype(jnp.int32)
    return jnp.where(n < max_exact, n, jnp.minimum(large, T5_BUCKETS - 1))


def _rel_bias(table, dist):
    return jnp.moveaxis(jnp.take(table.astype(F32), _t5_bucket(dist), axis=0), -1, 0)


def _band_bias_mask(table, max_dist, dil):
    r = jnp.arange(BLK)[:, None]
    s = jnp.arange(2 * BLK)[None, :]
    dist = BLK + r - s
    valid = (dist >= 0) & (dist <= max_dist)
    bias = _rel_bias(table, dist * dil)
    later = jnp.where(valid[None], bias, NEG)
    first = jnp.where((valid & (s >= BLK))[None], bias, NEG)
    return jnp.stack([first, later])


def _swa_sample_bias_mask(table):
    s = jnp.arange(DEC_SEQ)[:, None]
    dist_c = A_WINDOW + s - jnp.arange(A_WINDOW)[None, :]
    dist_n = s - jnp.arange(BLK)[None, :]
    out = []
    for hk in range(A_KV_HEADS):
        tab = table[:, hk * A_GQA:(hk + 1) * A_GQA]
        bc = jnp.where(((dist_c >= 0) & (dist_c < A_WINDOW))[None], _rel_bias(tab, dist_c), NEG)
        bn = jnp.where(((dist_n >= 0) & (jnp.arange(BLK)[None, :] < DEC_SEQ))[None], _rel_bias(tab, dist_n), NEG)
        out.append((bc.reshape(A_GQA * DEC_SEQ, A_WINDOW), bn.reshape(A_GQA * DEC_SEQ, BLK)))
    return out


def _dil_sample_bias_mask(table):
    s = jnp.arange(DEC_SEQ)[:, None]
    bcs, bns = [], []
    for g, (win, dil) in enumerate(D_PAIRS):
        lo = A_Q_HEADS + g * D_HEADS_PER
        tab = table[:, lo:lo + D_HEADS_PER]
        if g == 2:
            key = jnp.arange(D_G2_RES * (win // 16))
            row = 16 * (key % (win // 16)) + key // (win // 16)
        else:
            row = jnp.arange(win)
        dist_c = win + s - row[None, :]
        valid_c = (dist_c >= 0) & (dist_c % dil == 0) & (dist_c <= win)
        bc = jnp.where(valid_c[None], _rel_bias(tab, dist_c), NEG)
        dist_n = s - jnp.arange(BLK)[None, :]
        valid_n = (dist_n >= 0) & (dist_n % dil == 0) & (jnp.arange(BLK)[None, :] < DEC_SEQ)
        bn = jnp.where(valid_n[None], _rel_bias(tab, dist_n), NEG)
        bcs.append(bc.reshape(1, D_HEADS_PER * DEC_SEQ, row.shape[0]))
        bns.append(bn.reshape(D_HEADS_PER * DEC_SEQ, BLK))
    return bcs, jnp.stack(bns)


def _time_major(x):
    b, t, c = x.shape
    return x.transpose(1, 0, 2).reshape(t * b, c)


def _batch_major(x, b):
    n, c = x.shape
    return x.reshape(n // b, b, c).transpose(1, 0, 2)


def kernel(x_prompt, x_sample, cache_a_kv, state_b, state_c_h, state_c_conv, cache_d_g0, cache_d_g1, cache_d_g2, rel_table, norm_mix, norm_ffn, norm_final, w_in_even, w_out_even, sinks_a, s5_a_re, s5_a_im, s5_log_dt, s5_b_re, s5_b_im, s5_c_re, s5_c_im, s5_d, s5_glu_w, s5_glu_b, w_in_odd, w_out_odd, conv_w, conv_b, gate_a_w, gate_a_b, gate_x_w, gate_x_b, lru_lambda, moe_router_g, moe_router_g_b, moe_router_e, moe_router_e_b, moe_w_gate, moe_w_up, moe_w_down):
    xp = _time_major(x_prompt.astype(F32))
    xs = x_sample.astype(F32).reshape(NS_TOK, D_MODEL)

    n_even, n_odd = cache_a_kv.shape[0], state_c_h.shape[0]
    cache_a = cache_a_kv.reshape(n_even, DEC_BATCH, A_WINDOW, 2 * A_KV_W)
    caches_d = [c.reshape(n_odd, DEC_BATCH, c.shape[2], 2 * D_GROUP_W)
                for c in (cache_d_g0, cache_d_g1, cache_d_g2)]

    bm_a = _band_bias_mask(rel_table[:, :A_Q_HEADS], A_WINDOW - 1, 1)
    bm_d = [_band_bias_mask(rel_table[:, A_Q_HEADS + g * D_HEADS_PER:A_Q_HEADS + (g + 1) * D_HEADS_PER],
                            win // dil, dil) for g, (win, dil) in enumerate(D_PAIRS)]
    bm_a_s = _swa_sample_bias_mask(rel_table)
    bcs_d_s, bn_d_s = _dil_sample_bias_mask(rel_table)
    bmc_a_s = jnp.concatenate([bm_a_s[0][0], bm_a_s[1][0]], axis=0)
    bmn_a_s = jnp.concatenate([bm_a_s[0][1], bm_a_s[1][1]], axis=0)

    a_p, a_new, b_p, b_s = [], [], [], []
    ch_p, ch_s, cc_p, cc_s = [], [], [], []
    d_p = [[], [], []]
    d_new = [[], [], []]

    for layer in range(DEPTH):
        if layer % 2 == 0:
            e = layer // 2
            w_in = w_in_even[e].astype(BF16)
            w_out = w_out_even[e].astype(BF16)
            splits = (A_Q_W, 2 * A_KV_W, B_WIDTH)
            s5p = _s5_params(s5_a_re[e], s5_a_im[e], s5_log_dt[e], s5_b_re[e], s5_b_im[e],
                             s5_c_re[e], s5_c_im[e], s5_d[e], s5_glu_w[e], s5_glu_b[e])
            sinks = sinks_a[e].astype(F32)

            q_p, kv_p, u_p = _norm_proj(xp, norm_mix[layer], w_in, splits, f"in_proj_p{layer}")
            q_s, kv_s, u_s = _norm_proj(xs, norm_mix[layer], w_in, splits, f"in_proj_s{layer}")

            oa_p = _swa_prompt(q_p, kv_p, sinks, bm_a)
            oa_s = _swa_sample(q_s, kv_s, cache_a, e, sinks, bmc_a_s, bmn_a_s)

            ob_p, hl_p = _s5_mixer(u_p, jnp.zeros((BATCH, B_STATE_W), F32), s5p, BATCH, 64, "s5_prompt")
            u_s_tm = _time_major(u_s.reshape(DEC_BATCH, DEC_SEQ, B_WIDTH))
            ob_s_tm, hl_s = _s5_mixer(u_s_tm, _s5_state_to_cols(state_b[e]), s5p, DEC_BATCH, DEC_SEQ,
                                      "s5_sample")
            ob_s = _batch_major(ob_s_tm, DEC_BATCH).reshape(NS_TOK, B_WIDTH)

            xp = _out_proj(_out_proj_even_kernel, xp, [oa_p, ob_p], w_out, f"out_proj_p{layer}")
            xs = _out_proj(_out_proj_even_kernel, xs, [oa_s, ob_s], w_out, f"out_proj_s{layer}")

            kv_last = kv_p[(SEQ - A_WINDOW) * BATCH:].reshape(A_WINDOW, BATCH, 2, A_KV_HEADS, HEAD_DIM)
            a_p.append(kv_last.transpose(1, 0, 2, 3, 4))
            a_new.append(kv_s.reshape(DEC_BATCH, DEC_SEQ, 2 * A_KV_W))
            b_p.append(_s5_cols_to_state(hl_p))
            b_s.append(_s5_cols_to_state(hl_s))
        else:
            o = layer // 2
            w_in = w_in_odd[o].astype(BF16)
            w_out = w_out_odd[o].astype(BF16)
            splits = (C_WIDTH, C_WIDTH, D_QKV_W, D_QKV_W, D_QKV_W)
            lrp = _rglru_params(conv_w[o], conv_b[o], gate_a_w[o], gate_a_b[o], gate_x_w[o], gate_x_b[o],
                                lru_lambda[o])

            xr_p, gate_p, q_p, k_p, v_p = _norm_proj(xp, norm_mix[layer], w_in, splits, f"in_proj_p{layer}")
            xr_s, gate_s, q_s, k_s, v_s = _norm_proj(xs, norm_mix[layer], w_in, splits, f"in_proj_s{layer}")

            od_p, lse_p = [], []
            for g, (win, dil) in enumerate(D_PAIRS):
                og, lg = _dilated_prompt_group(q_p, k_p, v_p, bm_d[g], g, dil)
                od_p.append(og)
                lse_p.append(lg)
            od_s, lse_s = _dilated_sample(q_s, k_s, v_s, caches_d[0], caches_d[1], caches_d[2], o,
                                          bcs_d_s, bn_d_s)

            oc_p, hc_p = _rglru_mixer(xr_p, gate_p, jnp.zeros(((C_CONV - 1) * BATCH, C_WIDTH), F32),
                                      jnp.zeros((BATCH, C_WIDTH), F32), lrp, BATCH, 128, "rglru_prompt")
            to_tm = lambda x: _time_major(x.reshape(DEC_BATCH, DEC_SEQ, C_WIDTH))
            conv_s = state_c_conv[o].astype(F32).transpose(1, 0, 2).reshape((C_CONV - 1) * DEC_BATCH, C_WIDTH)
            oc_s_tm, hc_s = _rglru_mixer(to_tm(xr_s), to_tm(gate_s), conv_s, state_c_h[o].astype(F32), lrp,
                                         DEC_BATCH, DEC_SEQ, "rglru_sample")
            oc_s = _batch_major(oc_s_tm, DEC_BATCH).reshape(NS_TOK, C_WIDTH)

            xp = _out_proj(_out_proj_odd_kernel, xp, [oc_p] + od_p + lse_p, w_out, f"out_proj_p{layer}")
            xs = _out_proj(_out_proj_odd_kernel, xs, [oc_s] + list(od_s) + list(lse_s), w_out,
                           f"out_proj_s{layer}")

            ch_p.append(hc_p)
            ch_s.append(hc_s)
            xr_p3 = xr_p.reshape(SEQ, BATCH, C_WIDTH)
            cc_p.append(xr_p3[SEQ - (C_CONV - 1):].transpose(1, 0, 2))
            cc_s.append(xr_s.reshape(DEC_BATCH, DEC_SEQ, C_WIDTH)[:, DEC_SEQ - (C_CONV - 1):])
            k_p4 = k_p.reshape(SEQ, BATCH, D_N_GROUPS, D_GROUP_W)
            v_p4 = v_p.reshape(SEQ, BATCH, D_N_GROUPS, D_GROUP_W)
            k_s4 = k_s.reshape(DEC_BATCH, DEC_SEQ, D_N_GROUPS, D_GROUP_W)
            v_s4 = v_s.reshape(DEC_BATCH, DEC_SEQ, D_N_GROUPS, D_GROUP_W)
            for g, (win, dil) in enumerate(D_PAIRS):
                wc = min(win, SEQ)
                kg = k_p4[SEQ - wc:, :, g].transpose(1, 0, 2)
                vg = v_p4[SEQ - wc:, :, g].transpose(1, 0, 2)
                d_p[g].append(jnp.stack([kg, vg], axis=2).reshape(BATCH, wc, 2, D_HEADS_PER, HEAD_DIM))
                d_new[g].append(jnp.concatenate([k_s4[:, :, g], v_s4[:, :, g]], axis=-1))

        mp = _moe_params(norm_ffn[layer], moe_router_g[layer], moe_router_g_b[layer], moe_router_e[layer],
                         moe_router_e_b[layer], moe_w_gate[layer], moe_w_up[layer], moe_w_down[layer])
        xp = _moe(xp, mp, f"moe_p{layer}")
        xs = _moe(xs, mp, f"moe_s{layer}")

    y_prompt = _batch_major(_final_norm(xp, norm_final, "final_norm_p"), BATCH)
    y_sample = _final_norm(xs, norm_final, "final_norm_s").reshape(DEC_BATCH, DEC_SEQ, D_MODEL)

    new_a = _shift_cache(cache_a, jnp.stack(a_new), "shift_cache_a").reshape(cache_a_kv.shape)
    new_d = [_shift_cache(caches_d[g], jnp.stack(d_new[g]), f"shift_cache_d{g}").reshape(c.shape)
             for g, c in enumerate((cache_d_g0, cache_d_g1, cache_d_g2))]

    return (y_prompt, y_sample,
            jnp.stack(a_p), new_a, jnp.stack(b_p), jnp.stack(b_s),
            jnp.stack(ch_p), jnp.stack(ch_s), jnp.stack(cc_p), jnp.stack(cc_s),
            jnp.stack(d_p[0]), new_d[0], jnp.stack(d_p[1]), new_d[1], jnp.stack(d_p[2]), new_d[2])
```

```python
import functools
import math

import jax
import jax.numpy as jnp
from jax import lax
from jax.experimental import pallas as pl
from jax.experimental.pallas import tpu as pltpu

F32 = jnp.float32
BF16 = jnp.bfloat16

D_MODEL = 1024
BATCH = 8
SEQ = 2048
DEPTH = 4
DEC_BATCH = 128
DEC_SEQ = 8
HEAD_DIM = 64
BLK = 128
RMS_EPS = 1e-6
NEG = -1e30

A_Q_HEADS = 8
A_KV_HEADS = 2
A_GQA = 4
A_WINDOW = 128
A_Q_W = A_Q_HEADS * HEAD_DIM
A_KV_W = A_KV_HEADS * HEAD_DIM

B_WIDTH = 512
B_GROUP_CH = 16
B_GROUPS = 32
B_STATE = 64
B_LANE_BLOCKS = 4
B_GROUPS_PER_BLOCK = B_GROUPS // B_LANE_BLOCKS
B_HALF = B_GROUPS_PER_BLOCK * B_STATE
B_STATE_W = B_LANE_BLOCKS * 2 * B_HALF

C_WIDTH = 512
C_BLOCKS = 8
C_BLOCK_W = 64
C_CONV = 4
C_POWER = 8.0

D_PAIRS = ((128, 1), (512, 4), (2048, 16))
D_N_GROUPS = 3
D_HEADS_PER = 4
D_GROUP_W = D_HEADS_PER * HEAD_DIM
D_QKV_W = D_N_GROUPS * D_GROUP_W

T5_BUCKETS = 32
T5_MAX_DIST = 2048

MOE_GROUPS = 4
MOE_PER_GROUP = 4
MOE_EXPERTS = 16
MOE_FF = 256
ROUTER_LANES = 128

NP_TOK = BATCH * SEQ
NS_TOK = DEC_BATCH * DEC_SEQ

VMEM_LIMIT_BYTES = 52 * 2 ** 20


def _cparams(*sem):
    return pltpu.CompilerParams(dimension_semantics=sem, vmem_limit_bytes=VMEM_LIMIT_BYTES)


def _nt_dot(a, b):
    return lax.dot_general(a, b, (((1,), (1,)), ((), ())), preferred_element_type=F32)


def _norm_proj_kernel(x_ref, g_ref, w_ref, *out_refs, splits):
    x = x_ref[...]
    y = x * lax.rsqrt(jnp.mean(x * x, axis=-1, keepdims=True) + RMS_EPS)
    xn = (y * g_ref[...]).astype(BF16)
    off = 0
    for o_ref, width in zip(out_refs, splits):
        o_ref[...] = jnp.dot(xn, w_ref[:, off:off + width], preferred_element_type=F32)
        off += width


def _norm_proj(x, g, w, splits, name):
    n = x.shape[0]
    tm = 512
    return pl.pallas_call(
        functools.partial(_norm_proj_kernel, splits=splits),
        grid=(n // tm,),
        in_specs=[pl.BlockSpec((tm, D_MODEL), lambda i: (i, 0)),
                  pl.BlockSpec((1, D_MODEL), lambda i: (0, 0)),
                  pl.BlockSpec(w.shape, lambda i: (0, 0))],
        out_specs=[pl.BlockSpec((tm, s), lambda i: (i, 0)) for s in splits],
        out_shape=[jax.ShapeDtypeStruct((n, s), F32) for s in splits],
        compiler_params=_cparams("parallel"),
        name=name,
    )(x, g.reshape(1, D_MODEL), w)


def _softmax_parts(scores, sink):
    m = jnp.max(scores[0], axis=1, keepdims=True)
    for s in scores[1:]:
        m = jnp.maximum(m, jnp.max(s, axis=1, keepdims=True))
    if sink is not None:
        m = jnp.maximum(m, sink)
    ps = [jnp.exp(s - m) for s in scores]
    den = jnp.sum(ps[0], axis=1, keepdims=True)
    for p in ps[1:]:
        den = den + jnp.sum(p, axis=1, keepdims=True)
    if sink is not None:
        den = den + jnp.exp(sink - m)
    inv = 1.0 / den
    return [(p * inv).astype(BF16) for p in ps], den, m


def _band_attn_kernel(*refs, n_heads, gqa, with_sink, dil, has_prev):
    refs = list(refs)
    sink_ref = refs.pop(0) if with_sink else None
    q_ref, kc_ref = refs.pop(0), refs.pop(0)
    kp_ref = refs.pop(0) if has_prev else None
    vc_ref = refs.pop(0)
    vp_ref = refs.pop(0) if has_prev else None
    bm_ref, o_ref = refs.pop(0), refs.pop(0)
    lse_ref = refs.pop(0) if refs else None
    for r in range(dil):
        rows = slice(None) if dil == 1 else pl.ds(r, BLK, stride=dil)
        q = (q_ref[rows, :] * HEAD_DIM ** -0.5).astype(BF16)
        kc = kc_ref[rows, :].astype(BF16)
        vc = vc_ref[rows, :].astype(BF16)
        if has_prev:
            kp = kp_ref[rows, :].astype(BF16)
            vp = vp_ref[rows, :].astype(BF16)
        outs, lses = [], []
        for h in range(n_heads):
            hk = h // gqa
            ksl = slice(hk * HEAD_DIM, (hk + 1) * HEAD_DIM)
            qh = q[:, h * HEAD_DIM:(h + 1) * HEAD_DIM]
            scores = [_nt_dot(qh, kc[:, ksl]) + bm_ref[0, h, :, BLK:2 * BLK]]
            values = [vc[:, ksl]]
            if has_prev:
                scores.append(_nt_dot(qh, kp[:, ksl]) + bm_ref[0, h, :, 0:BLK])
                values.append(vp[:, ksl])
            sink = sink_ref[h] if with_sink else None
            ps, den, m = _softmax_parts(scores, sink)
            o = jnp.dot(ps[0], values[0], preferred_element_type=F32)
            for p, v in zip(ps[1:], values[1:]):
                o = o + jnp.dot(p, v, preferred_element_type=F32)
            outs.append(o)
            if lse_ref is not None:
                lses.append(jnp.broadcast_to(m + jnp.log(den), (BLK, HEAD_DIM)))
        o_ref[rows, :] = jnp.concatenate(outs, axis=1)
        if lse_ref is not None:
            lse_ref[rows, :] = jnp.concatenate(lses, axis=1)


def _swa_prompt(q, kv, sinks, bm):
    nblk = SEQ // BLK
    row = lambda b, i: b * nblk + i
    prev = lambda b, i: b * nblk + jnp.maximum(i - 1, 0)
    return pl.pallas_call(
        functools.partial(_band_attn_kernel, n_heads=A_Q_HEADS, gqa=A_GQA, with_sink=True, dil=1,
                          has_prev=True),
        grid=(BATCH, nblk),
        in_specs=[pl.BlockSpec(memory_space=pltpu.SMEM),
                  pl.BlockSpec((BLK, A_Q_W), lambda b, i: (row(b, i), 0)),
                  pl.BlockSpec((BLK, A_KV_W), lambda b, i: (row(b, i), 0)),
                  pl.BlockSpec((BLK, A_KV_W), lambda b, i: (prev(b, i), 0)),
                  pl.BlockSpec((BLK, A_KV_W), lambda b, i: (row(b, i), 1)),
                  pl.BlockSpec((BLK, A_KV_W), lambda b, i: (prev(b, i), 1)),
                  pl.BlockSpec((1, A_Q_HEADS, BLK, 2 * BLK), lambda b, i: (jnp.minimum(i, 1), 0, 0, 0))],
        out_specs=pl.BlockSpec((BLK, A_Q_W), lambda b, i: (row(b, i), 0)),
        out_shape=jax.ShapeDtypeStruct((NP_TOK, A_Q_W), F32),
        compiler_params=_cparams("parallel", "parallel"),
        name="swa_prompt",
    )(sinks, q, kv, kv, kv, kv, bm)


def _dilated_prompt_group(q, k, v, bm, g, dil):
    rows = BLK * dil
    nchunk = SEQ // rows
    has_prev = nchunk > 1
    pair = LANES // HEAD_DIM
    npair = D_HEADS_PER // pair
    row = lambda b, i: b * nchunk + i
    prev = lambda b, i: b * nchunk + jnp.maximum(i - 1, 0)
    cur_spec = pl.BlockSpec((rows, LANES), lambda b, i, p: (row(b, i), g * npair + p))
    prev_spec = pl.BlockSpec((rows, LANES), lambda b, i, p: (prev(b, i), g * npair + p))
    out_spec = pl.BlockSpec((rows, LANES), lambda b, i, p: (row(b, i), p))
    bm_spec = pl.BlockSpec((1, pair, BLK, 2 * BLK), lambda b, i, p: (jnp.minimum(i, 1), p, 0, 0))
    if has_prev:
        in_specs, args = [cur_spec, cur_spec, prev_spec, cur_spec, prev_spec, bm_spec], (q, k, k, v, v, bm)
    else:
        in_specs, args = [cur_spec, cur_spec, cur_spec, bm_spec], (q, k, v, bm)
    return pl.pallas_call(
        functools.partial(_band_attn_kernel, n_heads=pair, gqa=1, with_sink=False, dil=dil,
                          has_prev=has_prev),
        grid=(BATCH, nchunk, npair),
        in_specs=in_specs,
        out_specs=[out_spec, out_spec],
        out_shape=[jax.ShapeDtypeStruct((NP_TOK, D_GROUP_W), F32)] * 2,
        compiler_params=_cparams("parallel", "parallel", "parallel"),
        name=f"dilated_prompt_g{g}",
    )(*args)


SWA_SAMPLE_BB = 8


def _pad_rows(x, rows):
    return jnp.concatenate([x, jnp.zeros((rows - x.shape[0], x.shape[1]), x.dtype)], axis=0)


def _swa_sample_kernel(sink_ref, q_ref, kv_ref, c_ref, bmc_ref, bmn_ref, o_ref):
    for bb in range(SWA_SAMPLE_BB):
        rs = slice(bb * DEC_SEQ, (bb + 1) * DEC_SEQ)
        q = (q_ref[rs, :] * HEAD_DIM ** -0.5).astype(BF16)
        kvn = _pad_rows(kv_ref[rs, :], BLK).astype(BF16)
        cache = c_ref[0, bb].astype(BF16)
        outs = [None] * A_Q_HEADS
        for hk in range(A_KV_HEADS):
            heads = range(hk * A_GQA, (hk + 1) * A_GQA)
            qs = jnp.concatenate([q[:, h * HEAD_DIM:(h + 1) * HEAD_DIM] for h in heads], axis=0)
            ksl = slice(hk * HEAD_DIM, (hk + 1) * HEAD_DIM)
            vsl = slice(A_KV_W + hk * HEAD_DIM, A_KV_W + (hk + 1) * HEAD_DIM)
            brow = slice(hk * A_GQA * DEC_SEQ, (hk + 1) * A_GQA * DEC_SEQ)
            s1 = jnp.dot(qs, cache[ksl, :], preferred_element_type=F32) + bmc_ref[brow, :]
            s2 = _nt_dot(qs, kvn[:, ksl]) + bmn_ref[brow, :]
            sink = jnp.concatenate(
                [jnp.full((DEC_SEQ, 1), sink_ref[h], F32) for h in heads], axis=0)
            (p1, p2), den, _ = _softmax_parts([s1, s2], sink)
            o = _nt_dot(p1, cache[vsl, :]) + jnp.dot(p2, kvn[:, vsl], preferred_element_type=F32)
            for j, h in enumerate(heads):
                outs[h] = o[j * DEC_SEQ:(j + 1) * DEC_SEQ, :]
        o_ref[rs, :] = jnp.concatenate(outs, axis=1)


def _swa_sample(q, kv, cache_all, layer, sinks, bmc, bmn):
    bb = SWA_SAMPLE_BB
    return pl.pallas_call(
        _swa_sample_kernel,
        grid=(DEC_BATCH // bb,),
        in_specs=[pl.BlockSpec(memory_space=pltpu.SMEM),
                  pl.BlockSpec((bb * DEC_SEQ, A_Q_W), lambda i: (i, 0)),
                  pl.BlockSpec((bb * DEC_SEQ, 2 * A_KV_W), lambda i: (i, 0)),
                  pl.BlockSpec((1, bb, 2 * A_KV_W, A_WINDOW), lambda i: (layer, i, 0, 0)),
                  pl.BlockSpec(bmc.shape, lambda i: (0, 0)),
                  pl.BlockSpec(bmn.shape, lambda i: (0, 0))],
        out_specs=pl.BlockSpec((bb * DEC_SEQ, A_Q_W), lambda i: (i, 0)),
        out_shape=jax.ShapeDtypeStruct((NS_TOK, A_Q_W), F32),
        compiler_params=_cparams("parallel"),
        name="swa_sample",
    )(sinks, q, kv, cache_all, bmc, bmn)


def _dil_sample_kernel(q_ref, k_ref, v_ref, c0_ref, c1_ref, c2_ref,
                       bc0_ref, bc1_ref, bc2_ref, bn_ref,
                       o0_ref, o1_ref, o2_ref, l0_ref, l1_ref, l2_ref):
    q = q_ref[...] * HEAD_DIM ** -0.5
    k = k_ref[...]
    v = v_ref[...]
    nrow = D_HEADS_PER * DEC_SEQ
    row_head = lax.broadcasted_iota(jnp.int32, (nrow, D_GROUP_W), 0) // DEC_SEQ
    lane_head = lax.broadcasted_iota(jnp.int32, (nrow, D_GROUP_W), 1) // HEAD_DIM
    head_mask = row_head == lane_head
    out_lane_head = lax.broadcasted_iota(jnp.int32, (DEC_SEQ, D_GROUP_W), 1) // HEAD_DIM
    groups = ((c0_ref, bc0_ref, o0_ref, l0_ref), (c1_ref, bc1_ref, o1_ref, l1_ref),
              (c2_ref, bc2_ref, o2_ref, l2_ref))
    for g, (c_ref, bc_ref, o_ref, l_ref) in enumerate(groups):
        gsl = slice(g * D_GROUP_W, (g + 1) * D_GROUP_W)
        qbd = jnp.where(head_mask, jnp.concatenate([q[:, gsl]] * D_HEADS_PER, axis=0), 0.0).astype(BF16)
        kn = _pad_rows(k[:, gsl], BLK).astype(BF16)
        vn = _pad_rows(v[:, gsl], BLK).astype(BF16)
        kt = c_ref[0, 0, 0:D_GROUP_W, :].astype(BF16)
        vt = c_ref[0, 0, D_GROUP_W:2 * D_GROUP_W, :].astype(BF16)
        s1 = jnp.dot(qbd, kt, preferred_element_type=F32) + bc_ref[0]
        s2 = _nt_dot(qbd, kn) + bn_ref[g]
        (p1, p2), den, m = _softmax_parts([s1, s2], None)
        of = _nt_dot(p1, vt) + jnp.dot(p2, vn, preferred_element_type=F32)
        lse = m + jnp.log(den)
        og = jnp.zeros((DEC_SEQ, D_GROUP_W), F32)
        lg = jnp.zeros((DEC_SEQ, D_GROUP_W), F32)
        for h in range(D_HEADS_PER):
            rs = slice(h * DEC_SEQ, (h + 1) * DEC_SEQ)
            sel = out_lane_head == h
            og = og + jnp.where(sel, of[rs, :], 0.0)
            lg = lg + jnp.where(sel, lse[rs, :], 0.0)
        o_ref[...] = og
        l_ref[...] = lg


def _dilated_sample(q, k, v, caches, layer, bcs, bn):
    tok_spec = pl.BlockSpec((DEC_SEQ, D_QKV_W), lambda b: (b, 0))
    out_spec = pl.BlockSpec((DEC_SEQ, D_GROUP_W), lambda b: (b, 0))
    outs = pl.pallas_call(
        _dil_sample_kernel,
        grid=(DEC_BATCH,),
        in_specs=[tok_spec, tok_spec, tok_spec]
        + [pl.BlockSpec((1, 1, 2 * D_GROUP_W, win), lambda b: (layer, b, 0, 0)) for win, _ in D_PAIRS]
        + [pl.BlockSpec(bc.shape, lambda b: (0, 0, 0)) for bc in bcs]
        + [pl.BlockSpec(bn.shape, lambda b: (0, 0, 0))],
        out_specs=[out_spec] * 6,
        out_shape=[jax.ShapeDtypeStruct((NS_TOK, D_GROUP_W), F32)] * 6,
        compiler_params=_cparams("parallel"),
        name="dilated_sample",
    )(q, k, v, *caches, *bcs, bn)
    return outs[:3], outs[3:]


def _shift_kernel(c_ref, n_ref, o_ref, *, wc, bb):
    width = c_ref.shape[2]
    lane = lax.broadcasted_iota(jnp.int32, (width, BLK), 1)
    for i in range(bb):
        shifted = pltpu.roll(c_ref[0, i], wc - DEC_SEQ, 1)
        new_t = jnp.concatenate([jnp.zeros((BLK - DEC_SEQ, width), F32), n_ref[0, i]], axis=0).T
        if wc > BLK:
            o_ref[0, i, :, 0:wc - BLK] = shifted[:, 0:wc - BLK]
        o_ref[0, i, :, wc - BLK:wc] = jnp.where(lane >= BLK - DEC_SEQ, new_t, shifted[:, wc - BLK:wc])


def _shift_cache(cache, new_rows, name):
    n_layers, _, width, wc = cache.shape
    bb = max(1, (4 * 2 ** 20) // (wc * width * 4))
    return pl.pallas_call(
        functools.partial(_shift_kernel, wc=wc, bb=bb),
        grid=(n_layers, DEC_BATCH // bb),
        in_specs=[pl.BlockSpec((1, bb, width, wc), lambda l, i: (l, i, 0, 0)),
                  pl.BlockSpec((1, bb, DEC_SEQ, width), lambda l, i: (l, i, 0, 0))],
        out_specs=pl.BlockSpec((1, bb, width, wc), lambda l, i: (l, i, 0, 0)),
        out_shape=jax.ShapeDtypeStruct(cache.shape, cache.dtype),
        compiler_params=_cparams("parallel", "parallel"),
        name=name,
    )(cache, new_rows)


LANES = 128


def _to_time_major(src_ref, dst_ref, off, bt, lc):
    for b in range(bt):
        x = src_ref[b]
        for j in range(dst_ref.shape[0]):
            dst_ref[j, pl.ds(off + b, lc, stride=bt), :] = x[:, j * LANES:(j + 1) * LANES]


def _from_time_major(src_ref, dst_ref, bt, lc):
    for b in range(bt):
        dst_ref[b] = jnp.concatenate(
            [src_ref[j, pl.ds(b, lc, stride=bt), :] for j in range(src_ref.shape[0])], axis=1)


def _lane_blocks(ref, r0, nrows):
    return jnp.concatenate([ref[j, r0:r0 + nrows, :] for j in range(ref.shape[0])], axis=1)


def _store_lane_blocks(ref, r0, x):
    for j in range(ref.shape[0]):
        ref[j, r0:r0 + x.shape[0], :] = x[:, j * LANES:(j + 1) * LANES]


def _token_spec(bt, lc, width):
    return pl.BlockSpec((bt, lc, width), lambda c: (0, c, 0))


def _s5_kernel(u_ref, h0_ref, bm_ref, cm_ref, lam_ref, d_ref, gw_ref, gb_ref, o_ref, hl_ref,
               us_ref, hs_ref, *, bt, lc):
    rows = lc * bt
    blk_w = 2 * B_HALF

    @pl.when(pl.program_id(0) == 0)
    def _():
        hs_ref[0:bt, :] = h0_ref[...]

    _to_time_major(u_ref, us_ref, 0, bt, lc)
    u = _lane_blocks(us_ref, 0, rows)
    ub = u.astype(BF16)
    for j in range(B_LANE_BLOCKS):
        cols = jnp.dot(ub[:, j * LANES:(j + 1) * LANES], bm_ref[j], preferred_element_type=F32)
        hs_ref[bt:, j * blk_w:j * blk_w + B_HALF] = cols[:, :B_HALF]
        hs_ref[bt:, j * blk_w + B_HALF:(j + 1) * blk_w] = cols[:, :B_HALF] + cols[:, B_HALF:]

    def step(t, carry):
        r0 = pl.multiple_of(t * bt, bt)
        for j in range(B_LANE_BLOCKS):
            re = slice(j * blk_w, j * blk_w + B_HALF)
            im = slice(j * blk_w + B_HALF, (j + 1) * blk_w)
            lr = lam_ref[2 * j:2 * j + 1, :]
            li = lam_ref[2 * j + 1:2 * j + 2, :]
            pr = hs_ref[pl.ds(r0, bt), re]
            pi = hs_ref[pl.ds(r0, bt), im]
            hs_ref[pl.ds(r0 + bt, bt), re] = lr * pr - li * pi + hs_ref[pl.ds(r0 + bt, bt), re]
            hs_ref[pl.ds(r0 + bt, bt), im] = lr * pi + li * pr + hs_ref[pl.ds(r0 + bt, bt), im]
        return carry

    lax.fori_loop(0, lc, step, 0)

    ys = []
    for j in range(B_LANE_BLOCKS):
        h_re = hs_ref[bt:, j * blk_w:j * blk_w + B_HALF]
        h_im = hs_ref[bt:, j * blk_w + B_HALF:(j + 1) * blk_w]
        lhs = jnp.concatenate([h_re + h_im, h_im], axis=1).astype(BF16)
        ys.append(jnp.dot(lhs, cm_ref[j], preferred_element_type=F32))
    y = jax.nn.gelu(jnp.concatenate(ys, axis=1) + d_ref[...] * u)
    z = jnp.dot(y.astype(BF16), gw_ref[...], preferred_element_type=F32) + gb_ref[...]
    _store_lane_blocks(us_ref, 0, y * jax.nn.sigmoid(z))
    _from_time_major(us_ref, o_ref, bt, lc)
    last = hs_ref[rows:rows + bt, :]
    hl_ref[...] = last
    hs_ref[0:bt, :] = last


def _s5_mixer(u, h0, prm, bt, seq, lc, name):
    tok = _token_spec(bt, lc, B_WIDTH)
    u_in = u.reshape(bt, seq, B_WIDTH)
    full = lambda a: pl.BlockSpec(a.shape, lambda c: (0,) * a.ndim)
    out, hl = pl.pallas_call(
        functools.partial(_s5_kernel, bt=bt, lc=lc),
        grid=(seq // lc,),
        in_specs=[tok, full(h0), full(prm["bm"]), full(prm["cm"]), full(prm["lam"]), full(prm["d"]),
                  full(prm["glu_w"]), full(prm["glu_b"])],
        out_specs=[tok, pl.BlockSpec((bt, B_STATE_W), lambda c: (0, 0))],
        out_shape=[jax.ShapeDtypeStruct(u_in.shape, F32), jax.ShapeDtypeStruct((bt, B_STATE_W), F32)],
        scratch_shapes=[pltpu.VMEM((B_WIDTH // LANES, lc * bt, LANES), F32),
                        pltpu.VMEM((lc * bt + bt, B_STATE_W), F32)],
        compiler_params=_cparams("arbitrary"),
        name=name,
    )(u_in, h0, prm["bm"], prm["cm"], prm["lam"], prm["d"], prm["glu_w"], prm["glu_b"])
    return out.reshape(bt * seq, B_WIDTH), hl


def _s5_params(a_re, a_im, log_dt, b_re, b_im, c_re, c_im, d, glu_w, glu_b):
    lam = lax.complex(a_re.astype(F32), a_im.astype(F32))
    dt = jnp.exp(log_dt.astype(F32))[:, None]
    lam_bar = jnp.exp(lam * dt)
    b_bar = ((lam_bar - 1.0) / lam)[..., None] * lax.complex(b_re.astype(F32), b_im.astype(F32))
    nb, gb = B_LANE_BLOCKS, B_GROUPS_PER_BLOCK
    eye = jnp.eye(gb, dtype=F32)

    def in_mat(part):
        p = part.reshape(nb, gb, B_STATE, B_GROUP_CH)
        return jnp.einsum("jgnc,gh->jgchn", p, eye).reshape(nb, gb * B_GROUP_CH, gb * B_STATE)

    def out_mat(part):
        p = part.reshape(nb, gb, B_GROUP_CH, B_STATE)
        return jnp.einsum("jgcn,gh->jgnhc", p, eye).reshape(nb, gb * B_STATE, gb * B_GROUP_CH)

    bm = jnp.concatenate([in_mat(b_bar.real), in_mat(b_bar.imag - b_bar.real)], axis=2).astype(BF16)
    c_re32, c_im32 = c_re.astype(F32), c_im.astype(F32)
    cm = jnp.concatenate([out_mat(c_re32), out_mat(-(c_re32 + c_im32))], axis=1).astype(BF16)
    lam_rows = jnp.stack([lam_bar.real.reshape(nb, B_HALF), lam_bar.imag.reshape(nb, B_HALF)],
                         axis=1).reshape(2 * nb, B_HALF)
    return {"bm": bm, "cm": cm, "lam": lam_rows, "d": d.astype(F32).reshape(1, B_WIDTH),
            "glu_w": glu_w.astype(BF16), "glu_b": glu_b.astype(F32).reshape(1, B_WIDTH)}


def _s5_state_to_cols(state):
    bt = state.shape[0]
    s = state.astype(F32).reshape(bt, B_LANE_BLOCKS, B_GROUPS_PER_BLOCK, B_STATE, 2)
    return s.transpose(0, 1, 4, 2, 3).reshape(bt, B_STATE_W)


def _s5_cols_to_state(cols):
    bt = cols.shape[0]
    s = cols.reshape(bt, B_LANE_BLOCKS, 2, B_GROUPS_PER_BLOCK, B_STATE)
    return s.transpose(0, 1, 3, 4, 2).reshape(bt, B_GROUPS, B_STATE, 2)


def _rglru_kernel(xr_ref, gate_ref, cb_ref, h0_ref, cw_ref, cbias_ref, wa_ref, ba_ref, wx_ref, bx_ref,
                  nsp_ref, o_ref, hl_ref, xp_ref, gs_ref, a_ref, hs_ref, *, bt, lc):
    rows = lc * bt
    pad = (C_CONV - 1) * bt

    @pl.when(pl.program_id(0) == 0)
    def _():
        _store_lane_blocks(xp_ref, 0, cb_ref[...])
        hs_ref[0:bt, :] = h0_ref[...]

    _to_time_major(xr_ref, xp_ref, pad, bt, lc)
    _to_time_major(gate_ref, gs_ref, 0, bt, lc)
    xc = _lane_blocks(xp_ref, 0, rows) * cw_ref[0:1, :]
    for tap in range(1, C_CONV):
        xc = xc + _lane_blocks(xp_ref, tap * bt, rows) * cw_ref[tap:tap + 1, :]
    xcf = xc + cbias_ref[...]
    xb = xcf.astype(BF16)
    r = jax.nn.sigmoid(jnp.dot(xb, wa_ref[...], preferred_element_type=F32) + ba_ref[...])
    i = jax.nn.sigmoid(jnp.dot(xb, wx_ref[...], preferred_element_type=F32) + bx_ref[...])
    log_a = nsp_ref[...] * r
    a = jnp.exp(log_a)
    a_ref[...] = a
    hs_ref[bt:, :] = jnp.sqrt(-jnp.tanh(log_a) * (a * a + 1.0)) * (i * xcf)

    def step(t, carry):
        r0 = pl.multiple_of(t * bt, bt)
        hs_ref[pl.ds(r0 + bt, bt), :] = (a_ref[pl.ds(r0, bt), :] * hs_ref[pl.ds(r0, bt), :]
                                         + hs_ref[pl.ds(r0 + bt, bt), :])
        return carry

    lax.fori_loop(0, lc, step, 0)

    _store_lane_blocks(gs_ref, 0, hs_ref[bt:, :] * jax.nn.gelu(_lane_blocks(gs_ref, 0, rows)))
    _from_time_major(gs_ref, o_ref, bt, lc)
    last = hs_ref[rows:rows + bt, :]
    hl_ref[...] = last
    hs_ref[0:bt, :] = last
    _store_lane_blocks(xp_ref, 0, _lane_blocks(xp_ref, rows, pad))


def _rglru_mixer(xr, gate, conv_buf, h0, prm, bt, seq, lc, name):
    rows = lc * bt
    tok = _token_spec(bt, lc, C_WIDTH)
    shape3 = lambda x: x.reshape(bt, seq, C_WIDTH)
    full = lambda a: pl.BlockSpec(a.shape, lambda c: (0,) * a.ndim)
    names = ("conv_w", "conv_b", "wa", "ba", "wx", "bx", "nsp")
    out, hl = pl.pallas_call(
        functools.partial(_rglru_kernel, bt=bt, lc=lc),
        grid=(seq // lc,),
        in_specs=[tok, tok, full(conv_buf), full(h0)] + [full(prm[k]) for k in names],
        out_specs=[tok, pl.BlockSpec((bt, C_WIDTH), lambda c: (0, 0))],
        out_shape=[jax.ShapeDtypeStruct(shape3(xr).shape, F32), jax.ShapeDtypeStruct((bt, C_WIDTH), F32)],
        scratch_shapes=[pltpu.VMEM((C_WIDTH // LANES, rows + (C_CONV - 1) * bt, LANES), F32),
                        pltpu.VMEM((C_WIDTH // LANES, rows, LANES), F32),
                        pltpu.VMEM((rows, C_WIDTH), F32),
                        pltpu.VMEM((rows + bt, C_WIDTH), F32)],
        compiler_params=_cparams("arbitrary"),
        name=name,
    )(shape3(xr), shape3(gate), conv_buf, h0, *[prm[k] for k in names])
    return out.reshape(bt * seq, C_WIDTH), hl


def _rglru_params(conv_w, conv_b, gate_a_w, gate_a_b, gate_x_w, gate_x_b, lru_lambda):
    eye = jnp.eye(C_BLOCKS, dtype=F32)

    def block_diag(w):
        return jnp.einsum("njk,nm->njmk", w.astype(F32), eye).reshape(C_WIDTH, C_WIDTH).astype(BF16)

    row = lambda x: x.astype(F32).reshape(1, C_WIDTH)
    return {"conv_w": conv_w.astype(F32), "conv_b": row(conv_b),
            "wa": block_diag(gate_a_w), "ba": row(gate_a_b),
            "wx": block_diag(gate_x_w), "bx": row(gate_x_b),
            "nsp": row(-C_POWER * jax.nn.softplus(-lru_lambda.astype(F32)))}


def _out_proj_even_kernel(x_ref, oa_ref, ob_ref, w_ref, o_ref):
    acc = jnp.dot(oa_ref[...].astype(BF16), w_ref[0:A_Q_W, :], preferred_element_type=F32)
    acc = acc + jnp.dot(ob_ref[...].astype(BF16), w_ref[A_Q_W:, :], preferred_element_type=F32)
    o_ref[...] = x_ref[...] + acc


def _out_proj_odd_kernel(x_ref, oc_ref, o0_ref, o1_ref, o2_ref, l0_ref, l1_ref, l2_ref, w_ref, o_ref):
    l0, l1, l2 = l0_ref[...], l1_ref[...], l2_ref[...]
    m = jnp.maximum(jnp.maximum(l0, l1), l2)
    e0, e1, e2 = jnp.exp(l0 - m), jnp.exp(l1 - m), jnp.exp(l2 - m)
    od = (o0_ref[...] * e0 + o1_ref[...] * e1 + o2_ref[...] * e2) / (e0 + e1 + e2)
    acc = jnp.dot(oc_ref[...].astype(BF16), w_ref[0:C_WIDTH, :], preferred_element_type=F32)
    acc = acc + jnp.dot(od.astype(BF16), w_ref[C_WIDTH:, :], preferred_element_type=F32)
    o_ref[...] = x_ref[...] + acc


def _out_proj(kernel, x, parts, w, name):
    n = x.shape[0]
    tm = 512
    return pl.pallas_call(
        kernel,
        grid=(n // tm,),
        in_specs=[pl.BlockSpec((tm, D_MODEL), lambda i: (i, 0))]
        + [pl.BlockSpec((tm, p.shape[1]), lambda i: (i, 0)) for p in parts]
        + [pl.BlockSpec(w.shape, lambda i: (0, 0))],
        out_specs=pl.BlockSpec((tm, D_MODEL), lambda i: (i, 0)),
        out_shape=jax.ShapeDtypeStruct((n, D_MODEL), F32),
        input_output_aliases={0: 0},
        compiler_params=_cparams("parallel"),
        name=name,
    )(x, *parts, w)


MOE_TM = 512


def _moe_kernel(x_ref, g_ref, rw_ref, rb_ref, wg_ref, wu_ref, wd_ref, o_ref, xn_ref, gates_ref, acc_ref):
    e = pl.program_id(1)

    @pl.when(e == 0)
    def _():
        x = x_ref[...]
        y = x * lax.rsqrt(jnp.mean(x * x, axis=-1, keepdims=True) + RMS_EPS)
        xn = y * g_ref[...]
        xb0 = xn.astype(BF16)
        xn_ref[...] = xb0
        logits = jnp.dot(xb0, rw_ref[...], preferred_element_type=F32) + rb_ref[...]
        lane = lax.broadcasted_iota(jnp.int32, logits.shape, 1).astype(F32)
        ninf = float("-inf")
        far = float(ROUTER_LANES)
        lg = jnp.where(lane < MOE_GROUPS, logits, ninf)
        gmax = jnp.max(lg, axis=1, keepdims=True)
        g_idx = jnp.min(jnp.where(lg == gmax, lane, far), axis=1, keepdims=True)
        g_w = 1.0 / jnp.sum(jnp.exp(lg - gmax), axis=1, keepdims=True)
        lane_grp = jnp.floor((lane - MOE_GROUPS) * (1.0 / MOE_PER_GROUP))
        in_grp = (lane >= MOE_GROUPS) & (lane < MOE_GROUPS + MOE_EXPERTS) & (lane_grp == g_idx)
        le = jnp.where(in_grp, logits, ninf)
        v1 = jnp.max(le, axis=1, keepdims=True)
        i1 = jnp.min(jnp.where(le == v1, lane, far), axis=1, keepdims=True)
        le2 = jnp.where(lane == i1, ninf, le)
        v2 = jnp.max(le2, axis=1, keepdims=True)
        i2 = jnp.min(jnp.where(le2 == v2, lane, far), axis=1, keepdims=True)
        e2 = jnp.exp(v2 - v1)
        w1 = g_w / (1.0 + e2)
        w2 = g_w * e2 / (1.0 + e2)
        gates_ref[...] = (jnp.where(lane == i1 - MOE_GROUPS, w1, 0.0)
                          + jnp.where(lane == i2 - MOE_GROUPS, w2, 0.0))
        acc_ref[...] = jnp.zeros_like(acc_ref)

    xb = xn_ref[...]
    gates = gates_ref[...]
    lane_i = lax.broadcasted_iota(jnp.int32, gates.shape, 1)
    gate = jnp.sum(jnp.where(lane_i == e, gates, 0.0), axis=1, keepdims=True)
    hg = jnp.dot(xb, wg_ref[0], preferred_element_type=F32)
    hu = jnp.dot(xb, wu_ref[0], preferred_element_type=F32)
    hid = jax.nn.silu(hg) * hu * gate
    acc_ref[...] += jnp.dot(hid.astype(BF16), wd_ref[0], preferred_element_type=F32)

    @pl.when(e == MOE_EXPERTS - 1)
    def _():
        o_ref[...] = x_ref[...] + acc_ref[...]


def _moe(x, prm, name):
    n = x.shape[0]
    tm = MOE_TM
    return pl.pallas_call(
        _moe_kernel,
        grid=(n // tm, MOE_EXPERTS),
        in_specs=[pl.BlockSpec((tm, D_MODEL), lambda i, e: (i, 0)),
                  pl.BlockSpec((1, D_MODEL), lambda i, e: (0, 0)),
                  pl.BlockSpec((D_MODEL, ROUTER_LANES), lambda i, e: (0, 0)),
                  pl.BlockSpec((1, ROUTER_LANES), lambda i, e: (0, 0)),
                  pl.BlockSpec((1, D_MODEL, MOE_FF), lambda i, e: (e, 0, 0)),
                  pl.BlockSpec((1, D_MODEL, MOE_FF), lambda i, e: (e, 0, 0)),
                  pl.BlockSpec((1, MOE_FF, D_MODEL), lambda i, e: (e, 0, 0))],
        out_specs=pl.BlockSpec((tm, D_MODEL), lambda i, e: (i, 0)),
        out_shape=jax.ShapeDtypeStruct((n, D_MODEL), F32),
        scratch_shapes=[pltpu.VMEM((tm, D_MODEL), BF16),
                        pltpu.VMEM((tm, ROUTER_LANES), F32),
                        pltpu.VMEM((tm, D_MODEL), F32)],
        input_output_aliases={0: 0},
        compiler_params=_cparams("parallel", "arbitrary"),
        name=name,
    )(x, prm["g"], prm["rw"], prm["rb"], prm["wg"], prm["wu"], prm["wd"])


def _moe_params(norm_g, router_g, router_g_b, router_e, router_e_b, w_gate, w_up, w_down):
    used = MOE_GROUPS + MOE_EXPERTS
    rw = jnp.concatenate([router_g.astype(F32), router_e.astype(F32),
                          jnp.zeros((D_MODEL, ROUTER_LANES - used), F32)], axis=1)
    rb = jnp.concatenate([router_g_b.astype(F32), router_e_b.astype(F32),
                          jnp.zeros((ROUTER_LANES - used,), F32)]).reshape(1, ROUTER_LANES)
    return {"g": norm_g.astype(F32).reshape(1, D_MODEL), "rw": rw.astype(BF16), "rb": rb,
            "wg": w_gate.astype(BF16), "wu": w_up.astype(BF16), "wd": w_down.astype(BF16)}


def _final_norm_kernel(x_ref, g_ref, o_ref):
    x = x_ref[...]
    y = x * lax.rsqrt(jnp.mean(x * x, axis=-1, keepdims=True) + RMS_EPS)
    o_ref[...] = y * g_ref[...]


def _final_norm(x, g, name):
    n = x.shape[0]
    tm = 512
    return pl.pallas_call(
        _final_norm_kernel,
        grid=(n // tm,),
        in_specs=[pl.BlockSpec((tm, D_MODEL), lambda i: (i, 0)),
                  pl.BlockSpec((1, D_MODEL), lambda i: (0, 0))],
        out_specs=pl.BlockSpec((tm, D_MODEL), lambda i: (i, 0)),
        out_shape=jax.ShapeDtypeStruct((n, D_MODEL), F32),
        compiler_params=_cparams("parallel"),
        name=name,
    )(x, g.astype(F32).reshape(1, D_MODEL))


def _t5_bucket(dist):
    n = jnp.maximum(dist, 0)
    max_exact = T5_BUCKETS // 2
    nf = jnp.maximum(n, 1).astype(F32)
    large = max_exact + (jnp.log(nf / max_exact) / math.log(T5_MAX_DIST / max_exact)
                         * (T5_BUCKETS - max_exact)).astype(jnp.int32)
    return jnp.where(n < max_exact, n, jnp.minimum(large, T5_BUCKETS - 1))


def _rel_bias(table, dist):
    return jnp.moveaxis(jnp.take(table.astype(F32), _t5_bucket(dist), axis=0), -1, 0)


def _band_bias_mask(table, max_dist, dil):
    r = jnp.arange(BLK)[:, None]
    s = jnp.arange(2 * BLK)[None, :]
    dist = BLK + r - s
    valid = (dist >= 0) & (dist <= max_dist)
    bias = _rel_bias(table, dist * dil)
    later = jnp.where(valid[None], bias, NEG)
    first = jnp.where((valid & (s >= BLK))[None], bias, NEG)
    return jnp.stack([first, later])


def _swa_sample_bias_mask(table):
    s = jnp.arange(DEC_SEQ)[:, None]
    col = jnp.arange(BLK)[None, :]
    dist_c = A_WINDOW + s - col
    dist_n = s - col
    tab = table[:, :A_Q_HEADS]
    bc = jnp.where(((dist_c >= 0) & (dist_c < A_WINDOW))[None], _rel_bias(tab, dist_c), NEG)
    bn = jnp.where(((dist_n >= 0) & (col < DEC_SEQ))[None], _rel_bias(tab, dist_n), NEG)
    return bc.reshape(A_Q_HEADS * DEC_SEQ, A_WINDOW), bn.reshape(A_Q_HEADS * DEC_SEQ, BLK)


def _dil_sample_bias_mask(table):
    s = jnp.arange(DEC_SEQ)[:, None]
    col = jnp.arange(BLK)[None, :]
    bcs, bns = [], []
    for g, (win, dil) in enumerate(D_PAIRS):
        lo = A_Q_HEADS + g * D_HEADS_PER
        tab = table[:, lo:lo + D_HEADS_PER]
        dist_c = win + s - jnp.arange(win)[None, :]
        valid_c = (dist_c >= 0) & (dist_c % dil == 0) & (dist_c <= win)
        bc = jnp.where(valid_c[None], _rel_bias(tab, dist_c), NEG)
        dist_n = s - col
        valid_n = (dist_n >= 0) & (dist_n % dil == 0) & (col < DEC_SEQ)
        bn = jnp.where(valid_n[None], _rel_bias(tab, dist_n), NEG)
        bcs.append(bc.reshape(1, D_HEADS_PER * DEC_SEQ, win))
        bns.append(bn.reshape(D_HEADS_PER * DEC_SEQ, BLK))
    return bcs, jnp.stack(bns)


def _native_cache(c):
    n_layers, nb, wc = c.shape[:3]
    return c.transpose(0, 1, 3, 4, 5, 2).reshape(n_layers, nb, -1, wc)


def _logical_cache(c, heads):
    n_layers, nb, _, wc = c.shape
    return c.reshape(n_layers, nb, 2, heads, HEAD_DIM, wc).transpose(0, 1, 5, 2, 3, 4)


def kernel(x_prompt, x_sample, cache_a_kv, state_b, state_c_h, state_c_conv, cache_d_g0, cache_d_g1, cache_d_g2, rel_table, norm_mix, norm_ffn, norm_final, w_in_even, w_out_even, sinks_a, s5_a_re, s5_a_im, s5_log_dt, s5_b_re, s5_b_im, s5_c_re, s5_c_im, s5_d, s5_glu_w, s5_glu_b, w_in_odd, w_out_odd, conv_w, conv_b, gate_a_w, gate_a_b, gate_x_w, gate_x_b, lru_lambda, moe_router_g, moe_router_g_b, moe_router_e, moe_router_e_b, moe_w_gate, moe_w_up, moe_w_down):
    xp = x_prompt.astype(F32).reshape(NP_TOK, D_MODEL)
    xs = x_sample.astype(F32).reshape(NS_TOK, D_MODEL)

    cache_a = _native_cache(cache_a_kv)
    caches_d = [_native_cache(c) for c in (cache_d_g0, cache_d_g1, cache_d_g2)]

    bm_a = _band_bias_mask(rel_table[:, :A_Q_HEADS], A_WINDOW - 1, 1)
    bm_d = [_band_bias_mask(rel_table[:, A_Q_HEADS + g * D_HEADS_PER:A_Q_HEADS + (g + 1) * D_HEADS_PER],
                            win // dil, dil) for g, (win, dil) in enumerate(D_PAIRS)]
    bmc_a_s, bmn_a_s = _swa_sample_bias_mask(rel_table)
    bcs_d_s, bn_d_s = _dil_sample_bias_mask(rel_table)

    a_p, a_new, b_p, b_s = [], [], [], []
    ch_p, ch_s, cc_p, cc_s = [], [], [], []
    d_p = [[], [], []]
    d_new = [[], [], []]

    for layer in range(DEPTH):
        if layer % 2 == 0:
            e = layer // 2
            w_in = w_in_even[e].astype(BF16)
            w_out = w_out_even[e].astype(BF16)
            splits = (A_Q_W, 2 * A_KV_W, B_WIDTH)
            s5p = _s5_params(s5_a_re[e], s5_a_im[e], s5_log_dt[e], s5_b_re[e], s5_b_im[e],
                             s5_c_re[e], s5_c_im[e], s5_d[e], s5_glu_w[e], s5_glu_b[e])
            sinks = sinks_a[e].astype(F32)

            q_p, kv_p, u_p = _norm_proj(xp, norm_mix[layer], w_in, splits, f"in_proj_p{layer}")
            q_s, kv_s, u_s = _norm_proj(xs, norm_mix[layer], w_in, splits, f"in_proj_s{layer}")

            oa_p = _swa_prompt(q_p, kv_p, sinks, bm_a)
            oa_s = _swa_sample(q_s, kv_s, cache_a, e, sinks, bmc_a_s, bmn_a_s)

            ob_p, hl_p = _s5_mixer(u_p, jnp.zeros((BATCH, B_STATE_W), F32), s5p, BATCH, SEQ, 64, "s5_prompt")
            ob_s, hl_s = _s5_mixer(u_s, _s5_state_to_cols(state_b[e]), s5p, DEC_BATCH, DEC_SEQ, DEC_SEQ,
                                   "s5_sample")

            xp = _out_proj(_out_proj_even_kernel, xp, [oa_p, ob_p], w_out, f"out_proj_p{layer}")
            xs = _out_proj(_out_proj_even_kernel, xs, [oa_s, ob_s], w_out, f"out_proj_s{layer}")

            a_p.append(kv_p.reshape(BATCH, SEQ, 2, A_KV_HEADS, HEAD_DIM)[:, SEQ - A_WINDOW:])
            a_new.append(kv_s.reshape(DEC_BATCH, DEC_SEQ, 2 * A_KV_W))
            b_p.append(_s5_cols_to_state(hl_p))
            b_s.append(_s5_cols_to_state(hl_s))
        else:
            o = layer // 2
            w_in = w_in_odd[o].astype(BF16)
            w_out = w_out_odd[o].astype(BF16)
            splits = (C_WIDTH, C_WIDTH, D_QKV_W, D_QKV_W, D_QKV_W)
            lrp = _rglru_params(conv_w[o], conv_b[o], gate_a_w[o], gate_a_b[o], gate_x_w[o], gate_x_b[o],
                                lru_lambda[o])

            xr_p, gate_p, q_p, k_p, v_p = _norm_proj(xp, norm_mix[layer], w_in, splits, f"in_proj_p{layer}")
            xr_s, gate_s, q_s, k_s, v_s = _norm_proj(xs, norm_mix[layer], w_in, splits, f"in_proj_s{layer}")

            od_p, lse_p = [], []
            for g, (win, dil) in enumerate(D_PAIRS):
                og, lg = _dilated_prompt_group(q_p, k_p, v_p, bm_d[g], g, dil)
                od_p.append(og)
                lse_p.append(lg)
            od_s, lse_s = _dilated_sample(q_s, k_s, v_s, [c for c in caches_d], o, bcs_d_s, bn_d_s)

            oc_p, hc_p = _rglru_mixer(xr_p, gate_p, jnp.zeros(((C_CONV - 1) * BATCH, C_WIDTH), F32),
                                      jnp.zeros((BATCH, C_WIDTH), F32), lrp, BATCH, SEQ, 128, "rglru_prompt")
            conv_s = state_c_conv[o].astype(F32).transpose(1, 0, 2).reshape((C_CONV - 1) * DEC_BATCH, C_WIDTH)
            oc_s, hc_s = _rglru_mixer(xr_s, gate_s, conv_s, state_c_h[o].astype(F32), lrp,
                                      DEC_BATCH, DEC_SEQ, DEC_SEQ, "rglru_sample")

            xp = _out_proj(_out_proj_odd_kernel, xp, [oc_p] + od_p + lse_p, w_out, f"out_proj_p{layer}")
            xs = _out_proj(_out_proj_odd_kernel, xs, [oc_s] + list(od_s) + list(lse_s), w_out,
                           f"out_proj_s{layer}")

            ch_p.append(hc_p)
            ch_s.append(hc_s)
            cc_p.append(xr_p.reshape(BATCH, SEQ, C_WIDTH)[:, SEQ - (C_CONV - 1):])
            cc_s.append(xr_s.reshape(DEC_BATCH, DEC_SEQ, C_WIDTH)[:, DEC_SEQ - (C_CONV - 1):])
            k_p5 = k_p.reshape(BATCH, SEQ, D_N_GROUPS, D_HEADS_PER, HEAD_DIM)
            v_p5 = v_p.reshape(BATCH, SEQ, D_N_GROUPS, D_HEADS_PER, HEAD_DIM)
            k_s4 = k_s.reshape(DEC_BATCH, DEC_SEQ, D_N_GROUPS, D_GROUP_W)
            v_s4 = v_s.reshape(DEC_BATCH, DEC_SEQ, D_N_GROUPS, D_GROUP_W)
            for g, (win, dil) in enumerate(D_PAIRS):
                wc = min(win, SEQ)
                d_p[g].append(jnp.stack([k_p5[:, SEQ - wc:, g], v_p5[:, SEQ - wc:, g]], axis=2))
                d_new[g].append(jnp.concatenate([k_s4[:, :, g], v_s4[:, :, g]], axis=-1))

        mp = _moe_params(norm_ffn[layer], moe_router_g[layer], moe_router_g_b[layer], moe_router_e[layer],
                         moe_router_e_b[layer], moe_w_gate[layer], moe_w_up[layer], moe_w_down[layer])
        xp = _moe(xp, mp, f"moe_p{layer}")
        xs = _moe(xs, mp, f"moe_s{layer}")

    y_prompt = _final_norm(xp, norm_final, "final_norm_p").reshape(BATCH, SEQ, D_MODEL)
    y_sample = _final_norm(xs, norm_final, "final_norm_s").reshape(DEC_BATCH, DEC_SEQ, D_MODEL)

    new_a = _logical_cache(_shift_cache(cache_a, jnp.stack(a_new), "shift_cache_a"), A_KV_HEADS)
    new_d = [_logical_cache(_shift_cache(caches_d[g], jnp.stack(d_new[g]), f"shift_cache_d{g}"), D_HEADS_PER)
             for g in range(D_N_GROUPS)]

    return (y_prompt, y_sample,
            jnp.stack(a_p), new_a, jnp.stack(b_p), jnp.stack(b_s),
            jnp.stack(ch_p), jnp.stack(ch_s), jnp.stack(cc_p), jnp.stack(cc_s),
            jnp.stack(d_p[0]), new_d[0], jnp.stack(d_p[1]), new_d[1], jnp.stack(d_p[2]), new_d[2])
```

```python
import functools
import math

import jax
import jax.numpy as jnp
from jax import lax
from jax.experimental import pallas as pl
from jax.experimental.pallas import tpu as pltpu

F32 = jnp.float32
BF16 = jnp.bfloat16

D_MODEL = 1024
BATCH = 8
SEQ = 2048
DEPTH = 4
DEC_BATCH = 128
DEC_SEQ = 8
HEAD_DIM = 64
BLK = 128
RMS_EPS = 1e-6
NEG = -1e30

A_Q_HEADS = 8
A_KV_HEADS = 2
A_GQA = 4
A_WINDOW = 128
A_Q_W = A_Q_HEADS * HEAD_DIM
A_KV_W = A_KV_HEADS * HEAD_DIM

B_WIDTH = 512
B_GROUP_CH = 16
B_GROUPS = 32
B_STATE = 64
B_LANE_BLOCKS = 4
B_GROUPS_PER_BLOCK = B_GROUPS // B_LANE_BLOCKS
B_HALF = B_GROUPS_PER_BLOCK * B_STATE
B_STATE_W = B_LANE_BLOCKS * 2 * B_HALF

C_WIDTH = 512
C_BLOCKS = 8
C_BLOCK_W = 64
C_CONV = 4
C_POWER = 8.0

D_PAIRS = ((128, 1), (512, 4), (2048, 16))
D_N_GROUPS = 3
D_HEADS_PER = 4
D_GROUP_W = D_HEADS_PER * HEAD_DIM
D_QKV_W = D_N_GROUPS * D_GROUP_W

T5_BUCKETS = 32
T5_MAX_DIST = 2048

MOE_GROUPS = 4
MOE_PER_GROUP = 4
MOE_EXPERTS = 16
MOE_FF = 256
ROUTER_LANES = 128

NP_TOK = BATCH * SEQ
NS_TOK = DEC_BATCH * DEC_SEQ

VMEM_LIMIT_BYTES = 52 * 2 ** 20


def _cparams(*sem):
    return pltpu.CompilerParams(dimension_semantics=sem, vmem_limit_bytes=VMEM_LIMIT_BYTES)


def _nt_dot(a, b):
    return lax.dot_general(a, b, (((1,), (1,)), ((), ())), preferred_element_type=F32)


def _norm_proj_kernel(x_ref, g_ref, w_ref, *out_refs, splits):
    x = x_ref[...]
    y = x * lax.rsqrt(jnp.mean(x * x, axis=-1, keepdims=True) + RMS_EPS)
    xn = (y * g_ref[...]).astype(BF16)
    off = 0
    for o_ref, width in zip(out_refs, splits):
        o_ref[...] = jnp.dot(xn, w_ref[:, off:off + width], preferred_element_type=F32)
        off += width


def _norm_proj(x, g, w, splits, name):
    n = x.shape[0]
    tm = 512
    return pl.pallas_call(
        functools.partial(_norm_proj_kernel, splits=splits),
        grid=(n // tm,),
        in_specs=[pl.BlockSpec((tm, D_MODEL), lambda i: (i, 0)),
                  pl.BlockSpec((1, D_MODEL), lambda i: (0, 0)),
                  pl.BlockSpec(w.shape, lambda i: (0, 0))],
        out_specs=[pl.BlockSpec((tm, s), lambda i: (i, 0)) for s in splits],
        out_shape=[jax.ShapeDtypeStruct((n, s), F32) for s in splits],
        compiler_params=_cparams("parallel"),
        name=name,
    )(x, g.reshape(1, D_MODEL), w)


def _softmax_parts(scores, sink):
    m = jnp.max(scores[0], axis=1, keepdims=True)
    for s in scores[1:]:
        m = jnp.maximum(m, jnp.max(s, axis=1, keepdims=True))
    if sink is not None:
        m = jnp.maximum(m, sink)
    ps = [jnp.exp(s - m) for s in scores]
    den = jnp.sum(ps[0], axis=1, keepdims=True)
    for p in ps[1:]:
        den = den + jnp.sum(p, axis=1, keepdims=True)
    if sink is not None:
        den = den + jnp.exp(sink - m)
    inv = 1.0 / den
    return [(p * inv).astype(BF16) for p in ps], den, m


def _band_attn_kernel(*refs, n_heads, gqa, with_sink, dil, has_prev):
    refs = list(refs)
    sink_ref = refs.pop(0) if with_sink else None
    q_ref, kc_ref = refs.pop(0), refs.pop(0)
    kp_ref = refs.pop(0) if has_prev else None
    vc_ref = refs.pop(0)
    vp_ref = refs.pop(0) if has_prev else None
    bm_ref, o_ref = refs.pop(0), refs.pop(0)
    lse_ref = refs.pop(0) if refs else None
    n_kv = n_heads // gqa
    row_sel = [slice(None) if dil == 1 else pl.ds(r, BLK, stride=dil) for r in range(dil)]
    sc, sp, v_cur, v_prev = [], [], [], []
    for rows in row_sel:
        q = (q_ref[rows, :] * HEAD_DIM ** -0.5).astype(BF16)
        kc = kc_ref[rows, :].astype(BF16)
        vc = vc_ref[rows, :].astype(BF16)
        if has_prev:
            kp = kp_ref[rows, :].astype(BF16)
            vp = vp_ref[rows, :].astype(BF16)
        for hk in range(n_kv):
            ksl = slice(hk * HEAD_DIM, (hk + 1) * HEAD_DIM)
            qs = jnp.concatenate([q[:, h * HEAD_DIM:(h + 1) * HEAD_DIM]
                                  for h in range(hk * gqa, (hk + 1) * gqa)], axis=0)
            sc.append(_nt_dot(qs, kc[:, ksl]))
            v_cur.append(vc[:, ksl])
            if has_prev:
                sp.append(_nt_dot(qs, kp[:, ksl]))
                v_prev.append(vp[:, ksl])
    bias_c = jnp.concatenate([bm_ref[0, h, :, BLK:2 * BLK] for h in range(n_heads)] * dil, axis=0)
    scores = [jnp.concatenate(sc, axis=0) + bias_c]
    if has_prev:
        bias_p = jnp.concatenate([bm_ref[0, h, :, 0:BLK] for h in range(n_heads)] * dil, axis=0)
        scores.append(jnp.concatenate(sp, axis=0) + bias_p)
    sink = None
    if with_sink:
        sink = jnp.concatenate([jnp.full((BLK, 1), sink_ref[h], F32) for h in range(n_heads)] * dil, axis=0)
    ps, den, m = _softmax_parts(scores, sink)
    lse = m + jnp.log(den) if lse_ref is not None else None
    unit_rows = gqa * BLK
    for r, rows in enumerate(row_sel):
        outs = []
        for hk in range(n_kv):
            u = r * n_kv + hk
            usl = slice(u * unit_rows, (u + 1) * unit_rows)
            o = jnp.dot(ps[0][usl, :], v_cur[u], preferred_element_type=F32)
            if has_prev:
                o = o + jnp.dot(ps[1][usl, :], v_prev[u], preferred_element_type=F32)
            outs += [o[j * BLK:(j + 1) * BLK, :] for j in range(gqa)]
        o_ref[rows, :] = jnp.concatenate(outs, axis=1)
        if lse_ref is not None:
            base = r * n_heads * BLK
            lse_ref[rows, :] = jnp.concatenate(
                [jnp.broadcast_to(lse[base + h * BLK:base + (h + 1) * BLK, :], (BLK, HEAD_DIM))
                 for h in range(n_heads)], axis=1)


def _swa_prompt(q, kv, sinks, bm):
    nblk = SEQ // BLK
    row = lambda b, i: b * nblk + i
    prev = lambda b, i: b * nblk + jnp.maximum(i - 1, 0)
    return pl.pallas_call(
        functools.partial(_band_attn_kernel, n_heads=A_Q_HEADS, gqa=A_GQA, with_sink=True, dil=1,
                          has_prev=True),
        grid=(BATCH, nblk),
        in_specs=[pl.BlockSpec(memory_space=pltpu.SMEM),
                  pl.BlockSpec((BLK, A_Q_W), lambda b, i: (row(b, i), 0)),
                  pl.BlockSpec((BLK, A_KV_W), lambda b, i: (row(b, i), 0)),
                  pl.BlockSpec((BLK, A_KV_W), lambda b, i: (prev(b, i), 0)),
                  pl.BlockSpec((BLK, A_KV_W), lambda b, i: (row(b, i), 1)),
                  pl.BlockSpec((BLK, A_KV_W), lambda b, i: (prev(b, i), 1)),
                  pl.BlockSpec((1, A_Q_HEADS, BLK, 2 * BLK), lambda b, i: (jnp.minimum(i, 1), 0, 0, 0))],
        out_specs=pl.BlockSpec((BLK, A_Q_W), lambda b, i: (row(b, i), 0)),
        out_shape=jax.ShapeDtypeStruct((NP_TOK, A_Q_W), F32),
        compiler_params=_cparams("parallel", "parallel"),
        name="swa_prompt",
    )(sinks, q, kv, kv, kv, kv, bm)


def _dilated_prompt_group(q, k, v, bm, g, dil):
    rows = BLK * dil
    nchunk = SEQ // rows
    has_prev = nchunk > 1
    pair = LANES // HEAD_DIM if dil > 1 else D_HEADS_PER
    npair = D_HEADS_PER // pair
    width = pair * HEAD_DIM
    row = lambda b, i: b * nchunk + i
    prev = lambda b, i: b * nchunk + jnp.maximum(i - 1, 0)
    cur_spec = pl.BlockSpec((rows, width), lambda b, i, p: (row(b, i), g * npair + p))
    prev_spec = pl.BlockSpec((rows, width), lambda b, i, p: (prev(b, i), g * npair + p))
    out_spec = pl.BlockSpec((rows, width), lambda b, i, p: (row(b, i), p))
    bm_spec = pl.BlockSpec((1, pair, BLK, 2 * BLK), lambda b, i, p: (jnp.minimum(i, 1), p, 0, 0))
    if has_prev:
        in_specs, args = [cur_spec, cur_spec, prev_spec, cur_spec, prev_spec, bm_spec], (q, k, k, v, v, bm)
    else:
        in_specs, args = [cur_spec, cur_spec, cur_spec, bm_spec], (q, k, v, bm)
    return pl.pallas_call(
        functools.partial(_band_attn_kernel, n_heads=pair, gqa=1, with_sink=False, dil=dil,
                          has_prev=has_prev),
        grid=(BATCH, nchunk, npair),
        in_specs=in_specs,
        out_specs=[out_spec, out_spec],
        out_shape=[jax.ShapeDtypeStruct((NP_TOK, D_GROUP_W), F32)] * 2,
        compiler_params=_cparams("parallel", "parallel", "parallel"),
        name=f"dilated_prompt_g{g}",
    )(*args)


SWA_SAMPLE_BB = 8


def _pad_rows(x, rows):
    return jnp.concatenate([x, jnp.zeros((rows - x.shape[0], x.shape[1]), x.dtype)], axis=0)


def _swa_sample_kernel(sink_ref, q_ref, kv_ref, c_ref, bmc_ref, bmn_ref, o_ref):
    for bb in range(SWA_SAMPLE_BB):
        rs = slice(bb * DEC_SEQ, (bb + 1) * DEC_SEQ)
        q = (q_ref[rs, :] * HEAD_DIM ** -0.5).astype(BF16)
        kvn = _pad_rows(kv_ref[rs, :], BLK).astype(BF16)
        cache = c_ref[0, bb].astype(BF16)
        outs = [None] * A_Q_HEADS
        for hk in range(A_KV_HEADS):
            heads = range(hk * A_GQA, (hk + 1) * A_GQA)
            qs = jnp.concatenate([q[:, h * HEAD_DIM:(h + 1) * HEAD_DIM] for h in heads], axis=0)
            ksl = slice(hk * HEAD_DIM, (hk + 1) * HEAD_DIM)
            vsl = slice(A_KV_W + hk * HEAD_DIM, A_KV_W + (hk + 1) * HEAD_DIM)
            brow = slice(hk * A_GQA * DEC_SEQ, (hk + 1) * A_GQA * DEC_SEQ)
            s1 = jnp.dot(qs, cache[ksl, :], preferred_element_type=F32) + bmc_ref[brow, :]
            s2 = _nt_dot(qs, kvn[:, ksl]) + bmn_ref[brow, :]
            sink = jnp.concatenate(
                [jnp.full((DEC_SEQ, 1), sink_ref[h], F32) for h in heads], axis=0)
            (p1, p2), den, _ = _softmax_parts([s1, s2], sink)
            o = _nt_dot(p1, cache[vsl, :]) + jnp.dot(p2, kvn[:, vsl], preferred_element_type=F32)
            for j, h in enumerate(heads):
                outs[h] = o[j * DEC_SEQ:(j + 1) * DEC_SEQ, :]
        o_ref[rs, :] = jnp.concatenate(outs, axis=1)


def _swa_sample(q, kv, cache_all, layer, sinks, bmc, bmn):
    bb = SWA_SAMPLE_BB
    return pl.pallas_call(
        _swa_sample_kernel,
        grid=(DEC_BATCH // bb,),
        in_specs=[pl.BlockSpec(memory_space=pltpu.SMEM),
                  pl.BlockSpec((bb * DEC_SEQ, A_Q_W), lambda i: (i, 0)),
                  pl.BlockSpec((bb * DEC_SEQ, 2 * A_KV_W), lambda i: (i, 0)),
                  pl.BlockSpec((1, bb, 2 * A_KV_W, A_WINDOW), lambda i: (layer, i, 0, 0)),
                  pl.BlockSpec(bmc.shape, lambda i: (0, 0)),
                  pl.BlockSpec(bmn.shape, lambda i: (0, 0))],
        out_specs=pl.BlockSpec((bb * DEC_SEQ, A_Q_W), lambda i: (i, 0)),
        out_shape=jax.ShapeDtypeStruct((NS_TOK, A_Q_W), F32),
        compiler_params=_cparams("parallel"),
        name="swa_sample",
    )(sinks, q, kv, cache_all, bmc, bmn)


def _dil_sample_kernel(q_ref, k_ref, v_ref, c0_ref, c1_ref, c2_ref,
                       bc0_ref, bc1_ref, bc2_ref, bn_ref,
                       o0_ref, o1_ref, o2_ref, l0_ref, l1_ref, l2_ref):
    q = q_ref[...] * HEAD_DIM ** -0.5
    k = k_ref[...]
    v = v_ref[...]
    nrow = D_HEADS_PER * DEC_SEQ
    row_head = lax.broadcasted_iota(jnp.int32, (nrow, D_GROUP_W), 0) // DEC_SEQ
    lane_head = lax.broadcasted_iota(jnp.int32, (nrow, D_GROUP_W), 1) // HEAD_DIM
    head_mask = row_head == lane_head
    out_lane_head = lax.broadcasted_iota(jnp.int32, (DEC_SEQ, D_GROUP_W), 1) // HEAD_DIM
    groups = ((c0_ref, bc0_ref, o0_ref, l0_ref), (c1_ref, bc1_ref, o1_ref, l1_ref),
              (c2_ref, bc2_ref, o2_ref, l2_ref))
    for g, (c_ref, bc_ref, o_ref, l_ref) in enumerate(groups):
        gsl = slice(g * D_GROUP_W, (g + 1) * D_GROUP_W)
        qbd = jnp.where(head_mask, jnp.concatenate([q[:, gsl]] * D_HEADS_PER, axis=0), 0.0).astype(BF16)
        kn = _pad_rows(k[:, gsl], BLK).astype(BF16)
        vn = _pad_rows(v[:, gsl], BLK).astype(BF16)
        kt = c_ref[0, 0, 0:D_GROUP_W, :].astype(BF16)
        vt = c_ref[0, 0, D_GROUP_W:2 * D_GROUP_W, :].astype(BF16)
        s1 = jnp.dot(qbd, kt, preferred_element_type=F32) + bc_ref[0]
        s2 = _nt_dot(qbd, kn) + bn_ref[g]
        (p1, p2), den, m = _softmax_parts([s1, s2], None)
        of = _nt_dot(p1, vt) + jnp.dot(p2, vn, preferred_element_type=F32)
        lse = m + jnp.log(den)
        og = jnp.zeros((DEC_SEQ, D_GROUP_W), F32)
        lg = jnp.zeros((DEC_SEQ, D_GROUP_W), F32)
        for h in range(D_HEADS_PER):
            rs = slice(h * DEC_SEQ, (h + 1) * DEC_SEQ)
            sel = out_lane_head == h
            og = og + jnp.where(sel, of[rs, :], 0.0)
            lg = lg + jnp.where(sel, lse[rs, :], 0.0)
        o_ref[...] = og
        l_ref[...] = lg


def _dilated_sample(q, k, v, caches, layer, bcs, bn):
    tok_spec = pl.BlockSpec((DEC_SEQ, D_QKV_W), lambda b: (b, 0))
    out_spec = pl.BlockSpec((DEC_SEQ, D_GROUP_W), lambda b: (b, 0))
    outs = pl.pallas_call(
        _dil_sample_kernel,
        grid=(DEC_BATCH,),
        in_specs=[tok_spec, tok_spec, tok_spec]
        + [pl.BlockSpec((1, 1, 2 * D_GROUP_W, win), lambda b: (layer, b, 0, 0)) for win, _ in D_PAIRS]
        + [pl.BlockSpec(bc.shape, lambda b: (0, 0, 0)) for bc in bcs]
        + [pl.BlockSpec(bn.shape, lambda b: (0, 0, 0))],
        out_specs=[out_spec] * 6,
        out_shape=[jax.ShapeDtypeStruct((NS_TOK, D_GROUP_W), F32)] * 6,
        compiler_params=_cparams("parallel"),
        name="dilated_sample",
    )(q, k, v, *caches, *bcs, bn)
    return outs[:3], outs[3:]


def _shift_kernel(c_ref, n_ref, o_ref, *, wc, bb):
    width = c_ref.shape[2]
    lane = lax.broadcasted_iota(jnp.int32, (width, BLK), 1)
    for i in range(bb):
        shifted = pltpu.roll(c_ref[0, i], wc - DEC_SEQ, 1)
        new_t = jnp.concatenate([jnp.zeros((BLK - DEC_SEQ, width), F32), n_ref[0, i]], axis=0).T
        if wc > BLK:
            o_ref[0, i, :, 0:wc - BLK] = shifted[:, 0:wc - BLK]
        o_ref[0, i, :, wc - BLK:wc] = jnp.where(lane >= BLK - DEC_SEQ, new_t, shifted[:, wc - BLK:wc])


def _shift_cache(cache, new_rows, name):
    n_layers, _, width, wc = cache.shape
    bb = max(1, (4 * 2 ** 20) // (wc * width * 4))
    return pl.pallas_call(
        functools.partial(_shift_kernel, wc=wc, bb=bb),
        grid=(n_layers, DEC_BATCH // bb),
        in_specs=[pl.BlockSpec((1, bb, width, wc), lambda l, i: (l, i, 0, 0)),
                  pl.BlockSpec((1, bb, DEC_SEQ, width), lambda l, i: (l, i, 0, 0))],
        out_specs=pl.BlockSpec((1, bb, width, wc), lambda l, i: (l, i, 0, 0)),
        out_shape=jax.ShapeDtypeStruct(cache.shape, cache.dtype),
        compiler_params=_cparams("parallel", "parallel"),
        name=name,
    )(cache, new_rows)


LANES = 128


def _to_time_major(src_ref, dst_ref, off, bt, lc):
    for b in range(bt):
        x = src_ref[b]
        for j in range(dst_ref.shape[0]):
            dst_ref[j, pl.ds(off + b, lc, stride=bt), :] = x[:, j * LANES:(j + 1) * LANES]


def _from_time_major(src_ref, dst_ref, bt, lc):
    for b in range(bt):
        dst_ref[b] = jnp.concatenate(
            [src_ref[j, pl.ds(b, lc, stride=bt), :] for j in range(src_ref.shape[0])], axis=1)


def _lane_blocks(ref, r0, nrows):
    return jnp.concatenate([ref[j, r0:r0 + nrows, :] for j in range(ref.shape[0])], axis=1)


def _store_lane_blocks(ref, r0, x):
    for j in range(ref.shape[0]):
        ref[j, r0:r0 + x.shape[0], :] = x[:, j * LANES:(j + 1) * LANES]


def _token_spec(bt, lc, width):
    return pl.BlockSpec((bt, lc, width), lambda c: (0, c, 0))


def _s5_kernel(u_ref, h0_ref, bm_ref, cm_ref, lam_ref, d_ref, gw_ref, gb_ref, o_ref, hl_ref,
               us_ref, hs_ref, *, bt, lc):
    rows = lc * bt
    blk_w = 2 * B_HALF

    @pl.when(pl.program_id(0) == 0)
    def _():
        hs_ref[0:bt, :] = h0_ref[...]

    _to_time_major(u_ref, us_ref, 0, bt, lc)
    u = _lane_blocks(us_ref, 0, rows)
    ub = u.astype(BF16)
    for j in range(B_LANE_BLOCKS):
        cols = jnp.dot(ub[:, j * LANES:(j + 1) * LANES], bm_ref[j], preferred_element_type=F32)
        hs_ref[bt:, j * blk_w:j * blk_w + B_HALF] = cols[:, :B_HALF]
        hs_ref[bt:, j * blk_w + B_HALF:(j + 1) * blk_w] = cols[:, :B_HALF] + cols[:, B_HALF:]

    def step(t, carry):
        r0 = pl.multiple_of(t * bt, bt)
        for j in range(B_LANE_BLOCKS):
            re = slice(j * blk_w, j * blk_w + B_HALF)
            im = slice(j * blk_w + B_HALF, (j + 1) * blk_w)
            lr = lam_ref[2 * j:2 * j + 1, :]
            li = lam_ref[2 * j + 1:2 * j + 2, :]
            pr = hs_ref[pl.ds(r0, bt), re]
            pi = hs_ref[pl.ds(r0, bt), im]
            hs_ref[pl.ds(r0 + bt, bt), re] = lr * pr - li * pi + hs_ref[pl.ds(r0 + bt, bt), re]
            hs_ref[pl.ds(r0 + bt, bt), im] = lr * pi + li * pr + hs_ref[pl.ds(r0 + bt, bt), im]
        return carry

    lax.fori_loop(0, lc, step, 0)

    ys = []
    for j in range(B_LANE_BLOCKS):
        h_re = hs_ref[bt:, j * blk_w:j * blk_w + B_HALF]
        h_im = hs_ref[bt:, j * blk_w + B_HALF:(j + 1) * blk_w]
        lhs = jnp.concatenate([h_re + h_im, h_im], axis=1).astype(BF16)
        ys.append(jnp.dot(lhs, cm_ref[j], preferred_element_type=F32))
    y = jax.nn.gelu(jnp.concatenate(ys, axis=1) + d_ref[...] * u)
    z = jnp.dot(y.astype(BF16), gw_ref[...], preferred_element_type=F32) + gb_ref[...]
    _store_lane_blocks(us_ref, 0, y * jax.nn.sigmoid(z))
    _from_time_major(us_ref, o_ref, bt, lc)
    last = hs_ref[rows:rows + bt, :]
    hl_ref[...] = last
    hs_ref[0:bt, :] = last


def _s5_mixer(u, h0, prm, bt, seq, lc, name):
    tok = _token_spec(bt, lc, B_WIDTH)
    u_in = u.reshape(bt, seq, B_WIDTH)
    full = lambda a: pl.BlockSpec(a.shape, lambda c: (0,) * a.ndim)
    out, hl = pl.pallas_call(
        functools.partial(_s5_kernel, bt=bt, lc=lc),
        grid=(seq // lc,),
        in_specs=[tok, full(h0), full(prm["bm"]), full(prm["cm"]), full(prm["lam"]), full(prm["d"]),
                  full(prm["glu_w"]), full(prm["glu_b"])],
        out_specs=[tok, pl.BlockSpec((bt, B_STATE_W), lambda c: (0, 0))],
        out_shape=[jax.ShapeDtypeStruct(u_in.shape, F32), jax.ShapeDtypeStruct((bt, B_STATE_W), F32)],
        scratch_shapes=[pltpu.VMEM((B_WIDTH // LANES, lc * bt, LANES), F32),
                        pltpu.VMEM((lc * bt + bt, B_STATE_W), F32)],
        compiler_params=_cparams("arbitrary"),
        name=name,
    )(u_in, h0, prm["bm"], prm["cm"], prm["lam"], prm["d"], prm["glu_w"], prm["glu_b"])
    return out.reshape(bt * seq, B_WIDTH), hl


def _s5_params(a_re, a_im, log_dt, b_re, b_im, c_re, c_im, d, glu_w, glu_b):
    lam = lax.complex(a_re.astype(F32), a_im.astype(F32))
    dt = jnp.exp(log_dt.astype(F32))[:, None]
    lam_bar = jnp.exp(lam * dt)
    b_bar = ((lam_bar - 1.0) / lam)[..., None] * lax.complex(b_re.astype(F32), b_im.astype(F32))
    nb, gb = B_LANE_BLOCKS, B_GROUPS_PER_BLOCK
    eye = jnp.eye(gb, dtype=F32)

    def in_mat(part):
        p = part.reshape(nb, gb, B_STATE, B_GROUP_CH)
        return jnp.einsum("jgnc,gh->jgchn", p, eye).reshape(nb, gb * B_GROUP_CH, gb * B_STATE)

    def out_mat(part):
        p = part.reshape(nb, gb, B_GROUP_CH, B_STATE)
        return jnp.einsum("jgcn,gh->jgnhc", p, eye).reshape(nb, gb * B_STATE, gb * B_GROUP_CH)

    bm = jnp.concatenate([in_mat(b_bar.real), in_mat(b_bar.imag - b_bar.real)], axis=2).astype(BF16)
    c_re32, c_im32 = c_re.astype(F32), c_im.astype(F32)
    cm = jnp.concatenate([out_mat(c_re32), out_mat(-(c_re32 + c_im32))], axis=1).astype(BF16)
    lam_rows = jnp.stack([lam_bar.real.reshape(nb, B_HALF), lam_bar.imag.reshape(nb, B_HALF)],
                         axis=1).reshape(2 * nb, B_HALF)
    return {"bm": bm, "cm": cm, "lam": lam_rows, "d": d.astype(F32).reshape(1, B_WIDTH),
            "glu_w": glu_w.astype(BF16), "glu_b": glu_b.astype(F32).reshape(1, B_WIDTH)}


def _s5_state_to_cols(state):
    bt = state.shape[0]
    s = state.astype(F32).reshape(bt, B_LANE_BLOCKS, B_GROUPS_PER_BLOCK, B_STATE, 2)
    return s.transpose(0, 1, 4, 2, 3).reshape(bt, B_STATE_W)


def _s5_cols_to_state(cols):
    bt = cols.shape[0]
    s = cols.reshape(bt, B_LANE_BLOCKS, 2, B_GROUPS_PER_BLOCK, B_STATE)
    return s.transpose(0, 1, 3, 4, 2).reshape(bt, B_GROUPS, B_STATE, 2)


def _rglru_kernel(xr_ref, gate_ref, cb_ref, h0_ref, cw_ref, cbias_ref, wa_ref, ba_ref, wx_ref, bx_ref,
                  nsp_ref, o_ref, hl_ref, xp_ref, gs_ref, a_ref, hs_ref, *, bt, lc):
    rows = lc * bt
    pad = (C_CONV - 1) * bt

    @pl.when(pl.program_id(0) == 0)
    def _():
        _store_lane_blocks(xp_ref, 0, cb_ref[...])
        hs_ref[0:bt, :] = h0_ref[...]

    _to_time_major(xr_ref, xp_ref, pad, bt, lc)
    _to_time_major(gate_ref, gs_ref, 0, bt, lc)
    xc = _lane_blocks(xp_ref, 0, rows) * cw_ref[0:1, :]
    for tap in range(1, C_CONV):
        xc = xc + _lane_blocks(xp_ref, tap * bt, rows) * cw_ref[tap:tap + 1, :]
    xcf = xc + cbias_ref[...]
    xb = xcf.astype(BF16)
    r = jax.nn.sigmoid(jnp.dot(xb, wa_ref[...], preferred_element_type=F32) + ba_ref[...])
    i = jax.nn.sigmoid(jnp.dot(xb, wx_ref[...], preferred_element_type=F32) + bx_ref[...])
    log_a = nsp_ref[...] * r
    a = jnp.exp(log_a)
    a_ref[...] = a
    hs_ref[bt:, :] = jnp.sqrt(-jnp.tanh(log_a) * (a * a + 1.0)) * (i * xcf)

    def step(t, carry):
        r0 = pl.multiple_of(t * bt, bt)
        hs_ref[pl.ds(r0 + bt, bt), :] = (a_ref[pl.ds(r0, bt), :] * hs_ref[pl.ds(r0, bt), :]
                                         + hs_ref[pl.ds(r0 + bt, bt), :])
        return carry

    lax.fori_loop(0, lc, step, 0)

    _store_lane_blocks(gs_ref, 0, hs_ref[bt:, :] * jax.nn.gelu(_lane_blocks(gs_ref, 0, rows)))
    _from_time_major(gs_ref, o_ref, bt, lc)
    last = hs_ref[rows:rows + bt, :]
    hl_ref[...] = last
    hs_ref[0:bt, :] = last
    _store_lane_blocks(xp_ref, 0, _lane_blocks(xp_ref, rows, pad))


def _rglru_mixer(xr, gate, conv_buf, h0, prm, bt, seq, lc, name):
    rows = lc * bt
    tok = _token_spec(bt, lc, C_WIDTH)
    shape3 = lambda x: x.reshape(bt, seq, C_WIDTH)
    full = lambda a: pl.BlockSpec(a.shape, lambda c: (0,) * a.ndim)
    names = ("conv_w", "conv_b", "wa", "ba", "wx", "bx", "nsp")
    out, hl = pl.pallas_call(
        functools.partial(_rglru_kernel, bt=bt, lc=lc),
        grid=(seq // lc,),
        in_specs=[tok, tok, full(conv_buf), full(h0)] + [full(prm[k]) for k in names],
        out_specs=[tok, pl.BlockSpec((bt, C_WIDTH), lambda c: (0, 0))],
        out_shape=[jax.ShapeDtypeStruct(shape3(xr).shape, F32), jax.ShapeDtypeStruct((bt, C_WIDTH), F32)],
        scratch_shapes=[pltpu.VMEM((C_WIDTH // LANES, rows + (C_CONV - 1) * bt, LANES), F32),
                        pltpu.VMEM((C_WIDTH // LANES, rows, LANES), F32),
                        pltpu.VMEM((rows, C_WIDTH), F32),
                        pltpu.VMEM((rows + bt, C_WIDTH), F32)],
        compiler_params=_cparams("arbitrary"),
        name=name,
    )(shape3(xr), shape3(gate), conv_buf, h0, *[prm[k] for k in names])
    return out.reshape(bt * seq, C_WIDTH), hl


def _rglru_params(conv_w, conv_b, gate_a_w, gate_a_b, gate_x_w, gate_x_b, lru_lambda):
    eye = jnp.eye(C_BLOCKS, dtype=F32)

    def block_diag(w):
        return jnp.einsum("njk,nm->njmk", w.astype(F32), eye).reshape(C_WIDTH, C_WIDTH).astype(BF16)

    row = lambda x: x.astype(F32).reshape(1, C_WIDTH)
    return {"conv_w": conv_w.astype(F32), "conv_b": row(conv_b),
            "wa": block_diag(gate_a_w), "ba": row(gate_a_b),
            "wx": block_diag(gate_x_w), "bx": row(gate_x_b),
            "nsp": row(-C_POWER * jax.nn.softplus(-lru_lambda.astype(F32)))}


def _out_proj_even_kernel(x_ref, oa_ref, ob_ref, w_ref, o_ref):
    acc = jnp.dot(oa_ref[...].astype(BF16), w_ref[0:A_Q_W, :], preferred_element_type=F32)
    acc = acc + jnp.dot(ob_ref[...].astype(BF16), w_ref[A_Q_W:, :], preferred_element_type=F32)
    o_ref[...] = x_ref[...] + acc


def _out_proj_odd_kernel(x_ref, oc_ref, o0_ref, o1_ref, o2_ref, l0_ref, l1_ref, l2_ref, w_ref, o_ref):
    l0, l1, l2 = l0_ref[...], l1_ref[...], l2_ref[...]
    m = jnp.maximum(jnp.maximum(l0, l1), l2)
    e0, e1, e2 = jnp.exp(l0 - m), jnp.exp(l1 - m), jnp.exp(l2 - m)
    od = (o0_ref[...] * e0 + o1_ref[...] * e1 + o2_ref[...] * e2) / (e0 + e1 + e2)
    acc = jnp.dot(oc_ref[...].astype(BF16), w_ref[0:C_WIDTH, :], preferred_element_type=F32)
    acc = acc + jnp.dot(od.astype(BF16), w_ref[C_WIDTH:, :], preferred_element_type=F32)
    o_ref[...] = x_ref[...] + acc


def _out_proj(kernel, x, parts, w, name, in_place=True):
    n = x.shape[0]
    tm = 512
    return pl.pallas_call(
        kernel,
        grid=(n // tm,),
        in_specs=[pl.BlockSpec((tm, D_MODEL), lambda i: (i, 0))]
        + [pl.BlockSpec((tm, p.shape[1]), lambda i: (i, 0)) for p in parts]
        + [pl.BlockSpec(w.shape, lambda i: (0, 0))],
        out_specs=pl.BlockSpec((tm, D_MODEL), lambda i: (i, 0)),
        out_shape=jax.ShapeDtypeStruct((n, D_MODEL), F32),
        input_output_aliases={0: 0} if in_place else {},
        compiler_params=_cparams("parallel"),
        name=name,
    )(x, *parts, w)


MOE_TM = 512


def _moe_kernel(x_ref, g_ref, rw_ref, rb_ref, wg_ref, wu_ref, wd_ref, o_ref, xn_ref, gates_ref, acc_ref):
    e = pl.program_id(1)

    @pl.when(e == 0)
    def _():
        x = x_ref[...]
        y = x * lax.rsqrt(jnp.mean(x * x, axis=-1, keepdims=True) + RMS_EPS)
        xn = y * g_ref[...]
        xb0 = xn.astype(BF16)
        xn_ref[...] = xb0
        logits = jnp.dot(xb0, rw_ref[...], preferred_element_type=F32) + rb_ref[...]
        lane = lax.broadcasted_iota(jnp.int32, logits.shape, 1).astype(F32)
        ninf = float("-inf")
        far = float(ROUTER_LANES)
        lg = jnp.where(lane < MOE_GROUPS, logits, ninf)
        gmax = jnp.max(lg, axis=1, keepdims=True)
        g_idx = jnp.min(jnp.where(lg == gmax, lane, far), axis=1, keepdims=True)
        g_w = 1.0 / jnp.sum(jnp.exp(lg - gmax), axis=1, keepdims=True)
        lane_grp = jnp.floor((lane - MOE_GROUPS) * (1.0 / MOE_PER_GROUP))
        in_grp = (lane >= MOE_GROUPS) & (lane < MOE_GROUPS + MOE_EXPERTS) & (lane_grp == g_idx)
        le = jnp.where(in_grp, logits, ninf)
        v1 = jnp.max(le, axis=1, keepdims=True)
        i1 = jnp.min(jnp.where(le == v1, lane, far), axis=1, keepdims=True)
        le2 = jnp.where(lane == i1, ninf, le)
        v2 = jnp.max(le2, axis=1, keepdims=True)
        i2 = jnp.min(jnp.where(le2 == v2, lane, far), axis=1, keepdims=True)
        e2 = jnp.exp(v2 - v1)
        w1 = g_w / (1.0 + e2)
        w2 = g_w * e2 / (1.0 + e2)
        gates_ref[...] = (jnp.where(lane == i1 - MOE_GROUPS, w1, 0.0)
                          + jnp.where(lane == i2 - MOE_GROUPS, w2, 0.0))
        acc_ref[...] = jnp.zeros_like(acc_ref)

    xb = xn_ref[...]
    gates = gates_ref[...]
    lane_i = lax.broadcasted_iota(jnp.int32, gates.shape, 1)
    gate = jnp.sum(jnp.where(lane_i == e, gates, 0.0), axis=1, keepdims=True)
    hg = jnp.dot(xb, wg_ref[0], preferred_element_type=F32)
    hu = jnp.dot(xb, wu_ref[0], preferred_element_type=F32)
    hid = jax.nn.silu(hg) * hu * gate
    acc_ref[...] += jnp.dot(hid.astype(BF16), wd_ref[0], preferred_element_type=F32)

    @pl.when(e == MOE_EXPERTS - 1)
    def _():
        o_ref[...] = x_ref[...] + acc_ref[...]


def _moe(x, prm, name):
    n = x.shape[0]
    tm = MOE_TM
    return pl.pallas_call(
        _moe_kernel,
        grid=(n // tm, MOE_EXPERTS),
        in_specs=[pl.BlockSpec((tm, D_MODEL), lambda i, e: (i, 0)),
                  pl.BlockSpec((1, D_MODEL), lambda i, e: (0, 0)),
                  pl.BlockSpec((D_MODEL, ROUTER_LANES), lambda i, e: (0, 0)),
                  pl.BlockSpec((1, ROUTER_LANES), lambda i, e: (0, 0)),
                  pl.BlockSpec((1, D_MODEL, MOE_FF), lambda i, e: (e, 0, 0)),
                  pl.BlockSpec((1, D_MODEL, MOE_FF), lambda i, e: (e, 0, 0)),
                  pl.BlockSpec((1, MOE_FF, D_MODEL), lambda i, e: (e, 0, 0))],
        out_specs=pl.BlockSpec((tm, D_MODEL), lambda i, e: (i, 0)),
        out_shape=jax.ShapeDtypeStruct((n, D_MODEL), F32),
        scratch_shapes=[pltpu.VMEM((tm, D_MODEL), BF16),
                        pltpu.VMEM((tm, ROUTER_LANES), F32),
                        pltpu.VMEM((tm, D_MODEL), F32)],
        input_output_aliases={0: 0},
        compiler_params=_cparams("parallel", "arbitrary"),
        name=name,
    )(x, prm["g"], prm["rw"], prm["rb"], prm["wg"], prm["wu"], prm["wd"])


def _moe_params(norm_g, router_g, router_g_b, router_e, router_e_b, w_gate, w_up, w_down):
    used = MOE_GROUPS + MOE_EXPERTS
    rw = jnp.concatenate([router_g.astype(F32), router_e.astype(F32),
                          jnp.zeros((D_MODEL, ROUTER_LANES - used), F32)], axis=1)
    rb = jnp.concatenate([router_g_b.astype(F32), router_e_b.astype(F32),
                          jnp.zeros((ROUTER_LANES - used,), F32)]).reshape(1, ROUTER_LANES)
    return {"g": norm_g.astype(F32).reshape(1, D_MODEL), "rw": rw.astype(BF16), "rb": rb,
            "wg": w_gate.astype(BF16), "wu": w_up.astype(BF16), "wd": w_down.astype(BF16)}


def _final_norm_kernel(x_ref, g_ref, o_ref):
    x = x_ref[...]
    y = x * lax.rsqrt(jnp.mean(x * x, axis=-1, keepdims=True) + RMS_EPS)
    o_ref[...] = y * g_ref[...]


def _final_norm(x, g, name):
    n = x.shape[0]
    tm = 512
    return pl.pallas_call(
        _final_norm_kernel,
        grid=(n // tm,),
        in_specs=[pl.BlockSpec((tm, D_MODEL), lambda i: (i, 0)),
                  pl.BlockSpec((1, D_MODEL), lambda i: (0, 0))],
        out_specs=pl.BlockSpec((tm, D_MODEL), lambda i: (i, 0)),
        out_shape=jax.ShapeDtypeStruct((n, D_MODEL), F32),
        compiler_params=_cparams("parallel"),
        name=name,
    )(x, g.astype(F32).reshape(1, D_MODEL))


def _t5_bucket(dist):
    n = jnp.maximum(dist, 0)
    max_exact = T5_BUCKETS // 2
    nf = jnp.maximum(n, 1).astype(F32)
    large = max_exact + (jnp.log(nf / max_exact) / math.log(T5_MAX_DIST / max_exact)
                         * (T5_BUCKETS - max_exact)).astype(jnp.int32)
    return jnp.where(n < max_exact, n, jnp.minimum(large, T5_BUCKETS - 1))


def _rel_bias(table, dist):
    hit = _t5_bucket(dist)[..., None] == jnp.arange(T5_BUCKETS)
    rows = jnp.sum(jnp.where(hit[..., None], table.astype(F32), 0.0), axis=-2)
    return jnp.moveaxis(rows, -1, 0)


def _band_bias_mask(table, max_dist, dil):
    r = jnp.arange(BLK)[:, None]
    s = jnp.arange(2 * BLK)[None, :]
    dist = BLK + r - s
    valid = (dist >= 0) & (dist <= max_dist)
    bias = _rel_bias(table, dist * dil)
    later = jnp.where(valid[None], bias, NEG)
    first = jnp.where((valid & (s >= BLK))[None], bias, NEG)
    return jnp.stack([first, later])


def _swa_sample_bias_mask(table):
    s = jnp.arange(DEC_SEQ)[:, None]
    col = jnp.arange(BLK)[None, :]
    dist_c = A_WINDOW + s - col
    dist_n = s - col
    tab = table[:, :A_Q_HEADS]
    bc = jnp.where(((dist_c >= 0) & (dist_c < A_WINDOW))[None], _rel_bias(tab, dist_c), NEG)
    bn = jnp.where(((dist_n >= 0) & (col < DEC_SEQ))[None], _rel_bias(tab, dist_n), NEG)
    return bc.reshape(A_Q_HEADS * DEC_SEQ, A_WINDOW), bn.reshape(A_Q_HEADS * DEC_SEQ, BLK)


def _dil_sample_bias_mask(table):
    s = jnp.arange(DEC_SEQ)[:, None]
    col = jnp.arange(BLK)[None, :]
    bcs, bns = [], []
    for g, (win, dil) in enumerate(D_PAIRS):
        lo = A_Q_HEADS + g * D_HEADS_PER
        tab = table[:, lo:lo + D_HEADS_PER]
        dist_c = win + s - jnp.arange(win)[None, :]
        valid_c = (dist_c >= 0) & (dist_c % dil == 0) & (dist_c <= win)
        bc = jnp.where(valid_c[None], _rel_bias(tab, dist_c), NEG)
        dist_n = s - col
        valid_n = (dist_n >= 0) & (dist_n % dil == 0) & (col < DEC_SEQ)
        bn = jnp.where(valid_n[None], _rel_bias(tab, dist_n), NEG)
        bcs.append(bc.reshape(1, D_HEADS_PER * DEC_SEQ, win))
        bns.append(bn.reshape(D_HEADS_PER * DEC_SEQ, BLK))
    return bcs, jnp.stack(bns)


def _native_cache(c):
    n_layers, nb, wc = c.shape[:3]
    return c.transpose(0, 1, 3, 4, 5, 2).reshape(n_layers, nb, -1, wc)


def _logical_cache(c, heads):
    n_layers, nb, _, wc = c.shape
    return c.reshape(n_layers, nb, 2, heads, HEAD_DIM, wc).transpose(0, 1, 5, 2, 3, 4)


def kernel(x_prompt, x_sample, cache_a_kv, state_b, state_c_h, state_c_conv, cache_d_g0, cache_d_g1, cache_d_g2, rel_table, norm_mix, norm_ffn, norm_final, w_in_even, w_out_even, sinks_a, s5_a_re, s5_a_im, s5_log_dt, s5_b_re, s5_b_im, s5_c_re, s5_c_im, s5_d, s5_glu_w, s5_glu_b, w_in_odd, w_out_odd, conv_w, conv_b, gate_a_w, gate_a_b, gate_x_w, gate_x_b, lru_lambda, moe_router_g, moe_router_g_b, moe_router_e, moe_router_e_b, moe_w_gate, moe_w_up, moe_w_down):
    xp = x_prompt.astype(F32).reshape(NP_TOK, D_MODEL)
    xs = x_sample.astype(F32).reshape(NS_TOK, D_MODEL)

    cache_a = _native_cache(cache_a_kv)
    caches_d = [_native_cache(c) for c in (cache_d_g0, cache_d_g1, cache_d_g2)]

    bm_a = _band_bias_mask(rel_table[:, :A_Q_HEADS], A_WINDOW - 1, 1)
    bm_d = [_band_bias_mask(rel_table[:, A_Q_HEADS + g * D_HEADS_PER:A_Q_HEADS + (g + 1) * D_HEADS_PER],
                            win // dil, dil) for g, (win, dil) in enumerate(D_PAIRS)]
    bmc_a_s, bmn_a_s = _swa_sample_bias_mask(rel_table)
    bcs_d_s, bn_d_s = _dil_sample_bias_mask(rel_table)

    a_p, a_new, b_p, b_s = [], [], [], []
    ch_p, ch_s, cc_p, cc_s = [], [], [], []
    d_p = [[], [], []]
    d_new = [[], [], []]

    for layer in range(DEPTH):
        if layer % 2 == 0:
            e = layer // 2
            w_in = w_in_even[e].astype(BF16)
            w_out = w_out_even[e].astype(BF16)
            splits = (A_Q_W, 2 * A_KV_W, B_WIDTH)
            s5p = _s5_params(s5_a_re[e], s5_a_im[e], s5_log_dt[e], s5_b_re[e], s5_b_im[e],
                             s5_c_re[e], s5_c_im[e], s5_d[e], s5_glu_w[e], s5_glu_b[e])
            sinks = sinks_a[e].astype(F32)

            q_p, kv_p, u_p = _norm_proj(xp, norm_mix[layer], w_in, splits, f"in_proj_p{layer}")
            q_s, kv_s, u_s = _norm_proj(xs, norm_mix[layer], w_in, splits, f"in_proj_s{layer}")

            oa_p = _swa_prompt(q_p, kv_p, sinks, bm_a)
            oa_s = _swa_sample(q_s, kv_s, cache_a, e, sinks, bmc_a_s, bmn_a_s)

            ob_p, hl_p = _s5_mixer(u_p, jnp.zeros((BATCH, B_STATE_W), F32), s5p, BATCH, SEQ, 64, "s5_prompt")
            ob_s, hl_s = _s5_mixer(u_s, _s5_state_to_cols(state_b[e]), s5p, DEC_BATCH, DEC_SEQ, DEC_SEQ,
                                   "s5_sample")

            xp = _out_proj(_out_proj_even_kernel, xp, [oa_p, ob_p], w_out, f"out_proj_p{layer}",
                           in_place=layer > 0)
            xs = _out_proj(_out_proj_even_kernel, xs, [oa_s, ob_s], w_out, f"out_proj_s{layer}",
                           in_place=layer > 0)

            a_p.append(kv_p.reshape(BATCH, SEQ, 2, A_KV_HEADS, HEAD_DIM)[:, SEQ - A_WINDOW:])
            a_new.append(kv_s.reshape(DEC_BATCH, DEC_SEQ, 2 * A_KV_W))
            b_p.append(_s5_cols_to_state(hl_p))
            b_s.append(_s5_cols_to_state(hl_s))
        else:
            o = layer // 2
            w_in = w_in_odd[o].astype(BF16)
            w_out = w_out_odd[o].astype(BF16)
            splits = (C_WIDTH, C_WIDTH, D_QKV_W, D_QKV_W, D_QKV_W)
            lrp = _rglru_params(conv_w[o], conv_b[o], gate_a_w[o], gate_a_b[o], gate_x_w[o], gate_x_b[o],
                                lru_lambda[o])

            xr_p, gate_p, q_p, k_p, v_p = _norm_proj(xp, norm_mix[layer], w_in, splits, f"in_proj_p{layer}")
            xr_s, gate_s, q_s, k_s, v_s = _norm_proj(xs, norm_mix[layer], w_in, splits, f"in_proj_s{layer}")

            od_p, lse_p = [], []
            for g, (win, dil) in enumerate(D_PAIRS):
                og, lg = _dilated_prompt_group(q_p, k_p, v_p, bm_d[g], g, dil)
                od_p.append(og)
                lse_p.append(lg)
            od_s, lse_s = _dilated_sample(q_s, k_s, v_s, [c for c in caches_d], o, bcs_d_s, bn_d_s)

            oc_p, hc_p = _rglru_mixer(xr_p, gate_p, jnp.zeros(((C_CONV - 1) * BATCH, C_WIDTH), F32),
                                      jnp.zeros((BATCH, C_WIDTH), F32), lrp, BATCH, SEQ, 128, "rglru_prompt")
            conv_s = state_c_conv[o].astype(F32).transpose(1, 0, 2).reshape((C_CONV - 1) * DEC_BATCH, C_WIDTH)
            oc_s, hc_s = _rglru_mixer(xr_s, gate_s, conv_s, state_c_h[o].astype(F32), lrp,
                                      DEC_BATCH, DEC_SEQ, DEC_SEQ, "rglru_sample")

            xp = _out_proj(_out_proj_odd_kernel, xp, [oc_p] + od_p + lse_p, w_out, f"out_proj_p{layer}")
            xs = _out_proj(_out_proj_odd_kernel, xs, [oc_s] + list(od_s) + list(lse_s), w_out,
                           f"out_proj_s{layer}")

            ch_p.append(hc_p)
            ch_s.append(hc_s)
            cc_p.append(xr_p.reshape(BATCH, SEQ, C_WIDTH)[:, SEQ - (C_CONV - 1):])
            cc_s.append(xr_s.reshape(DEC_BATCH, DEC_SEQ, C_WIDTH)[:, DEC_SEQ - (C_CONV - 1):])
            k_p5 = k_p.reshape(BATCH, SEQ, D_N_GROUPS, D_HEADS_PER, HEAD_DIM)
            v_p5 = v_p.reshape(BATCH, SEQ, D_N_GROUPS, D_HEADS_PER, HEAD_DIM)
            k_s4 = k_s.reshape(DEC_BATCH, DEC_SEQ, D_N_GROUPS, D_GROUP_W)
            v_s4 = v_s.reshape(DEC_BATCH, DEC_SEQ, D_N_GROUPS, D_GROUP_W)
            for g, (win, dil) in enumerate(D_PAIRS):
                wc = min(win, SEQ)
                d_p[g].append(jnp.stack([k_p5[:, SEQ - wc:, g], v_p5[:, SEQ - wc:, g]], axis=2))
                d_new[g].append(jnp.concatenate([k_s4[:, :, g], v_s4[:, :, g]], axis=-1))

        mp = _moe_params(norm_ffn[layer], moe_router_g[layer], moe_router_g_b[layer], moe_router_e[layer],
                         moe_router_e_b[layer], moe_w_gate[layer], moe_w_up[layer], moe_w_down[layer])
        xp = _moe(xp, mp, f"moe_p{layer}")
        xs = _moe(xs, mp, f"moe_s{layer}")

    y_prompt = _final_norm(xp, norm_final, "final_norm_p").reshape(BATCH, SEQ, D_MODEL)
    y_sample = _final_norm(xs, norm_final, "final_norm_s").reshape(DEC_BATCH, DEC_SEQ, D_MODEL)

    new_a = _logical_cache(_shift_cache(cache_a, jnp.stack(a_new), "shift_cache_a"), A_KV_HEADS)
    new_d = [_logical_cache(_shift_cache(caches_d[g], jnp.stack(d_new[g]), f"shift_cache_d{g}"), D_HEADS_PER)
             for g in range(D_N_GROUPS)]

    return (y_prompt, y_sample,
            jnp.stack(a_p), new_a, jnp.stack(b_p), jnp.stack(b_s),
            jnp.stack(ch_p), jnp.stack(ch_s), jnp.stack(cc_p), jnp.stack(cc_s),
            jnp.stack(d_p[0]), new_d[0], jnp.stack(d_p[1]), new_d[1], jnp.stack(d_p[2]), new_d[2])
```

```python
import functools
import math

import jax
import jax.numpy as jnp
from jax import lax
from jax.experimental import pallas as pl
from jax.experimental.pallas import tpu as pltpu

F32 = jnp.float32
BF16 = jnp.bfloat16

D_MODEL = 1024
BATCH = 8
SEQ = 2048
DEPTH = 4
DEC_BATCH = 128
DEC_SEQ = 8
HEAD_DIM = 64
BLK = 128
RMS_EPS = 1e-6
NEG = -1e30

A_Q_HEADS = 8
A_KV_HEADS = 2
A_GQA = 4
A_WINDOW = 128
A_Q_W = A_Q_HEADS * HEAD_DIM
A_KV_W = A_KV_HEADS * HEAD_DIM

B_WIDTH = 512
B_GROUP_CH = 16
B_GROUPS = 32
B_STATE = 64
B_LANE_BLOCKS = 4
B_GROUPS_PER_BLOCK = B_GROUPS // B_LANE_BLOCKS
B_HALF = B_GROUPS_PER_BLOCK * B_STATE
B_STATE_W = B_LANE_BLOCKS * 2 * B_HALF

C_WIDTH = 512
C_BLOCKS = 8
C_BLOCK_W = 64
C_CONV = 4
C_POWER = 8.0

D_PAIRS = ((128, 1), (512, 4), (2048, 16))
D_N_GROUPS = 3
D_HEADS_PER = 4
D_GROUP_W = D_HEADS_PER * HEAD_DIM
D_QKV_W = D_N_GROUPS * D_GROUP_W

T5_BUCKETS = 32
T5_MAX_DIST = 2048

MOE_GROUPS = 4
MOE_PER_GROUP = 4
MOE_EXPERTS = 16
MOE_FF = 256
ROUTER_LANES = 128

NP_TOK = BATCH * SEQ
NS_TOK = DEC_BATCH * DEC_SEQ

VMEM_LIMIT_BYTES = 52 * 2 ** 20


def _cparams(*sem):
    return pltpu.CompilerParams(dimension_semantics=sem, vmem_limit_bytes=VMEM_LIMIT_BYTES)


def _nt_dot(a, b):
    return lax.dot_general(a, b, (((1,), (1,)), ((), ())), preferred_element_type=F32)


def _norm_proj_kernel(x_ref, g_ref, w_ref, *out_refs, splits):
    x = x_ref[...]
    y = x * lax.rsqrt(jnp.mean(x * x, axis=-1, keepdims=True) + RMS_EPS)
    xn = (y * g_ref[...]).astype(BF16)
    off = 0
    for o_ref, width in zip(out_refs, splits):
        o_ref[...] = jnp.dot(xn, w_ref[:, off:off + width], preferred_element_type=F32)
        off += width


def _norm_proj(x, g, w, splits, name):
    n = x.shape[0]
    tm = 512
    return pl.pallas_call(
        functools.partial(_norm_proj_kernel, splits=splits),
        grid=(n // tm,),
        in_specs=[pl.BlockSpec((tm, D_MODEL), lambda i: (i, 0)),
                  pl.BlockSpec((1, D_MODEL), lambda i: (0, 0)),
                  pl.BlockSpec(w.shape, lambda i: (0, 0))],
        out_specs=[pl.BlockSpec((tm, s), lambda i: (i, 0)) for s in splits],
        out_shape=[jax.ShapeDtypeStruct((n, s), F32) for s in splits],
        compiler_params=_cparams("parallel"),
        name=name,
    )(x, g.reshape(1, D_MODEL), w)


def _softmax_parts(scores, sink):
    m = jnp.max(scores[0], axis=1, keepdims=True)
    for s in scores[1:]:
        m = jnp.maximum(m, jnp.max(s, axis=1, keepdims=True))
    if sink is not None:
        m = jnp.maximum(m, sink)
    ps = [jnp.exp(s - m) for s in scores]
    den = jnp.sum(ps[0], axis=1, keepdims=True)
    for p in ps[1:]:
        den = den + jnp.sum(p, axis=1, keepdims=True)
    if sink is not None:
        den = den + jnp.exp(sink - m)
    inv = 1.0 / den
    return [(p * inv).astype(BF16) for p in ps], den, m


def _band_attn_kernel(*refs, n_heads, gqa, with_sink, dil, has_prev):
    refs = list(refs)
    sink_ref = refs.pop(0) if with_sink else None
    q_ref, kc_ref = refs.pop(0), refs.pop(0)
    kp_ref = refs.pop(0) if has_prev else None
    vc_ref = refs.pop(0)
    vp_ref = refs.pop(0) if has_prev else None
    bm_ref, o_ref = refs.pop(0), refs.pop(0)
    lse_ref = refs.pop(0) if refs else None
    n_kv = n_heads // gqa
    row_sel = [slice(None) if dil == 1 else pl.ds(r, BLK, stride=dil) for r in range(dil)]
    sc, sp, v_cur, v_prev = [], [], [], []
    for rows in row_sel:
        q = (q_ref[rows, :] * HEAD_DIM ** -0.5).astype(BF16)
        kc = kc_ref[rows, :].astype(BF16)
        vc = vc_ref[rows, :].astype(BF16)
        if has_prev:
            kp = kp_ref[rows, :].astype(BF16)
            vp = vp_ref[rows, :].astype(BF16)
        for hk in range(n_kv):
            ksl = slice(hk * HEAD_DIM, (hk + 1) * HEAD_DIM)
            qs = jnp.concatenate([q[:, h * HEAD_DIM:(h + 1) * HEAD_DIM]
                                  for h in range(hk * gqa, (hk + 1) * gqa)], axis=0)
            sc.append(_nt_dot(qs, kc[:, ksl]))
            v_cur.append(vc[:, ksl])
            if has_prev:
                sp.append(_nt_dot(qs, kp[:, ksl]))
                v_prev.append(vp[:, ksl])
    bias_c = jnp.concatenate([bm_ref[0, h, :, BLK:2 * BLK] for h in range(n_heads)] * dil, axis=0)
    scores = [jnp.concatenate(sc, axis=0) + bias_c]
    if has_prev:
        bias_p = jnp.concatenate([bm_ref[0, h, :, 0:BLK] for h in range(n_heads)] * dil, axis=0)
        scores.append(jnp.concatenate(sp, axis=0) + bias_p)
    sink = None
    if with_sink:
        sink = jnp.concatenate([jnp.full((BLK, 1), sink_ref[h], F32) for h in range(n_heads)] * dil, axis=0)
    ps, den, m = _softmax_parts(scores, sink)
    lse = m + jnp.log(den) if lse_ref is not None else None
    unit_rows = gqa * BLK
    for r, rows in enumerate(row_sel):
        outs = []
        for hk in range(n_kv):
            u = r * n_kv + hk
            usl = slice(u * unit_rows, (u + 1) * unit_rows)
            o = jnp.dot(ps[0][usl, :], v_cur[u], preferred_element_type=F32)
            if has_prev:
                o = o + jnp.dot(ps[1][usl, :], v_prev[u], preferred_element_type=F32)
            outs += [o[j * BLK:(j + 1) * BLK, :] for j in range(gqa)]
        o_ref[rows, :] = jnp.concatenate(outs, axis=1)
        if lse_ref is not None:
            base = r * n_heads * BLK
            lse_ref[rows, :] = jnp.concatenate(
                [jnp.broadcast_to(lse[base + h * BLK:base + (h + 1) * BLK, :], (BLK, HEAD_DIM))
                 for h in range(n_heads)], axis=1)


def _swa_prompt(q, kv, sinks, bm):
    nblk = SEQ // BLK
    row = lambda b, i: b * nblk + i
    prev = lambda b, i: b * nblk + jnp.maximum(i - 1, 0)
    return pl.pallas_call(
        functools.partial(_band_attn_kernel, n_heads=A_Q_HEADS, gqa=A_GQA, with_sink=True, dil=1,
                          has_prev=True),
        grid=(BATCH, nblk),
        in_specs=[pl.BlockSpec(memory_space=pltpu.SMEM),
                  pl.BlockSpec((BLK, A_Q_W), lambda b, i: (row(b, i), 0)),
                  pl.BlockSpec((BLK, A_KV_W), lambda b, i: (row(b, i), 0)),
                  pl.BlockSpec((BLK, A_KV_W), lambda b, i: (prev(b, i), 0)),
                  pl.BlockSpec((BLK, A_KV_W), lambda b, i: (row(b, i), 1)),
                  pl.BlockSpec((BLK, A_KV_W), lambda b, i: (prev(b, i), 1)),
                  pl.BlockSpec((1, A_Q_HEADS, BLK, 2 * BLK), lambda b, i: (jnp.minimum(i, 1), 0, 0, 0))],
        out_specs=pl.BlockSpec((BLK, A_Q_W), lambda b, i: (row(b, i), 0)),
        out_shape=jax.ShapeDtypeStruct((NP_TOK, A_Q_W), F32),
        compiler_params=_cparams("parallel", "parallel"),
        name="swa_prompt",
    )(sinks, q, kv, kv, kv, kv, bm)


def _dilated_prompt_group(q, k, v, bm, g, dil):
    rows = BLK * dil
    nchunk = SEQ // rows
    has_prev = nchunk > 1
    pair = LANES // HEAD_DIM if dil > 1 else D_HEADS_PER
    npair = D_HEADS_PER // pair
    width = pair * HEAD_DIM
    row = lambda b, i: b * nchunk + i
    prev = lambda b, i: b * nchunk + jnp.maximum(i - 1, 0)
    cur_spec = pl.BlockSpec((rows, width), lambda b, i, p: (row(b, i), g * npair + p))
    prev_spec = pl.BlockSpec((rows, width), lambda b, i, p: (prev(b, i), g * npair + p))
    out_spec = pl.BlockSpec((rows, width), lambda b, i, p: (row(b, i), p))
    bm_spec = pl.BlockSpec((1, pair, BLK, 2 * BLK), lambda b, i, p: (jnp.minimum(i, 1), p, 0, 0))
    if has_prev:
        in_specs, args = [cur_spec, cur_spec, prev_spec, cur_spec, prev_spec, bm_spec], (q, k, k, v, v, bm)
    else:
        in_specs, args = [cur_spec, cur_spec, cur_spec, bm_spec], (q, k, v, bm)
    return pl.pallas_call(
        functools.partial(_band_attn_kernel, n_heads=pair, gqa=1, with_sink=False, dil=dil,
                          has_prev=has_prev),
        grid=(BATCH, nchunk, npair),
        in_specs=in_specs,
        out_specs=[out_spec, out_spec],
        out_shape=[jax.ShapeDtypeStruct((NP_TOK, D_GROUP_W), F32)] * 2,
        compiler_params=_cparams("parallel", "parallel", "parallel"),
        name=f"dilated_prompt_g{g}",
    )(*args)


SWA_SAMPLE_BB = 8


def _pad_rows(x, rows):
    return jnp.concatenate([x, jnp.zeros((rows - x.shape[0], x.shape[1]), x.dtype)], axis=0)


def _swa_sample_kernel(sink_ref, q_ref, kv_ref, c_ref, bmc_ref, bmn_ref, o_ref):
    for bb in range(SWA_SAMPLE_BB):
        rs = slice(bb * DEC_SEQ, (bb + 1) * DEC_SEQ)
        q = (q_ref[rs, :] * HEAD_DIM ** -0.5).astype(BF16)
        kvn = _pad_rows(kv_ref[rs, :], BLK).astype(BF16)
        cache = c_ref[0, bb].astype(BF16)
        outs = [None] * A_Q_HEADS
        for hk in range(A_KV_HEADS):
            heads = range(hk * A_GQA, (hk + 1) * A_GQA)
            qs = jnp.concatenate([q[:, h * HEAD_DIM:(h + 1) * HEAD_DIM] for h in heads], axis=0)
            ksl = slice(hk * HEAD_DIM, (hk + 1) * HEAD_DIM)
            vsl = slice(A_KV_W + hk * HEAD_DIM, A_KV_W + (hk + 1) * HEAD_DIM)
            brow = slice(hk * A_GQA * DEC_SEQ, (hk + 1) * A_GQA * DEC_SEQ)
            s1 = jnp.dot(qs, cache[ksl, :], preferred_element_type=F32) + bmc_ref[brow, :]
            s2 = _nt_dot(qs, kvn[:, ksl]) + bmn_ref[brow, :]
            sink = jnp.concatenate(
                [jnp.full((DEC_SEQ, 1), sink_ref[h], F32) for h in heads], axis=0)
            (p1, p2), den, _ = _softmax_parts([s1, s2], sink)
            o = _nt_dot(p1, cache[vsl, :]) + jnp.dot(p2, kvn[:, vsl], preferred_element_type=F32)
            for j, h in enumerate(heads):
                outs[h] = o[j * DEC_SEQ:(j + 1) * DEC_SEQ, :]
        o_ref[rs, :] = jnp.concatenate(outs, axis=1)


def _swa_sample(q, kv, cache_all, layer, sinks, bmc, bmn):
    bb = SWA_SAMPLE_BB
    return pl.pallas_call(
        _swa_sample_kernel,
        grid=(DEC_BATCH // bb,),
        in_specs=[pl.BlockSpec(memory_space=pltpu.SMEM),
                  pl.BlockSpec((bb * DEC_SEQ, A_Q_W), lambda i: (i, 0)),
                  pl.BlockSpec((bb * DEC_SEQ, 2 * A_KV_W), lambda i: (i, 0)),
                  pl.BlockSpec((1, bb, 2 * A_KV_W, A_WINDOW), lambda i: (layer, i, 0, 0)),
                  pl.BlockSpec(bmc.shape, lambda i: (0, 0)),
                  pl.BlockSpec(bmn.shape, lambda i: (0, 0))],
        out_specs=pl.BlockSpec((bb * DEC_SEQ, A_Q_W), lambda i: (i, 0)),
        out_shape=jax.ShapeDtypeStruct((NS_TOK, A_Q_W), F32),
        compiler_params=_cparams("parallel"),
        name="swa_sample",
    )(sinks, q, kv, cache_all, bmc, bmn)


def _dil_sample_kernel(q_ref, k_ref, v_ref, c0_ref, c1_ref, c2_ref,
                       bc0_ref, bc1_ref, bc2_ref, bn_ref,
                       o0_ref, o1_ref, o2_ref, l0_ref, l1_ref, l2_ref):
    q = q_ref[...] * HEAD_DIM ** -0.5
    k = k_ref[...]
    v = v_ref[...]
    nrow = D_HEADS_PER * DEC_SEQ
    row_head = lax.broadcasted_iota(jnp.int32, (nrow, D_GROUP_W), 0) // DEC_SEQ
    lane_head = lax.broadcasted_iota(jnp.int32, (nrow, D_GROUP_W), 1) // HEAD_DIM
    head_mask = row_head == lane_head
    out_lane_head = lax.broadcasted_iota(jnp.int32, (DEC_SEQ, D_GROUP_W), 1) // HEAD_DIM
    groups = ((c0_ref, bc0_ref, o0_ref, l0_ref), (c1_ref, bc1_ref, o1_ref, l1_ref),
              (c2_ref, bc2_ref, o2_ref, l2_ref))
    for g, (c_ref, bc_ref, o_ref, l_ref) in enumerate(groups):
        gsl = slice(g * D_GROUP_W, (g + 1) * D_GROUP_W)
        qbd = jnp.where(head_mask, jnp.concatenate([q[:, gsl]] * D_HEADS_PER, axis=0), 0.0).astype(BF16)
        kn = _pad_rows(k[:, gsl], BLK).astype(BF16)
        vn = _pad_rows(v[:, gsl], BLK).astype(BF16)
        kt = c_ref[0, 0, 0:D_GROUP_W, :].astype(BF16)
        vt = c_ref[0, 0, D_GROUP_W:2 * D_GROUP_W, :].astype(BF16)
        s1 = jnp.dot(qbd, kt, preferred_element_type=F32) + bc_ref[0]
        s2 = _nt_dot(qbd, kn) + bn_ref[g]
        (p1, p2), den, m = _softmax_parts([s1, s2], None)
        of = _nt_dot(p1, vt) + jnp.dot(p2, vn, preferred_element_type=F32)
        lse = m + jnp.log(den)
        og = jnp.zeros((DEC_SEQ, D_GROUP_W), F32)
        lg = jnp.zeros((DEC_SEQ, D_GROUP_W), F32)
        for h in range(D_HEADS_PER):
            rs = slice(h * DEC_SEQ, (h + 1) * DEC_SEQ)
            sel = out_lane_head == h
            og = og + jnp.where(sel, of[rs, :], 0.0)
            lg = lg + jnp.where(sel, lse[rs, :], 0.0)
        o_ref[...] = og
        l_ref[...] = lg


def _dilated_sample(q, k, v, caches, layer, bcs, bn):
    tok_spec = pl.BlockSpec((DEC_SEQ, D_QKV_W), lambda b: (b, 0))
    out_spec = pl.BlockSpec((DEC_SEQ, D_GROUP_W), lambda b: (b, 0))
    outs = pl.pallas_call(
        _dil_sample_kernel,
        grid=(DEC_BATCH,),
        in_specs=[tok_spec, tok_spec, tok_spec]
        + [pl.BlockSpec((1, 1, 2 * D_GROUP_W, win), lambda b: (layer, b, 0, 0)) for win, _ in D_PAIRS]
        + [pl.BlockSpec(bc.shape, lambda b: (0, 0, 0)) for bc in bcs]
        + [pl.BlockSpec(bn.shape, lambda b: (0, 0, 0))],
        out_specs=[out_spec] * 6,
        out_shape=[jax.ShapeDtypeStruct((NS_TOK, D_GROUP_W), F32)] * 6,
        compiler_params=_cparams("parallel"),
        name="dilated_sample",
    )(q, k, v, *caches, *bcs, bn)
    return outs[:3], outs[3:]


def _shift_kernel(c_ref, n_ref, o_ref, *, wc, bb):
    width = c_ref.shape[2]
    lane = lax.broadcasted_iota(jnp.int32, (width, BLK), 1)
    for i in range(bb):
        shifted = pltpu.roll(c_ref[0, i], wc - DEC_SEQ, 1)
        new_t = jnp.concatenate([jnp.zeros((BLK - DEC_SEQ, width), F32), n_ref[0, i]], axis=0).T
        if wc > BLK:
            o_ref[0, i, :, 0:wc - BLK] = shifted[:, 0:wc - BLK]
        o_ref[0, i, :, wc - BLK:wc] = jnp.where(lane >= BLK - DEC_SEQ, new_t, shifted[:, wc - BLK:wc])


def _shift_cache(cache, new_rows, name):
    n_layers, _, width, wc = cache.shape
    bb = max(1, (4 * 2 ** 20) // (wc * width * 4))
    return pl.pallas_call(
        functools.partial(_shift_kernel, wc=wc, bb=bb),
        grid=(n_layers, DEC_BATCH // bb),
        in_specs=[pl.BlockSpec((1, bb, width, wc), lambda l, i: (l, i, 0, 0)),
                  pl.BlockSpec((1, bb, DEC_SEQ, width), lambda l, i: (l, i, 0, 0))],
        out_specs=pl.BlockSpec((1, bb, width, wc), lambda l, i: (l, i, 0, 0)),
        out_shape=jax.ShapeDtypeStruct(cache.shape, cache.dtype),
        compiler_params=_cparams("parallel", "parallel"),
        name=name,
    )(cache, new_rows)


LANES = 128


def _to_time_major(src_ref, dst_ref, off, bt, lc):
    for b in range(bt):
        x = src_ref[b]
        for j in range(dst_ref.shape[0]):
            dst_ref[j, pl.ds(off + b, lc, stride=bt), :] = x[:, j * LANES:(j + 1) * LANES]


def _from_time_major(src_ref, dst_ref, bt, lc):
    for b in range(bt):
        dst_ref[b] = jnp.concatenate(
            [src_ref[j, pl.ds(b, lc, stride=bt), :] for j in range(src_ref.shape[0])], axis=1)


def _lane_blocks(ref, r0, nrows):
    return jnp.concatenate([ref[j, r0:r0 + nrows, :] for j in range(ref.shape[0])], axis=1)


def _store_lane_blocks(ref, r0, x):
    for j in range(ref.shape[0]):
        ref[j, r0:r0 + x.shape[0], :] = x[:, j * LANES:(j + 1) * LANES]


def _token_spec(bt, lc, width):
    return pl.BlockSpec((bt, lc, width), lambda c: (0, c, 0))


def _s5_kernel(u_ref, h0_ref, bm_ref, cm_ref, lam_ref, d_ref, gw_ref, gb_ref, o_ref, hl_ref,
               us_ref, hs_ref, *, bt, lc):
    rows = lc * bt
    blk_w = 2 * B_HALF

    @pl.when(pl.program_id(0) == 0)
    def _():
        hs_ref[0:bt, :] = h0_ref[...]

    _to_time_major(u_ref, us_ref, 0, bt, lc)
    u = _lane_blocks(us_ref, 0, rows)
    ub = u.astype(BF16)
    for j in range(B_LANE_BLOCKS):
        cols = jnp.dot(ub[:, j * LANES:(j + 1) * LANES], bm_ref[j], preferred_element_type=F32)
        hs_ref[bt:, j * blk_w:j * blk_w + B_HALF] = cols[:, :B_HALF]
        hs_ref[bt:, j * blk_w + B_HALF:(j + 1) * blk_w] = cols[:, :B_HALF] + cols[:, B_HALF:]

    def step(t, carry):
        r0 = pl.multiple_of(t * bt, bt)
        for j in range(B_LANE_BLOCKS):
            re = slice(j * blk_w, j * blk_w + B_HALF)
            im = slice(j * blk_w + B_HALF, (j + 1) * blk_w)
            lr = lam_ref[2 * j:2 * j + 1, :]
            li = lam_ref[2 * j + 1:2 * j + 2, :]
            pr = hs_ref[pl.ds(r0, bt), re]
            pi = hs_ref[pl.ds(r0, bt), im]
            hs_ref[pl.ds(r0 + bt, bt), re] = lr * pr - li * pi + hs_ref[pl.ds(r0 + bt, bt), re]
            hs_ref[pl.ds(r0 + bt, bt), im] = lr * pi + li * pr + hs_ref[pl.ds(r0 + bt, bt), im]
        return carry

    lax.fori_loop(0, lc, step, 0)

    ys = []
    for j in range(B_LANE_BLOCKS):
        h_re = hs_ref[bt:, j * blk_w:j * blk_w + B_HALF]
        h_im = hs_ref[bt:, j * blk_w + B_HALF:(j + 1) * blk_w]
        lhs = jnp.concatenate([h_re + h_im, h_im], axis=1).astype(BF16)
        ys.append(jnp.dot(lhs, cm_ref[j], preferred_element_type=F32))
    y = jax.nn.gelu(jnp.concatenate(ys, axis=1) + d_ref[...] * u)
    z = jnp.dot(y.astype(BF16), gw_ref[...], preferred_element_type=F32) + gb_ref[...]
    _store_lane_blocks(us_ref, 0, y * jax.nn.sigmoid(z))
    _from_time_major(us_ref, o_ref, bt, lc)
    last = hs_ref[rows:rows + bt, :]
    hl_ref[...] = last
    hs_ref[0:bt, :] = last


def _s5_mixer(u, h0, prm, bt, seq, lc, name):
    tok = _token_spec(bt, lc, B_WIDTH)
    u_in = u.reshape(bt, seq, B_WIDTH)
    full = lambda a: pl.BlockSpec(a.shape, lambda c: (0,) * a.ndim)
    out, hl = pl.pallas_call(
        functools.partial(_s5_kernel, bt=bt, lc=lc),
        grid=(seq // lc,),
        in_specs=[tok, full(h0), full(prm["bm"]), full(prm["cm"]), full(prm["lam"]), full(prm["d"]),
                  full(prm["glu_w"]), full(prm["glu_b"])],
        out_specs=[tok, pl.BlockSpec((bt, B_STATE_W), lambda c: (0, 0))],
        out_shape=[jax.ShapeDtypeStruct(u_in.shape, F32), jax.ShapeDtypeStruct((bt, B_STATE_W), F32)],
        scratch_shapes=[pltpu.VMEM((B_WIDTH // LANES, lc * bt, LANES), F32),
                        pltpu.VMEM((lc * bt + bt, B_STATE_W), F32)],
        compiler_params=_cparams("arbitrary"),
        name=name,
    )(u_in, h0, prm["bm"], prm["cm"], prm["lam"], prm["d"], prm["glu_w"], prm["glu_b"])
    return out.reshape(bt * seq, B_WIDTH), hl


def _s5_params(a_re, a_im, log_dt, b_re, b_im, c_re, c_im, d, glu_w, glu_b):
    lam = lax.complex(a_re.astype(F32), a_im.astype(F32))
    dt = jnp.exp(log_dt.astype(F32))[:, None]
    lam_bar = jnp.exp(lam * dt)
    b_bar = ((lam_bar - 1.0) / lam)[..., None] * lax.complex(b_re.astype(F32), b_im.astype(F32))
    nb, gb = B_LANE_BLOCKS, B_GROUPS_PER_BLOCK
    eye = jnp.eye(gb, dtype=F32)

    def in_mat(part):
        p = part.reshape(nb, gb, B_STATE, B_GROUP_CH)
        return jnp.einsum("jgnc,gh->jgchn", p, eye).reshape(nb, gb * B_GROUP_CH, gb * B_STATE)

    def out_mat(part):
        p = part.reshape(nb, gb, B_GROUP_CH, B_STATE)
        return jnp.einsum("jgcn,gh->jgnhc", p, eye).reshape(nb, gb * B_STATE, gb * B_GROUP_CH)

    bm = jnp.concatenate([in_mat(b_bar.real), in_mat(b_bar.imag - b_bar.real)], axis=2).astype(BF16)
    c_re32, c_im32 = c_re.astype(F32), c_im.astype(F32)
    cm = jnp.concatenate([out_mat(c_re32), out_mat(-(c_re32 + c_im32))], axis=1).astype(BF16)
    lam_rows = jnp.stack([lam_bar.real.reshape(nb, B_HALF), lam_bar.imag.reshape(nb, B_HALF)],
                         axis=1).reshape(2 * nb, B_HALF)
    return {"bm": bm, "cm": cm, "lam": lam_rows, "d": d.astype(F32).reshape(1, B_WIDTH),
            "glu_w": glu_w.astype(BF16), "glu_b": glu_b.astype(F32).reshape(1, B_WIDTH)}


def _s5_state_to_cols(state):
    bt = state.shape[0]
    s = state.astype(F32).reshape(bt, B_LANE_BLOCKS, B_GROUPS_PER_BLOCK, B_STATE, 2)
    return s.transpose(0, 1, 4, 2, 3).reshape(bt, B_STATE_W)


def _s5_cols_to_state(cols):
    bt = cols.shape[0]
    s = cols.reshape(bt, B_LANE_BLOCKS, 2, B_GROUPS_PER_BLOCK, B_STATE)
    return s.transpose(0, 1, 3, 4, 2).reshape(bt, B_GROUPS, B_STATE, 2)


def _rglru_kernel(xr_ref, gate_ref, cb_ref, h0_ref, cw_ref, cbias_ref, wa_ref, ba_ref, wx_ref, bx_ref,
                  nsp_ref, o_ref, hl_ref, xp_ref, gs_ref, a_ref, hs_ref, *, bt, lc):
    rows = lc * bt
    pad = (C_CONV - 1) * bt

    @pl.when(pl.program_id(0) == 0)
    def _():
        _store_lane_blocks(xp_ref, 0, cb_ref[...])
        hs_ref[0:bt, :] = h0_ref[...]

    _to_time_major(xr_ref, xp_ref, pad, bt, lc)
    _to_time_major(gate_ref, gs_ref, 0, bt, lc)
    xc = _lane_blocks(xp_ref, 0, rows) * cw_ref[0:1, :]
    for tap in range(1, C_CONV):
        xc = xc + _lane_blocks(xp_ref, tap * bt, rows) * cw_ref[tap:tap + 1, :]
    xcf = xc + cbias_ref[...]
    xb = xcf.astype(BF16)
    r = jax.nn.sigmoid(jnp.dot(xb, wa_ref[...], preferred_element_type=F32) + ba_ref[...])
    i = jax.nn.sigmoid(jnp.dot(xb, wx_ref[...], preferred_element_type=F32) + bx_ref[...])
    log_a = nsp_ref[...] * r
    a = jnp.exp(log_a)
    a_ref[...] = a
    hs_ref[bt:, :] = jnp.sqrt(-jnp.tanh(log_a) * (a * a + 1.0)) * (i * xcf)

    def step(t, carry):
        r0 = pl.multiple_of(t * bt, bt)
        hs_ref[pl.ds(r0 + bt, bt), :] = (a_ref[pl.ds(r0, bt), :] * hs_ref[pl.ds(r0, bt), :]
                                         + hs_ref[pl.ds(r0 + bt, bt), :])
        return carry

    lax.fori_loop(0, lc, step, 0)

    _store_lane_blocks(gs_ref, 0, hs_ref[bt:, :] * jax.nn.gelu(_lane_blocks(gs_ref, 0, rows)))
    _from_time_major(gs_ref, o_ref, bt, lc)
    last = hs_ref[rows:rows + bt, :]
    hl_ref[...] = last
    hs_ref[0:bt, :] = last
    _store_lane_blocks(xp_ref, 0, _lane_blocks(xp_ref, rows, pad))


def _rglru_mixer(xr, gate, conv_buf, h0, prm, bt, seq, lc, name):
    rows = lc * bt
    tok = _token_spec(bt, lc, C_WIDTH)
    shape3 = lambda x: x.reshape(bt, seq, C_WIDTH)
    full = lambda a: pl.BlockSpec(a.shape, lambda c: (0,) * a.ndim)
    names = ("conv_w", "conv_b", "wa", "ba", "wx", "bx", "nsp")
    out, hl = pl.pallas_call(
        functools.partial(_rglru_kernel, bt=bt, lc=lc),
        grid=(seq // lc,),
        in_specs=[tok, tok, full(conv_buf), full(h0)] + [full(prm[k]) for k in names],
        out_specs=[tok, pl.BlockSpec((bt, C_WIDTH), lambda c: (0, 0))],
        out_shape=[jax.ShapeDtypeStruct(shape3(xr).shape, F32), jax.ShapeDtypeStruct((bt, C_WIDTH), F32)],
        scratch_shapes=[pltpu.VMEM((C_WIDTH // LANES, rows + (C_CONV - 1) * bt, LANES), F32),
                        pltpu.VMEM((C_WIDTH // LANES, rows, LANES), F32),
                        pltpu.VMEM((rows, C_WIDTH), F32),
                        pltpu.VMEM((rows + bt, C_WIDTH), F32)],
        compiler_params=_cparams("arbitrary"),
        name=name,
    )(shape3(xr), shape3(gate), conv_buf, h0, *[prm[k] for k in names])
    return out.reshape(bt * seq, C_WIDTH), hl


def _rglru_params(conv_w, conv_b, gate_a_w, gate_a_b, gate_x_w, gate_x_b, lru_lambda):
    eye = jnp.eye(C_BLOCKS, dtype=F32)

    def block_diag(w):
        return jnp.einsum("njk,nm->njmk", w.astype(F32), eye).reshape(C_WIDTH, C_WIDTH).astype(BF16)

    row = lambda x: x.astype(F32).reshape(1, C_WIDTH)
    return {"conv_w": conv_w.astype(F32), "conv_b": row(conv_b),
            "wa": block_diag(gate_a_w), "ba": row(gate_a_b),
            "wx": block_diag(gate_x_w), "bx": row(gate_x_b),
            "nsp": row(-C_POWER * jax.nn.softplus(-lru_lambda.astype(F32)))}


def _out_proj_even_kernel(x_ref, oa_ref, ob_ref, w_ref, o_ref):
    acc = jnp.dot(oa_ref[...].astype(BF16), w_ref[0:A_Q_W, :], preferred_element_type=F32)
    acc = acc + jnp.dot(ob_ref[...].astype(BF16), w_ref[A_Q_W:, :], preferred_element_type=F32)
    o_ref[...] = x_ref[...] + acc


def _out_proj_odd_kernel(x_ref, oc_ref, o0_ref, o1_ref, o2_ref, l0_ref, l1_ref, l2_ref, w_ref, o_ref):
    l0, l1, l2 = l0_ref[...], l1_ref[...], l2_ref[...]
    m = jnp.maximum(jnp.maximum(l0, l1), l2)
    e0, e1, e2 = jnp.exp(l0 - m), jnp.exp(l1 - m), jnp.exp(l2 - m)
    od = (o0_ref[...] * e0 + o1_ref[...] * e1 + o2_ref[...] * e2) / (e0 + e1 + e2)
    acc = jnp.dot(oc_ref[...].astype(BF16), w_ref[0:C_WIDTH, :], preferred_element_type=F32)
    acc = acc + jnp.dot(od.astype(BF16), w_ref[C_WIDTH:, :], preferred_element_type=F32)
    o_ref[...] = x_ref[...] + acc


def _out_proj(kernel, x, parts, w, name, in_place=True):
    n = x.shape[0]
    tm = 512
    return pl.pallas_call(
        kernel,
        grid=(n // tm,),
        in_specs=[pl.BlockSpec((tm, D_MODEL), lambda i: (i, 0))]
        + [pl.BlockSpec((tm, p.shape[1]), lambda i: (i, 0)) for p in parts]
        + [pl.BlockSpec(w.shape, lambda i: (0, 0))],
        out_specs=pl.BlockSpec((tm, D_MODEL), lambda i: (i, 0)),
        out_shape=jax.ShapeDtypeStruct((n, D_MODEL), F32),
        input_output_aliases={0: 0} if in_place else {},
        compiler_params=_cparams("parallel"),
        name=name,
    )(x, *parts, w)


MOE_TM = 512


def _ffn_norm(x, g):
    y = x * lax.rsqrt(jnp.mean(x * x, axis=-1, keepdims=True) + RMS_EPS)
    return (y * g).astype(BF16)


def _route(xb, rw, rb):
    logits = jnp.dot(xb, rw, preferred_element_type=F32) + rb
    lane = lax.broadcasted_iota(jnp.int32, logits.shape, 1).astype(F32)
    ninf = float("-inf")
    far = float(ROUTER_LANES)
    lg = jnp.where(lane < MOE_GROUPS, logits, ninf)
    gmax = jnp.max(lg, axis=1, keepdims=True)
    g_idx = jnp.min(jnp.where(lg == gmax, lane, far), axis=1, keepdims=True)
    g_w = 1.0 / jnp.sum(jnp.exp(lg - gmax), axis=1, keepdims=True)
    lane_grp = jnp.floor((lane - MOE_GROUPS) * (1.0 / MOE_PER_GROUP))
    in_grp = (lane >= MOE_GROUPS) & (lane < MOE_GROUPS + MOE_EXPERTS) & (lane_grp == g_idx)
    le = jnp.where(in_grp, logits, ninf)
    v1 = jnp.max(le, axis=1, keepdims=True)
    i1 = jnp.min(jnp.where(le == v1, lane, far), axis=1, keepdims=True)
    le2 = jnp.where(lane == i1, ninf, le)
    v2 = jnp.max(le2, axis=1, keepdims=True)
    i2 = jnp.min(jnp.where(le2 == v2, lane, far), axis=1, keepdims=True)
    e2 = jnp.exp(v2 - v1)
    w1 = g_w / (1.0 + e2)
    w2 = g_w * e2 / (1.0 + e2)
    return lane, g_idx, i1 - MOE_GROUPS, i2 - MOE_GROUPS, w1, w2


def _moe_kernel(x_ref, g_ref, rw_ref, rb_ref, wg_ref, wu_ref, wd_ref, o_ref, xn_ref, gates_ref, acc_ref):
    e = pl.program_id(1)

    @pl.when(e == 0)
    def _():
        xb0 = _ffn_norm(x_ref[...], g_ref[...])
        xn_ref[...] = xb0
        lane, _, e1, e2, w1, w2 = _route(xb0, rw_ref[...], rb_ref[...])
        gates_ref[...] = jnp.where(lane == e1, w1, 0.0) + jnp.where(lane == e2, w2, 0.0)
        acc_ref[...] = jnp.zeros_like(acc_ref)

    xb = xn_ref[...]
    gates = gates_ref[...]
    lane_i = lax.broadcasted_iota(jnp.int32, gates.shape, 1)
    gate = jnp.sum(jnp.where(lane_i == e, gates, 0.0), axis=1, keepdims=True)
    hg = jnp.dot(xb, wg_ref[0], preferred_element_type=F32)
    hu = jnp.dot(xb, wu_ref[0], preferred_element_type=F32)
    hid = jax.nn.silu(hg) * hu * gate
    acc_ref[...] += jnp.dot(hid.astype(BF16), wd_ref[0], preferred_element_type=F32)

    @pl.when(e == MOE_EXPERTS - 1)
    def _():
        o_ref[...] = x_ref[...] + acc_ref[...]


def _moe(x, prm, name):
    n = x.shape[0]
    tm = MOE_TM
    return pl.pallas_call(
        _moe_kernel,
        grid=(n // tm, MOE_EXPERTS),
        in_specs=[pl.BlockSpec((tm, D_MODEL), lambda i, e: (i, 0)),
                  pl.BlockSpec((1, D_MODEL), lambda i, e: (0, 0)),
                  pl.BlockSpec((D_MODEL, ROUTER_LANES), lambda i, e: (0, 0)),
                  pl.BlockSpec((1, ROUTER_LANES), lambda i, e: (0, 0)),
                  pl.BlockSpec((1, D_MODEL, MOE_FF), lambda i, e: (e, 0, 0)),
                  pl.BlockSpec((1, D_MODEL, MOE_FF), lambda i, e: (e, 0, 0)),
                  pl.BlockSpec((1, MOE_FF, D_MODEL), lambda i, e: (e, 0, 0))],
        out_specs=pl.BlockSpec((tm, D_MODEL), lambda i, e: (i, 0)),
        out_shape=jax.ShapeDtypeStruct((n, D_MODEL), F32),
        scratch_shapes=[pltpu.VMEM((tm, D_MODEL), BF16),
                        pltpu.VMEM((tm, ROUTER_LANES), F32),
                        pltpu.VMEM((tm, D_MODEL), F32)],
        input_output_aliases={0: 0},
        compiler_params=_cparams("parallel", "arbitrary"),
        name=name,
    )(x, prm["g"], prm["rw"], prm["rb"], prm["wg"], prm["wu"], prm["wd"])


def _moe_params(norm_g, router_g, router_g_b, router_e, router_e_b, w_gate, w_up, w_down):
    used = MOE_GROUPS + MOE_EXPERTS
    rw = jnp.concatenate([router_g.astype(F32), router_e.astype(F32),
                          jnp.zeros((D_MODEL, ROUTER_LANES - used), F32)], axis=1)
    rb = jnp.concatenate([router_g_b.astype(F32), router_e_b.astype(F32),
                          jnp.zeros((ROUTER_LANES - used,), F32)]).reshape(1, ROUTER_LANES)
    return {"g": norm_g.astype(F32).reshape(1, D_MODEL), "rw": rw.astype(BF16), "rb": rb,
            "wg": w_gate.astype(BF16), "wu": w_up.astype(BF16), "wd": w_down.astype(BF16)}


MOE_AUG_W = D_MODEL + ROUTER_LANES
GATE_LANE0 = 1


def _moe_route_kernel(x_ref, g_ref, rw_ref, rb_ref, o_ref):
    x = x_ref[...]
    lane, g_idx, e1, e2, w1, w2 = _route(_ffn_norm(x, g_ref[...]), rw_ref[...], rb_ref[...])
    l1 = e1 - g_idx * MOE_PER_GROUP + GATE_LANE0
    l2 = e2 - g_idx * MOE_PER_GROUP + GATE_LANE0
    o_ref[:, 0:D_MODEL] = x
    o_ref[:, D_MODEL:] = (jnp.where(lane == 0.0, g_idx, 0.0)
                          + jnp.where(lane == l1, w1, 0.0) + jnp.where(lane == l2, w2, 0.0))


def _moe_route(x, prm, name):
    n = x.shape[0]
    tm = MOE_TM
    return pl.pallas_call(
        _moe_route_kernel,
        grid=(n // tm,),
        in_specs=[pl.BlockSpec((tm, D_MODEL), lambda i: (i, 0)),
                  pl.BlockSpec((1, D_MODEL), lambda i: (0, 0)),
                  pl.BlockSpec((D_MODEL, ROUTER_LANES), lambda i: (0, 0)),
                  pl.BlockSpec((1, ROUTER_LANES), lambda i: (0, 0))],
        out_specs=pl.BlockSpec((tm, MOE_AUG_W), lambda i: (i, 0)),
        out_shape=jax.ShapeDtypeStruct((n, MOE_AUG_W), F32),
        compiler_params=_cparams("parallel"),
        name=name,
    )(x, prm["g"], prm["rw"], prm["rb"])


def _moe_plan(group_col, n):
    tm = MOE_TM
    n_tiles_max = n // tm + MOE_GROUPS
    g = group_col.astype(jnp.int32)
    onehot = (g[:, None] == jnp.arange(MOE_GROUPS)[None, :]).astype(jnp.int32)
    counts = jnp.sum(onehot, axis=0)
    rank = jnp.sum((jnp.cumsum(onehot, axis=0) - onehot) * onehot, axis=1)
    padded = (counts + tm - 1) // tm * tm
    ends = jnp.cumsum(padded)
    starts = ends - padded
    pos = jnp.sum(onehot * starts[None, :], axis=1) + rank
    src = jnp.zeros((n_tiles_max * tm,), jnp.int32).at[pos].set(jnp.arange(n, dtype=jnp.int32))
    tile0 = jnp.arange(n_tiles_max, dtype=jnp.int32) * tm
    tile_group = jnp.minimum(jnp.sum((tile0[:, None] >= ends[None, :]).astype(jnp.int32), axis=1),
                             MOE_GROUPS - 1)
    real = jnp.clip(counts[tile_group] - (tile0 - starts[tile_group]), 0, tm)
    real = jnp.where(tile0 < ends[-1], real, 0).astype(jnp.int32)
    return src, tile_group.astype(jnp.int32), real, (ends[-1:] // tm).astype(jnp.int32)


def _moe_expert_kernel(src_ref, grp_ref, real_ref, nt_ref, xa_hbm, g_ref, wg_ref, wu_ref, wd_ref, out_hbm,
                       xbuf, ybuf, gsem, ssem):
    del grp_ref
    tm = MOE_TM
    i = pl.program_id(0)
    n_tiles = nt_ref[0]
    slot = lax.rem(i, 2)

    def row_in(tile, s, r):
        return pltpu.make_async_copy(xa_hbm.at[pl.ds(src_ref[tile * tm + r], 1)],
                                     xbuf.at[s, pl.ds(r, 1)], gsem.at[s])

    def row_out(tile, s, r):
        return pltpu.make_async_copy(ybuf.at[s, pl.ds(r, 1)],
                                     out_hbm.at[pl.ds(src_ref[tile * tm + r], 1)], ssem.at[s])

    def start_gather(tile, s):
        def body(r, c):
            row_in(tile, s, r).start()
            return c
        lax.fori_loop(0, tm, body, 0)

    def wait_gather(s):
        pltpu.make_async_copy(xa_hbm.at[pl.ds(0, tm)], xbuf.at[s], gsem.at[s]).wait()

    def start_scatter(tile, s):
        def body(r, c):
            row_out(tile, s, r).start()
            return c
        lax.fori_loop(0, real_ref[tile], body, 0)

    def wait_scatter(tile, s):
        real = real_ref[tile]
        for k in range(tm.bit_length()):
            size = 1 << k

            @pl.when((real >> k) & 1 == 1)
            def _():
                pltpu.make_async_copy(ybuf.at[s, pl.ds(0, size)], out_hbm.at[pl.ds(0, size)], ssem.at[s]).wait()

    @pl.when(i < n_tiles)
    def _():
        @pl.when(i == 0)
        def _():
            start_gather(0, 0)

        @pl.when(i + 1 < n_tiles)
        def _():
            start_gather(i + 1, 1 - slot)

        wait_gather(slot)

        @pl.when(i >= 2)
        def _():
            wait_scatter(i - 2, slot)

        xa = xbuf[slot]
        x = xa[:, 0:D_MODEL]
        route = xa[:, D_MODEL:]
        xb = _ffn_norm(x, g_ref[...])
        acc = jnp.zeros((tm, D_MODEL), F32)
        for e in range(MOE_PER_GROUP):
            gate = route[:, GATE_LANE0 + e:GATE_LANE0 + e + 1]
            hg = jnp.dot(xb, wg_ref[0, e], preferred_element_type=F32)
            hu = jnp.dot(xb, wu_ref[0, e], preferred_element_type=F32)
            hid = jax.nn.silu(hg) * hu * gate
            acc = acc + jnp.dot(hid.astype(BF16), wd_ref[0, e], preferred_element_type=F32)
        ybuf[slot] = x + acc
        start_scatter(i, slot)

        @pl.when(i == n_tiles - 1)
        def _():
            wait_scatter(i, slot)

            @pl.when(i >= 1)
            def _():
                wait_scatter(i - 1, 1 - slot)


def _moe_experts(xa, plan, prm, name):
    n = xa.shape[0]
    tm = MOE_TM
    src, tile_group, real, n_tiles = plan
    grouped = lambda w: w.reshape(MOE_GROUPS, MOE_PER_GROUP, w.shape[1], w.shape[2])
    w_spec = lambda a, b: pl.BlockSpec((1, MOE_PER_GROUP, a, b), lambda i, src, grp, real, nt: (grp[i], 0, 0, 0))
    grid_spec = pltpu.PrefetchScalarGridSpec(
        num_scalar_prefetch=4,
        grid=(n // tm + MOE_GROUPS,),
        in_specs=[pl.BlockSpec(memory_space=pl.ANY),
                  pl.BlockSpec((1, D_MODEL), lambda i, *_: (0, 0)),
                  w_spec(D_MODEL, MOE_FF), w_spec(D_MODEL, MOE_FF), w_spec(MOE_FF, D_MODEL)],
        out_specs=pl.BlockSpec(memory_space=pl.ANY),
        scratch_shapes=[pltpu.VMEM((2, tm, MOE_AUG_W), F32),
                        pltpu.VMEM((2, tm, D_MODEL), F32),
                        pltpu.SemaphoreType.DMA((2,)),
                        pltpu.SemaphoreType.DMA((2,))])
    return pl.pallas_call(
        _moe_expert_kernel,
        grid_spec=grid_spec,
        out_shape=jax.ShapeDtypeStruct((n, D_MODEL), F32),
        compiler_params=_cparams("arbitrary"),
        name=name,
    )(src, tile_group, real, n_tiles, xa, prm["g"], grouped(prm["wg"]), grouped(prm["wu"]), grouped(prm["wd"]))


def _final_norm_kernel(x_ref, g_ref, o_ref):
    x = x_ref[...]
    y = x * lax.rsqrt(jnp.mean(x * x, axis=-1, keepdims=True) + RMS_EPS)
    o_ref[...] = y * g_ref[...]


def _final_norm(x, g, name):
    n = x.shape[0]
    tm = 512
    return pl.pallas_call(
        _final_norm_kernel,
        grid=(n // tm,),
        in_specs=[pl.BlockSpec((tm, D_MODEL), lambda i: (i, 0)),
                  pl.BlockSpec((1, D_MODEL), lambda i: (0, 0))],
        out_specs=pl.BlockSpec((tm, D_MODEL), lambda i: (i, 0)),
        out_shape=jax.ShapeDtypeStruct((n, D_MODEL), F32),
        compiler_params=_cparams("parallel"),
        name=name,
    )(x, g.astype(F32).reshape(1, D_MODEL))


def _t5_bucket(dist):
    n = jnp.maximum(dist, 0)
    max_exact = T5_BUCKETS // 2
    nf = jnp.maximum(n, 1).astype(F32)
    large = max_exact + (jnp.log(nf / max_exact) / math.log(T5_MAX_DIST / max_exact)
                         * (T5_BUCKETS - max_exact)).astype(jnp.int32)
    return jnp.where(n < max_exact, n, jnp.minimum(large, T5_BUCKETS - 1))


def _rel_bias(table, dist):
    hit = _t5_bucket(dist)[..., None] == jnp.arange(T5_BUCKETS)
    rows = jnp.sum(jnp.where(hit[..., None], table.astype(F32), 0.0), axis=-2)
    return jnp.moveaxis(rows, -1, 0)


def _band_bias_mask(table, max_dist, dil):
    r = jnp.arange(BLK)[:, None]
    s = jnp.arange(2 * BLK)[None, :]
    dist = BLK + r - s
    valid = (dist >= 0) & (dist <= max_dist)
    bias = _rel_bias(table, dist * dil)
    later = jnp.where(valid[None], bias, NEG)
    first = jnp.where((valid & (s >= BLK))[None], bias, NEG)
    return jnp.stack([first, later])


def _swa_sample_bias_mask(table):
    s = jnp.arange(DEC_SEQ)[:, None]
    col = jnp.arange(BLK)[None, :]
    dist_c = A_WINDOW + s - col
    dist_n = s - col
    tab = table[:, :A_Q_HEADS]
    bc = jnp.where(((dist_c >= 0) & (dist_c < A_WINDOW))[None], _rel_bias(tab, dist_c), NEG)
    bn = jnp.where(((dist_n >= 0) & (col < DEC_SEQ))[None], _rel_bias(tab, dist_n), NEG)
    return bc.reshape(A_Q_HEADS * DEC_SEQ, A_WINDOW), bn.reshape(A_Q_HEADS * DEC_SEQ, BLK)


def _dil_sample_bias_mask(table):
    s = jnp.arange(DEC_SEQ)[:, None]
    col = jnp.arange(BLK)[None, :]
    bcs, bns = [], []
    for g, (win, dil) in enumerate(D_PAIRS):
        lo = A_Q_HEADS + g * D_HEADS_PER
        tab = table[:, lo:lo + D_HEADS_PER]
        dist_c = win + s - jnp.arange(win)[None, :]
        valid_c = (dist_c >= 0) & (dist_c % dil == 0) & (dist_c <= win)
        bc = jnp.where(valid_c[None], _rel_bias(tab, dist_c), NEG)
        dist_n = s - col
        valid_n = (dist_n >= 0) & (dist_n % dil == 0) & (col < DEC_SEQ)
        bn = jnp.where(valid_n[None], _rel_bias(tab, dist_n), NEG)
        bcs.append(bc.reshape(1, D_HEADS_PER * DEC_SEQ, win))
        bns.append(bn.reshape(D_HEADS_PER * DEC_SEQ, BLK))
    return bcs, jnp.stack(bns)


def _native_cache(c):
    n_layers, nb, wc = c.shape[:3]
    return c.transpose(0, 1, 3, 4, 5, 2).reshape(n_layers, nb, -1, wc)


def _logical_cache(c, heads):
    n_layers, nb, _, wc = c.shape
    return c.reshape(n_layers, nb, 2, heads, HEAD_DIM, wc).transpose(0, 1, 5, 2, 3, 4)


def kernel(x_prompt, x_sample, cache_a_kv, state_b, state_c_h, state_c_conv, cache_d_g0, cache_d_g1, cache_d_g2, rel_table, norm_mix, norm_ffn, norm_final, w_in_even, w_out_even, sinks_a, s5_a_re, s5_a_im, s5_log_dt, s5_b_re, s5_b_im, s5_c_re, s5_c_im, s5_d, s5_glu_w, s5_glu_b, w_in_odd, w_out_odd, conv_w, conv_b, gate_a_w, gate_a_b, gate_x_w, gate_x_b, lru_lambda, moe_router_g, moe_router_g_b, moe_router_e, moe_router_e_b, moe_w_gate, moe_w_up, moe_w_down):
    xp = x_prompt.astype(F32).reshape(NP_TOK, D_MODEL)
    xs = x_sample.astype(F32).reshape(NS_TOK, D_MODEL)

    cache_a = _native_cache(cache_a_kv)
    caches_d = [_native_cache(c) for c in (cache_d_g0, cache_d_g1, cache_d_g2)]

    bm_a = _band_bias_mask(rel_table[:, :A_Q_HEADS], A_WINDOW - 1, 1)
    bm_d = [_band_bias_mask(rel_table[:, A_Q_HEADS + g * D_HEADS_PER:A_Q_HEADS + (g + 1) * D_HEADS_PER],
                            win // dil, dil) for g, (win, dil) in enumerate(D_PAIRS)]
    bmc_a_s, bmn_a_s = _swa_sample_bias_mask(rel_table)
    bcs_d_s, bn_d_s = _dil_sample_bias_mask(rel_table)

    a_p, a_new, b_p, b_s = [], [], [], []
    ch_p, ch_s, cc_p, cc_s = [], [], [], []
    d_p = [[], [], []]
    d_new = [[], [], []]

    for layer in range(DEPTH):
        if layer % 2 == 0:
            e = layer // 2
            w_in = w_in_even[e].astype(BF16)
            w_out = w_out_even[e].astype(BF16)
            splits = (A_Q_W, 2 * A_KV_W, B_WIDTH)
            s5p = _s5_params(s5_a_re[e], s5_a_im[e], s5_log_dt[e], s5_b_re[e], s5_b_im[e],
                             s5_c_re[e], s5_c_im[e], s5_d[e], s5_glu_w[e], s5_glu_b[e])
            sinks = sinks_a[e].astype(F32)

            q_p, kv_p, u_p = _norm_proj(xp, norm_mix[layer], w_in, splits, f"in_proj_p{layer}")
            q_s, kv_s, u_s = _norm_proj(xs, norm_mix[layer], w_in, splits, f"in_proj_s{layer}")

            oa_p = _swa_prompt(q_p, kv_p, sinks, bm_a)
            oa_s = _swa_sample(q_s, kv_s, cache_a, e, sinks, bmc_a_s, bmn_a_s)

            ob_p, hl_p = _s5_mixer(u_p, jnp.zeros((BATCH, B_STATE_W), F32), s5p, BATCH, SEQ, 64, "s5_prompt")
            ob_s, hl_s = _s5_mixer(u_s, _s5_state_to_cols(state_b[e]), s5p, DEC_BATCH, DEC_SEQ, DEC_SEQ,
                                   "s5_sample")

            xp = _out_proj(_out_proj_even_kernel, xp, [oa_p, ob_p], w_out, f"out_proj_p{layer}",
                           in_place=layer > 0)
            xs = _out_proj(_out_proj_even_kernel, xs, [oa_s, ob_s], w_out, f"out_proj_s{layer}",
                           in_place=layer > 0)

            a_p.append(kv_p.reshape(BATCH, SEQ, 2, A_KV_HEADS, HEAD_DIM)[:, SEQ - A_WINDOW:])
            a_new.append(kv_s.reshape(DEC_BATCH, DEC_SEQ, 2 * A_KV_W))
            b_p.append(_s5_cols_to_state(hl_p))
            b_s.append(_s5_cols_to_state(hl_s))
        else:
            o = layer // 2
            w_in = w_in_odd[o].astype(BF16)
            w_out = w_out_odd[o].astype(BF16)
            splits = (C_WIDTH, C_WIDTH, D_QKV_W, D_QKV_W, D_QKV_W)
            lrp = _rglru_params(conv_w[o], conv_b[o], gate_a_w[o], gate_a_b[o], gate_x_w[o], gate_x_b[o],
                                lru_lambda[o])

            xr_p, gate_p, q_p, k_p, v_p = _norm_proj(xp, norm_mix[layer], w_in, splits, f"in_proj_p{layer}")
            xr_s, gate_s, q_s, k_s, v_s = _norm_proj(xs, norm_mix[layer], w_in, splits, f"in_proj_s{layer}")

            od_p, lse_p = [], []
            for g, (win, dil) in enumerate(D_PAIRS):
                og, lg = _dilated_prompt_group(q_p, k_p, v_p, bm_d[g], g, dil)
                od_p.append(og)
                lse_p.append(lg)
            od_s, lse_s = _dilated_sample(q_s, k_s, v_s, [c for c in caches_d], o, bcs_d_s, bn_d_s)

            oc_p, hc_p = _rglru_mixer(xr_p, gate_p, jnp.zeros(((C_CONV - 1) * BATCH, C_WIDTH), F32),
                                      jnp.zeros((BATCH, C_WIDTH), F32), lrp, BATCH, SEQ, 128, "rglru_prompt")
            conv_s = state_c_conv[o].astype(F32).transpose(1, 0, 2).reshape((C_CONV - 1) * DEC_BATCH, C_WIDTH)
            oc_s, hc_s = _rglru_mixer(xr_s, gate_s, conv_s, state_c_h[o].astype(F32), lrp,
                                      DEC_BATCH, DEC_SEQ, DEC_SEQ, "rglru_sample")

            xp = _out_proj(_out_proj_odd_kernel, xp, [oc_p] + od_p + lse_p, w_out, f"out_proj_p{layer}")
            xs = _out_proj(_out_proj_odd_kernel, xs, [oc_s] + list(od_s) + list(lse_s), w_out,
                           f"out_proj_s{layer}")

            ch_p.append(hc_p)
            ch_s.append(hc_s)
            cc_p.append(xr_p.reshape(BATCH, SEQ, C_WIDTH)[:, SEQ - (C_CONV - 1):])
            cc_s.append(xr_s.reshape(DEC_BATCH, DEC_SEQ, C_WIDTH)[:, DEC_SEQ - (C_CONV - 1):])
            k_p5 = k_p.reshape(BATCH, SEQ, D_N_GROUPS, D_HEADS_PER, HEAD_DIM)
            v_p5 = v_p.reshape(BATCH, SEQ, D_N_GROUPS, D_HEADS_PER, HEAD_DIM)
            k_s4 = k_s.reshape(DEC_BATCH, DEC_SEQ, D_N_GROUPS, D_GROUP_W)
            v_s4 = v_s.reshape(DEC_BATCH, DEC_SEQ, D_N_GROUPS, D_GROUP_W)
            for g, (win, dil) in enumerate(D_PAIRS):
                wc = min(win, SEQ)
                d_p[g].append(jnp.stack([k_p5[:, SEQ - wc:, g], v_p5[:, SEQ - wc:, g]], axis=2))
                d_new[g].append(jnp.concatenate([k_s4[:, :, g], v_s4[:, :, g]], axis=-1))

        mp = _moe_params(norm_ffn[layer], moe_router_g[layer], moe_router_g_b[layer], moe_router_e[layer],
                         moe_router_e_b[layer], moe_w_gate[layer], moe_w_up[layer], moe_w_down[layer])
        xa = _moe_route(xp, mp, f"moe_route_p{layer}")
        xp = _moe_experts(xa, _moe_plan(xa[:, D_MODEL], NP_TOK), mp, f"moe_experts_p{layer}")
        xs = _moe(xs, mp, f"moe_s{layer}")

    y_prompt = _final_norm(xp, norm_final, "final_norm_p").reshape(BATCH, SEQ, D_MODEL)
    y_sample = _final_norm(xs, norm_final, "final_norm_s").reshape(DEC_BATCH, DEC_SEQ, D_MODEL)

    new_a = _logical_cache(_shift_cache(cache_a, jnp.stack(a_new), "shift_cache_a"), A_KV_HEADS)
    new_d = [_logical_cache(_shift_cache(caches_d[g], jnp.stack(d_new[g]), f"shift_cache_d{g}"), D_HEADS_PER)
             for g in range(D_N_GROUPS)]

    return (y_prompt, y_sample,
            jnp.stack(a_p), new_a, jnp.stack(b_p), jnp.stack(b_s),
            jnp.stack(ch_p), jnp.stack(ch_s), jnp.stack(cc_p), jnp.stack(cc_s),
            jnp.stack(d_p[0]), new_d[0], jnp.stack(d_p[1]), new_d[1], jnp.stack(d_p[2]), new_d[2])
```

```python
import functools
import math

import jax
import jax.numpy as jnp
from jax import lax
from jax.experimental import pallas as pl
from jax.experimental.pallas import tpu as pltpu

F32 = jnp.float32
BF16 = jnp.bfloat16

D_MODEL = 1024
BATCH = 8
SEQ = 2048
DEPTH = 4
DEC_BATCH = 128
DEC_SEQ = 8
HEAD_DIM = 64
BLK = 128
RMS_EPS = 1e-6
NEG = -1e30

A_Q_HEADS = 8
A_KV_HEADS = 2
A_GQA = 4
A_WINDOW = 128
A_Q_W = A_Q_HEADS * HEAD_DIM
A_KV_W = A_KV_HEADS * HEAD_DIM

B_WIDTH = 512
B_GROUP_CH = 16
B_GROUPS = 32
B_STATE = 64
B_LANE_BLOCKS = 4
B_GROUPS_PER_BLOCK = B_GROUPS // B_LANE_BLOCKS
B_HALF = B_GROUPS_PER_BLOCK * B_STATE
B_STATE_W = B_LANE_BLOCKS * 2 * B_HALF

C_WIDTH = 512
C_BLOCKS = 8
C_BLOCK_W = 64
C_CONV = 4
C_POWER = 8.0

D_PAIRS = ((128, 1), (512, 4), (2048, 16))
D_N_GROUPS = 3
D_HEADS_PER = 4
D_GROUP_W = D_HEADS_PER * HEAD_DIM
D_QKV_W = D_N_GROUPS * D_GROUP_W

T5_BUCKETS = 32
T5_MAX_DIST = 2048

MOE_GROUPS = 4
MOE_PER_GROUP = 4
MOE_EXPERTS = 16
MOE_FF = 256
ROUTER_LANES = 128

NP_TOK = BATCH * SEQ
NS_TOK = DEC_BATCH * DEC_SEQ

VMEM_LIMIT_BYTES = 52 * 2 ** 20


def _cparams(*sem):
    return pltpu.CompilerParams(dimension_semantics=sem, vmem_limit_bytes=VMEM_LIMIT_BYTES)


def _nt_dot(a, b):
    return lax.dot_general(a, b, (((1,), (1,)), ((), ())), preferred_element_type=F32)


LANES = 128
SUBLANES = 8


def _norm_proj_kernel(x_ref, g_ref, w_ref, *out_refs, splits):
    x = x_ref[...]
    y = x * lax.rsqrt(jnp.mean(x * x, axis=-1, keepdims=True) + RMS_EPS)
    xn = (y * g_ref[...]).astype(BF16)
    off = 0
    for o_ref, width in zip(out_refs, splits):
        o_ref[...] = jnp.dot(xn, w_ref[:, off:off + width], preferred_element_type=F32)
        off += width


def _norm_proj(x, g, w, splits, name):
    n = x.shape[0]
    tm = 512
    return pl.pallas_call(
        functools.partial(_norm_proj_kernel, splits=splits),
        grid=(n // tm,),
        in_specs=[pl.BlockSpec((tm, D_MODEL), lambda i: (i, 0)),
                  pl.BlockSpec((1, D_MODEL), lambda i: (0, 0)),
                  pl.BlockSpec(w.shape, lambda i: (0, 0))],
        out_specs=[pl.BlockSpec((tm, s), lambda i: (i, 0)) for s in splits],
        out_shape=[jax.ShapeDtypeStruct((n, s), F32) for s in splits],
        compiler_params=_cparams("parallel"),
        name=name,
    )(x, g.reshape(1, D_MODEL), w)


def _softmax_parts(scores, sink):
    m = jnp.max(scores[0], axis=1, keepdims=True)
    for s in scores[1:]:
        m = jnp.maximum(m, jnp.max(s, axis=1, keepdims=True))
    if sink is not None:
        m = jnp.maximum(m, sink)
    ps = [jnp.exp(s - m) for s in scores]
    den = jnp.sum(ps[0], axis=1, keepdims=True)
    for p in ps[1:]:
        den = den + jnp.sum(p, axis=1, keepdims=True)
    if sink is not None:
        den = den + jnp.exp(sink - m)
    inv = 1.0 / den
    return [(p * inv).astype(BF16) for p in ps], den, m


def _band_attn_kernel(*refs, n_heads, gqa, with_sink, dil, has_prev):
    refs = list(refs)
    sink_ref = refs.pop(0) if with_sink else None
    q_ref, kc_ref = refs.pop(0), refs.pop(0)
    kp_ref = refs.pop(0) if has_prev else None
    vc_ref = refs.pop(0)
    vp_ref = refs.pop(0) if has_prev else None
    bm_ref, o_ref = refs.pop(0), refs.pop(0)
    lse_ref = refs.pop(0) if refs else None
    n_kv = n_heads // gqa
    row_sel = [slice(None) if dil == 1 else pl.ds(r, BLK, stride=dil) for r in range(dil)]
    sc, sp, v_cur, v_prev = [], [], [], []
    for rows in row_sel:
        q = (q_ref[rows, :] * HEAD_DIM ** -0.5).astype(BF16)
        kc = kc_ref[rows, :].astype(BF16)
        vc = vc_ref[rows, :].astype(BF16)
        if has_prev:
            kp = kp_ref[rows, :].astype(BF16)
            vp = vp_ref[rows, :].astype(BF16)
        for hk in range(n_kv):
            ksl = slice(hk * HEAD_DIM, (hk + 1) * HEAD_DIM)
            qs = jnp.concatenate([q[:, h * HEAD_DIM:(h + 1) * HEAD_DIM]
                                  for h in range(hk * gqa, (hk + 1) * gqa)], axis=0)
            sc.append(_nt_dot(qs, kc[:, ksl]))
            v_cur.append(vc[:, ksl])
            if has_prev:
                sp.append(_nt_dot(qs, kp[:, ksl]))
                v_prev.append(vp[:, ksl])
    bias_c = jnp.concatenate([bm_ref[0, h, :, BLK:2 * BLK] for h in range(n_heads)] * dil, axis=0)
    scores = [jnp.concatenate(sc, axis=0) + bias_c]
    if has_prev:
        bias_p = jnp.concatenate([bm_ref[0, h, :, 0:BLK] for h in range(n_heads)] * dil, axis=0)
        scores.append(jnp.concatenate(sp, axis=0) + bias_p)
    sink = None
    if with_sink:
        sink = jnp.concatenate([jnp.full((BLK, 1), sink_ref[h], F32) for h in range(n_heads)] * dil, axis=0)
    ps, den, m = _softmax_parts(scores, sink)
    lse = m + jnp.log(den) if lse_ref is not None else None
    unit_rows = gqa * BLK
    for r, rows in enumerate(row_sel):
        outs = []
        for hk in range(n_kv):
            u = r * n_kv + hk
            usl = slice(u * unit_rows, (u + 1) * unit_rows)
            o = jnp.dot(ps[0][usl, :], v_cur[u], preferred_element_type=F32)
            if has_prev:
                o = o + jnp.dot(ps[1][usl, :], v_prev[u], preferred_element_type=F32)
            outs += [o[j * BLK:(j + 1) * BLK, :] for j in range(gqa)]
        o_ref[rows, :] = jnp.concatenate(outs, axis=1)
        if lse_ref is not None:
            base = r * n_heads * BLK
            lse_ref[rows, :] = jnp.concatenate(
                [jnp.broadcast_to(lse[base + h * BLK:base + (h + 1) * BLK, :], (BLK, HEAD_DIM))
                 for h in range(n_heads)], axis=1)


def _swa_prompt(q, kv, sinks, bm):
    nblk = SEQ // BLK
    row = lambda b, i: b * nblk + i
    prev = lambda b, i: b * nblk + jnp.maximum(i - 1, 0)
    return pl.pallas_call(
        functools.partial(_band_attn_kernel, n_heads=A_Q_HEADS, gqa=A_GQA, with_sink=True, dil=1,
                          has_prev=True),
        grid=(BATCH, nblk),
        in_specs=[pl.BlockSpec(memory_space=pltpu.SMEM),
                  pl.BlockSpec((BLK, A_Q_W), lambda b, i: (row(b, i), 0)),
                  pl.BlockSpec((BLK, A_KV_W), lambda b, i: (row(b, i), 0)),
                  pl.BlockSpec((BLK, A_KV_W), lambda b, i: (prev(b, i), 0)),
                  pl.BlockSpec((BLK, A_KV_W), lambda b, i: (row(b, i), 1)),
                  pl.BlockSpec((BLK, A_KV_W), lambda b, i: (prev(b, i), 1)),
                  pl.BlockSpec((1, A_Q_HEADS, BLK, 2 * BLK), lambda b, i: (jnp.minimum(i, 1), 0, 0, 0))],
        out_specs=pl.BlockSpec((BLK, A_Q_W), lambda b, i: (row(b, i), 0)),
        out_shape=jax.ShapeDtypeStruct((NP_TOK, A_Q_W), F32),
        compiler_params=_cparams("parallel", "parallel"),
        name="swa_prompt",
    )(sinks, q, kv, kv, kv, kv, bm)


def _dilated_prompt_group(q, k, v, bm, g, dil):
    rows = BLK * dil
    nchunk = SEQ // rows
    has_prev = nchunk > 1
    pair = LANES // HEAD_DIM if dil > 1 else D_HEADS_PER
    npair = D_HEADS_PER // pair
    width = pair * HEAD_DIM
    row = lambda b, i: b * nchunk + i
    prev = lambda b, i: b * nchunk + jnp.maximum(i - 1, 0)
    cur_spec = pl.BlockSpec((rows, width), lambda b, i, p: (row(b, i), g * npair + p))
    prev_spec = pl.BlockSpec((rows, width), lambda b, i, p: (prev(b, i), g * npair + p))
    out_spec = pl.BlockSpec((rows, width), lambda b, i, p: (row(b, i), p))
    bm_spec = pl.BlockSpec((1, pair, BLK, 2 * BLK), lambda b, i, p: (jnp.minimum(i, 1), p, 0, 0))
    if has_prev:
        in_specs, args = [cur_spec, cur_spec, prev_spec, cur_spec, prev_spec, bm_spec], (q, k, k, v, v, bm)
    else:
        in_specs, args = [cur_spec, cur_spec, cur_spec, bm_spec], (q, k, v, bm)
    return pl.pallas_call(
        functools.partial(_band_attn_kernel, n_heads=pair, gqa=1, with_sink=False, dil=dil,
                          has_prev=has_prev),
        grid=(BATCH, nchunk, npair),
        in_specs=in_specs,
        out_specs=[out_spec, out_spec],
        out_shape=[jax.ShapeDtypeStruct((NP_TOK, D_GROUP_W), F32)] * 2,
        compiler_params=_cparams("parallel", "parallel", "parallel"),
        name=f"dilated_prompt_g{g}",
    )(*args)


SWA_SAMPLE_BB = 8


def _pad_rows(x, rows):
    return jnp.concatenate([x, jnp.zeros((rows - x.shape[0], x.shape[1]), x.dtype)], axis=0)


def _swa_sample_kernel(sink_ref, q_ref, kv_ref, c_ref, bmc_ref, bmn_ref, o_ref):
    for bb in range(SWA_SAMPLE_BB):
        rs = slice(bb * DEC_SEQ, (bb + 1) * DEC_SEQ)
        q = (q_ref[rs, :] * HEAD_DIM ** -0.5).astype(BF16)
        kvn = _pad_rows(kv_ref[rs, :], BLK).astype(BF16)
        cache = c_ref[0, bb].astype(BF16)
        outs = [None] * A_Q_HEADS
        for hk in range(A_KV_HEADS):
            heads = range(hk * A_GQA, (hk + 1) * A_GQA)
            qs = jnp.concatenate([q[:, h * HEAD_DIM:(h + 1) * HEAD_DIM] for h in heads], axis=0)
            ksl = slice(hk * HEAD_DIM, (hk + 1) * HEAD_DIM)
            vsl = slice(A_KV_W + hk * HEAD_DIM, A_KV_W + (hk + 1) * HEAD_DIM)
            brow = slice(hk * A_GQA * DEC_SEQ, (hk + 1) * A_GQA * DEC_SEQ)
            s1 = jnp.dot(qs, cache[ksl, :], preferred_element_type=F32) + bmc_ref[brow, :]
            s2 = _nt_dot(qs, kvn[:, ksl]) + bmn_ref[brow, :]
            sink = jnp.concatenate(
                [jnp.full((DEC_SEQ, 1), sink_ref[h], F32) for h in heads], axis=0)
            (p1, p2), den, _ = _softmax_parts([s1, s2], sink)
            o = _nt_dot(p1, cache[vsl, :]) + jnp.dot(p2, kvn[:, vsl], preferred_element_type=F32)
            for j, h in enumerate(heads):
                outs[h] = o[j * DEC_SEQ:(j + 1) * DEC_SEQ, :]
        o_ref[rs, :] = jnp.concatenate(outs, axis=1)


def _swa_sample(q, kv, cache_all, layer, sinks, bmc, bmn):
    bb = SWA_SAMPLE_BB
    return pl.pallas_call(
        _swa_sample_kernel,
        grid=(DEC_BATCH // bb,),
        in_specs=[pl.BlockSpec(memory_space=pltpu.SMEM),
                  pl.BlockSpec((bb * DEC_SEQ, A_Q_W), lambda i: (i, 0)),
                  pl.BlockSpec((bb * DEC_SEQ, 2 * A_KV_W), lambda i: (i, 0)),
                  pl.BlockSpec((1, bb, 2 * A_KV_W, A_WINDOW), lambda i: (layer, i, 0, 0)),
                  pl.BlockSpec(bmc.shape, lambda i: (0, 0)),
                  pl.BlockSpec(bmn.shape, lambda i: (0, 0))],
        out_specs=pl.BlockSpec((bb * DEC_SEQ, A_Q_W), lambda i: (i, 0)),
        out_shape=jax.ShapeDtypeStruct((NS_TOK, A_Q_W), F32),
        compiler_params=_cparams("parallel"),
        name="swa_sample",
    )(sinks, q, kv, cache_all, bmc, bmn)


def _dil_sample_kernel(q_ref, k_ref, v_ref, c0_ref, c1_ref, c2_ref,
                       bc0_ref, bc1_ref, bc2_ref, bn_ref,
                       o0_ref, o1_ref, o2_ref, l0_ref, l1_ref, l2_ref):
    q = q_ref[...] * HEAD_DIM ** -0.5
    k = k_ref[...]
    v = v_ref[...]
    nrow = D_HEADS_PER * DEC_SEQ
    row_head = lax.broadcasted_iota(jnp.int32, (nrow, D_GROUP_W), 0) // DEC_SEQ
    lane_head = lax.broadcasted_iota(jnp.int32, (nrow, D_GROUP_W), 1) // HEAD_DIM
    head_mask = row_head == lane_head
    out_lane_head = lax.broadcasted_iota(jnp.int32, (DEC_SEQ, D_GROUP_W), 1) // HEAD_DIM
    groups = ((c0_ref, bc0_ref, o0_ref, l0_ref), (c1_ref, bc1_ref, o1_ref, l1_ref),
              (c2_ref, bc2_ref, o2_ref, l2_ref))
    for g, (c_ref, bc_ref, o_ref, l_ref) in enumerate(groups):
        gsl = slice(g * D_GROUP_W, (g + 1) * D_GROUP_W)
        qbd = jnp.where(head_mask, jnp.concatenate([q[:, gsl]] * D_HEADS_PER, axis=0), 0.0).astype(BF16)
        kn = _pad_rows(k[:, gsl], BLK).astype(BF16)
        vn = _pad_rows(v[:, gsl], BLK).astype(BF16)
        kt = c_ref[0, 0, 0:D_GROUP_W, :].astype(BF16)
        vt = c_ref[0, 0, D_GROUP_W:2 * D_GROUP_W, :].astype(BF16)
        s1 = jnp.dot(qbd, kt, preferred_element_type=F32) + bc_ref[0]
        s2 = _nt_dot(qbd, kn) + bn_ref[g]
        (p1, p2), den, m = _softmax_parts([s1, s2], None)
        of = _nt_dot(p1, vt) + jnp.dot(p2, vn, preferred_element_type=F32)
        lse = m + jnp.log(den)
        og = jnp.zeros((DEC_SEQ, D_GROUP_W), F32)
        lg = jnp.zeros((DEC_SEQ, D_GROUP_W), F32)
        for h in range(D_HEADS_PER):
            rs = slice(h * DEC_SEQ, (h + 1) * DEC_SEQ)
            sel = out_lane_head == h
            og = og + jnp.where(sel, of[rs, :], 0.0)
            lg = lg + jnp.where(sel, lse[rs, :], 0.0)
        o_ref[...] = og
        l_ref[...] = lg


def _dilated_sample(q, k, v, caches, layer, bcs, bn):
    tok_spec = pl.BlockSpec((DEC_SEQ, D_QKV_W), lambda b: (b, 0))
    out_spec = pl.BlockSpec((DEC_SEQ, D_GROUP_W), lambda b: (b, 0))
    outs = pl.pallas_call(
        _dil_sample_kernel,
        grid=(DEC_BATCH,),
        in_specs=[tok_spec, tok_spec, tok_spec]
        + [pl.BlockSpec((1, 1, 2 * D_GROUP_W, win), lambda b: (layer, b, 0, 0)) for win, _ in D_PAIRS]
        + [pl.BlockSpec(bc.shape, lambda b: (0, 0, 0)) for bc in bcs]
        + [pl.BlockSpec(bn.shape, lambda b: (0, 0, 0))],
        out_specs=[out_spec] * 6,
        out_shape=[jax.ShapeDtypeStruct((NS_TOK, D_GROUP_W), F32)] * 6,
        compiler_params=_cparams("parallel"),
        name="dilated_sample",
    )(q, k, v, *caches, *bcs, bn)
    return outs[:3], outs[3:]


def _shift_block(c_ref, n_ref, o_ref, i):
    width, wc = c_ref.shape[2], c_ref.shape[3]
    lane = lax.broadcasted_iota(jnp.int32, (width, BLK), 1)
    shifted = pltpu.roll(c_ref[0, i], wc - DEC_SEQ, 1)
    new_t = jnp.concatenate([jnp.zeros((BLK - DEC_SEQ, width), F32), n_ref[0, i]], axis=0).T
    if wc > BLK:
        o_ref[0, i, :, 0:wc - BLK] = shifted[:, 0:wc - BLK]
    o_ref[0, i, :, wc - BLK:wc] = jnp.where(lane >= BLK - DEC_SEQ, new_t, shifted[:, wc - BLK:wc])


def _shift_kernel(c_ref, n_ref, o_ref, *, bb):
    for i in range(bb):
        _shift_block(c_ref, n_ref, o_ref, i)


def _dil_sample_shift_kernel(q_ref, k_ref, v_ref, c0_ref, c1_ref, c2_ref, n0_ref, n1_ref, n2_ref,
                             bc0_ref, bc1_ref, bc2_ref, bn_ref,
                             o0_ref, o1_ref, o2_ref, l0_ref, l1_ref, l2_ref, s0_ref, s1_ref, s2_ref):
    for c_ref, n_ref, s_ref in ((c0_ref, n0_ref, s0_ref), (c1_ref, n1_ref, s1_ref), (c2_ref, n2_ref, s2_ref)):
        _shift_block(c_ref, n_ref, s_ref, 0)

    @pl.when(pl.program_id(0) == pl.num_programs(0) - 1)
    def _():
        _dil_sample_kernel(q_ref, k_ref, v_ref, c0_ref, c1_ref, c2_ref, bc0_ref, bc1_ref, bc2_ref, bn_ref,
                           o0_ref, o1_ref, o2_ref, l0_ref, l1_ref, l2_ref)


def _dilated_sample_and_shift(q, k, v, caches, new_rows, bcs, bn):
    n_layers = caches[0].shape[0]
    last = n_layers - 1
    tok_row = lambda l, b: jnp.where(l == last, b, 0)
    tok_spec = pl.BlockSpec((DEC_SEQ, D_QKV_W), lambda l, b: (tok_row(l, b), 0))
    out_spec = pl.BlockSpec((DEC_SEQ, D_GROUP_W), lambda l, b: (tok_row(l, b), 0))
    cache_specs = [pl.BlockSpec((1, 1, 2 * D_GROUP_W, win), lambda l, b: (l, b, 0, 0)) for win, _ in D_PAIRS]
    new_spec = pl.BlockSpec((1, 1, DEC_SEQ, 2 * D_GROUP_W), lambda l, b: (l, b, 0, 0))
    outs = pl.pallas_call(
        _dil_sample_shift_kernel,
        grid=(n_layers, DEC_BATCH),
        in_specs=[tok_spec, tok_spec, tok_spec] + cache_specs + [new_spec] * 3
        + [pl.BlockSpec(bc.shape, lambda l, b: (0, 0, 0)) for bc in bcs]
        + [pl.BlockSpec(bn.shape, lambda l, b: (0, 0, 0))],
        out_specs=[out_spec] * 6 + cache_specs,
        out_shape=[jax.ShapeDtypeStruct((NS_TOK, D_GROUP_W), F32)] * 6
        + [jax.ShapeDtypeStruct(c.shape, c.dtype) for c in caches],
        compiler_params=_cparams("arbitrary", "arbitrary"),
        name="dilated_sample_shift",
    )(q, k, v, *caches, *new_rows, *bcs, bn)
    return outs[:3], outs[3:6], outs[6:]


def _shift_cache(cache, new_rows, name):
    n_layers, _, width, wc = cache.shape
    bb = max(1, (4 * 2 ** 20) // (wc * width * 4))
    return pl.pallas_call(
        functools.partial(_shift_kernel, bb=bb),
        grid=(n_layers, DEC_BATCH // bb),
        in_specs=[pl.BlockSpec((1, bb, width, wc), lambda l, i: (l, i, 0, 0)),
                  pl.BlockSpec((1, bb, DEC_SEQ, width), lambda l, i: (l, i, 0, 0))],
        out_specs=pl.BlockSpec((1, bb, width, wc), lambda l, i: (l, i, 0, 0)),
        out_shape=jax.ShapeDtypeStruct(cache.shape, cache.dtype),
        compiler_params=_cparams("parallel", "parallel"),
        name=name,
    )(cache, new_rows)


def _to_time_major(src_ref, dst_ref, off, bt, lc):
    for b in range(bt):
        x = src_ref[b]
        for j in range(dst_ref.shape[0]):
            dst_ref[j, pl.ds(off + b, lc, stride=bt), :] = x[:, j * LANES:(j + 1) * LANES]


def _from_time_major(src_ref, dst_ref, bt, lc):
    for b in range(bt):
        dst_ref[b] = jnp.concatenate(
            [src_ref[j, pl.ds(b, lc, stride=bt), :] for j in range(src_ref.shape[0])], axis=1)


def _lane_blocks(ref, r0, nrows):
    return jnp.concatenate([ref[j, r0:r0 + nrows, :] for j in range(ref.shape[0])], axis=1)


def _store_lane_blocks(ref, r0, x):
    for j in range(ref.shape[0]):
        ref[j, r0:r0 + x.shape[0], :] = x[:, j * LANES:(j + 1) * LANES]


def _token_spec(bt, lc, width):
    return pl.BlockSpec((bt, lc, width), lambda c: (0, c, 0))


def _s5_kernel(u_ref, h0_ref, bm_ref, cm_ref, lam_ref, d_ref, gw_ref, gb_ref, o_ref, hl_ref,
               us_ref, hs_ref, *, bt, lc):
    rows = lc * bt
    blk_w = 2 * B_HALF

    @pl.when(pl.program_id(0) == 0)
    def _():
        hs_ref[0:bt, :] = h0_ref[...]

    _to_time_major(u_ref, us_ref, 0, bt, lc)
    u = _lane_blocks(us_ref, 0, rows)
    ub = u.astype(BF16)
    for j in range(B_LANE_BLOCKS):
        cols = jnp.dot(ub[:, j * LANES:(j + 1) * LANES], bm_ref[j], preferred_element_type=F32)
        hs_ref[bt:, j * blk_w:j * blk_w + B_HALF] = cols[:, :B_HALF]
        hs_ref[bt:, j * blk_w + B_HALF:(j + 1) * blk_w] = cols[:, :B_HALF] + cols[:, B_HALF:]

    def step(t, carry):
        r0 = pl.multiple_of(t * bt, bt)
        for j in range(B_LANE_BLOCKS):
            re = slice(j * blk_w, j * blk_w + B_HALF)
            im = slice(j * blk_w + B_HALF, (j + 1) * blk_w)
            lr = lam_ref[2 * j:2 * j + 1, :]
            li = lam_ref[2 * j + 1:2 * j + 2, :]
            pr = hs_ref[pl.ds(r0, bt), re]
            pi = hs_ref[pl.ds(r0, bt), im]
            hs_ref[pl.ds(r0 + bt, bt), re] = lr * pr - li * pi + hs_ref[pl.ds(r0 + bt, bt), re]
            hs_ref[pl.ds(r0 + bt, bt), im] = lr * pi + li * pr + hs_ref[pl.ds(r0 + bt, bt), im]
        return carry

    lax.fori_loop(0, lc, step, 0)

    ys = []
    for j in range(B_LANE_BLOCKS):
        h_re = hs_ref[bt:, j * blk_w:j * blk_w + B_HALF]
        h_im = hs_ref[bt:, j * blk_w + B_HALF:(j + 1) * blk_w]
        lhs = jnp.concatenate([h_re + h_im, h_im], axis=1).astype(BF16)
        ys.append(jnp.dot(lhs, cm_ref[j], preferred_element_type=F32))
    y = jax.nn.gelu(jnp.concatenate(ys, axis=1) + d_ref[...] * u)
    z = jnp.dot(y.astype(BF16), gw_ref[...], preferred_element_type=F32) + gb_ref[...]
    _store_lane_blocks(us_ref, 0, y * jax.nn.sigmoid(z))
    _from_time_major(us_ref, o_ref, bt, lc)
    last = hs_ref[rows:rows + bt, :]
    hl_ref[...] = last
    hs_ref[0:bt, :] = last


def _s5_mixer(u, h0, prm, bt, seq, lc, name):
    tok = _token_spec(bt, lc, B_WIDTH)
    u_in = u.reshape(bt, seq, B_WIDTH)
    full = lambda a: pl.BlockSpec(a.shape, lambda c: (0,) * a.ndim)
    out, hl = pl.pallas_call(
        functools.partial(_s5_kernel, bt=bt, lc=lc),
        grid=(seq // lc,),
        in_specs=[tok, full(h0), full(prm["bm"]), full(prm["cm"]), full(prm["lam"]), full(prm["d"]),
                  full(prm["glu_w"]), full(prm["glu_b"])],
        out_specs=[tok, pl.BlockSpec((bt, B_STATE_W), lambda c: (0, 0))],
        out_shape=[jax.ShapeDtypeStruct(u_in.shape, F32), jax.ShapeDtypeStruct((bt, B_STATE_W), F32)],
        scratch_shapes=[pltpu.VMEM((B_WIDTH // LANES, lc * bt, LANES), F32),
                        pltpu.VMEM((lc * bt + bt, B_STATE_W), F32)],
        compiler_params=_cparams("arbitrary"),
        name=name,
    )(u_in, h0, prm["bm"], prm["cm"], prm["lam"], prm["d"], prm["glu_w"], prm["glu_b"])
    return out.reshape(bt * seq, B_WIDTH), hl


def _s5_params(a_re, a_im, log_dt, b_re, b_im, c_re, c_im, d, glu_w, glu_b):
    lam = lax.complex(a_re.astype(F32), a_im.astype(F32))
    dt = jnp.exp(log_dt.astype(F32))[:, None]
    lam_bar = jnp.exp(lam * dt)
    b_bar = ((lam_bar - 1.0) / lam)[..., None] * lax.complex(b_re.astype(F32), b_im.astype(F32))
    nb, gb = B_LANE_BLOCKS, B_GROUPS_PER_BLOCK
    eye = jnp.eye(gb, dtype=F32)

    def in_mat(part):
        p = part.reshape(nb, gb, B_STATE, B_GROUP_CH)
        return jnp.einsum("jgnc,gh->jgchn", p, eye).reshape(nb, gb * B_GROUP_CH, gb * B_STATE)

    def out_mat(part):
        p = part.reshape(nb, gb, B_GROUP_CH, B_STATE)
        return jnp.einsum("jgcn,gh->jgnhc", p, eye).reshape(nb, gb * B_STATE, gb * B_GROUP_CH)

    bm = jnp.concatenate([in_mat(b_bar.real), in_mat(b_bar.imag - b_bar.real)], axis=2).astype(BF16)
    c_re32, c_im32 = c_re.astype(F32), c_im.astype(F32)
    cm = jnp.concatenate([out_mat(c_re32), out_mat(-(c_re32 + c_im32))], axis=1).astype(BF16)
    lam_rows = jnp.stack([lam_bar.real.reshape(nb, B_HALF), lam_bar.imag.reshape(nb, B_HALF)],
                         axis=1).reshape(2 * nb, B_HALF)
    return {"bm": bm, "cm": cm, "lam": lam_rows, "d": d.astype(F32).reshape(1, B_WIDTH),
            "glu_w": glu_w.astype(BF16), "glu_b": glu_b.astype(F32).reshape(1, B_WIDTH)}


def _s5_state_to_cols(state):
    bt = state.shape[0]
    s = state.astype(F32).reshape(bt, B_LANE_BLOCKS, B_GROUPS_PER_BLOCK, B_STATE, 2)
    return s.transpose(0, 1, 4, 2, 3).reshape(bt, B_STATE_W)


def _s5_cols_to_state(cols):
    bt = cols.shape[0]
    s = cols.reshape(bt, B_LANE_BLOCKS, 2, B_GROUPS_PER_BLOCK, B_STATE)
    return s.transpose(0, 1, 3, 4, 2).reshape(bt, B_GROUPS, B_STATE, 2)


def _rglru_kernel(xr_ref, gate_ref, cb_ref, h0_ref, cw_ref, cbias_ref, wa_ref, ba_ref, wx_ref, bx_ref,
                  nsp_ref, o_ref, hl_ref, xp_ref, gs_ref, a_ref, hs_ref, *, bt, lc):
    rows = lc * bt
    pad = (C_CONV - 1) * bt

    @pl.when(pl.program_id(0) == 0)
    def _():
        _store_lane_blocks(xp_ref, 0, cb_ref[...])
        hs_ref[0:bt, :] = h0_ref[...]

    _to_time_major(xr_ref, xp_ref, pad, bt, lc)
    _to_time_major(gate_ref, gs_ref, 0, bt, lc)
    xc = _lane_blocks(xp_ref, 0, rows) * cw_ref[0:1, :]
    for tap in range(1, C_CONV):
        xc = xc + _lane_blocks(xp_ref, tap * bt, rows) * cw_ref[tap:tap + 1, :]
    xcf = xc + cbias_ref[...]
    xb = xcf.astype(BF16)
    r = jax.nn.sigmoid(jnp.dot(xb, wa_ref[...], preferred_element_type=F32) + ba_ref[...])
    i = jax.nn.sigmoid(jnp.dot(xb, wx_ref[...], preferred_element_type=F32) + bx_ref[...])
    log_a = nsp_ref[...] * r
    a = jnp.exp(log_a)
    a_ref[...] = a
    hs_ref[bt:, :] = jnp.sqrt(-jnp.tanh(log_a) * (a * a + 1.0)) * (i * xcf)

    def step(t, carry):
        r0 = pl.multiple_of(t * bt, bt)
        hs_ref[pl.ds(r0 + bt, bt), :] = (a_ref[pl.ds(r0, bt), :] * hs_ref[pl.ds(r0, bt), :]
                                         + hs_ref[pl.ds(r0 + bt, bt), :])
        return carry

    lax.fori_loop(0, lc, step, 0)

    _store_lane_blocks(gs_ref, 0, hs_ref[bt:, :] * jax.nn.gelu(_lane_blocks(gs_ref, 0, rows)))
    _from_time_major(gs_ref, o_ref, bt, lc)
    last = hs_ref[rows:rows + bt, :]
    hl_ref[...] = last
    hs_ref[0:bt, :] = last
    _store_lane_blocks(xp_ref, 0, _lane_blocks(xp_ref, rows, pad))


def _rglru_mixer(xr, gate, conv_buf, h0, prm, bt, seq, lc, name):
    rows = lc * bt
    tok = _token_spec(bt, lc, C_WIDTH)
    shape3 = lambda x: x.reshape(bt, seq, C_WIDTH)
    full = lambda a: pl.BlockSpec(a.shape, lambda c: (0,) * a.ndim)
    names = ("conv_w", "conv_b", "wa", "ba", "wx", "bx", "nsp")
    out, hl = pl.pallas_call(
        functools.partial(_rglru_kernel, bt=bt, lc=lc),
        grid=(seq // lc,),
        in_specs=[tok, tok, full(conv_buf), full(h0)] + [full(prm[k]) for k in names],
        out_specs=[tok, pl.BlockSpec((bt, C_WIDTH), lambda c: (0, 0))],
        out_shape=[jax.ShapeDtypeStruct(shape3(xr).shape, F32), jax.ShapeDtypeStruct((bt, C_WIDTH), F32)],
        scratch_shapes=[pltpu.VMEM((C_WIDTH // LANES, rows + (C_CONV - 1) * bt, LANES), F32),
                        pltpu.VMEM((C_WIDTH // LANES, rows, LANES), F32),
                        pltpu.VMEM((rows, C_WIDTH), F32),
                        pltpu.VMEM((rows + bt, C_WIDTH), F32)],
        compiler_params=_cparams("arbitrary"),
        name=name,
    )(shape3(xr), shape3(gate), conv_buf, h0, *[prm[k] for k in names])
    return out.reshape(bt * seq, C_WIDTH), hl


def _rglru_params(conv_w, conv_b, gate_a_w, gate_a_b, gate_x_w, gate_x_b, lru_lambda):
    eye = jnp.eye(C_BLOCKS, dtype=F32)

    def block_diag(w):
        return jnp.einsum("njk,nm->njmk", w.astype(F32), eye).reshape(C_WIDTH, C_WIDTH).astype(BF16)

    row = lambda x: x.astype(F32).reshape(1, C_WIDTH)
    return {"conv_w": conv_w.astype(F32), "conv_b": row(conv_b),
            "wa": block_diag(gate_a_w), "ba": row(gate_a_b),
            "wx": block_diag(gate_x_w), "bx": row(gate_x_b),
            "nsp": row(-C_POWER * jax.nn.softplus(-lru_lambda.astype(F32)))}


def _out_proj_even_kernel(x_ref, oa_ref, ob_ref, w_ref, o_ref):
    acc = jnp.dot(oa_ref[...].astype(BF16), w_ref[0:A_Q_W, :], preferred_element_type=F32)
    acc = acc + jnp.dot(ob_ref[...].astype(BF16), w_ref[A_Q_W:, :], preferred_element_type=F32)
    o_ref[...] = x_ref[...] + acc


def _out_proj_odd_kernel(x_ref, oc_ref, o0_ref, o1_ref, o2_ref, l0_ref, l1_ref, l2_ref, w_ref, o_ref):
    l0, l1, l2 = l0_ref[...], l1_ref[...], l2_ref[...]
    m = jnp.maximum(jnp.maximum(l0, l1), l2)
    e0, e1, e2 = jnp.exp(l0 - m), jnp.exp(l1 - m), jnp.exp(l2 - m)
    od = (o0_ref[...] * e0 + o1_ref[...] * e1 + o2_ref[...] * e2) / (e0 + e1 + e2)
    acc = jnp.dot(oc_ref[...].astype(BF16), w_ref[0:C_WIDTH, :], preferred_element_type=F32)
    acc = acc + jnp.dot(od.astype(BF16), w_ref[C_WIDTH:, :], preferred_element_type=F32)
    o_ref[...] = x_ref[...] + acc


def _out_proj(kernel, x, parts, w, name, in_place=True):
    n = x.shape[0]
    tm = 512
    return pl.pallas_call(
        kernel,
        grid=(n // tm,),
        in_specs=[pl.BlockSpec((tm, D_MODEL), lambda i: (i, 0))]
        + [pl.BlockSpec((tm, p.shape[1]), lambda i: (i, 0)) for p in parts]
        + [pl.BlockSpec(w.shape, lambda i: (0, 0))],
        out_specs=pl.BlockSpec((tm, D_MODEL), lambda i: (i, 0)),
        out_shape=jax.ShapeDtypeStruct((n, D_MODEL), F32),
        input_output_aliases={0: 0} if in_place else {},
        compiler_params=_cparams("parallel"),
        name=name,
    )(x, *parts, w)


MOE_TM = 512


def _ffn_norm(x, g):
    y = x * lax.rsqrt(jnp.mean(x * x, axis=-1, keepdims=True) + RMS_EPS)
    return (y * g).astype(BF16)


def _route(xb, rw, rb):
    logits = jnp.dot(xb, rw, preferred_element_type=F32) + rb
    lane = lax.broadcasted_iota(jnp.int32, logits.shape, 1).astype(F32)
    ninf = float("-inf")
    far = float(ROUTER_LANES)
    lg = jnp.where(lane < MOE_GROUPS, logits, ninf)
    gmax = jnp.max(lg, axis=1, keepdims=True)
    g_idx = jnp.min(jnp.where(lg == gmax, lane, far), axis=1, keepdims=True)
    g_w = 1.0 / jnp.sum(jnp.exp(lg - gmax), axis=1, keepdims=True)
    lane_grp = jnp.floor((lane - MOE_GROUPS) * (1.0 / MOE_PER_GROUP))
    in_grp = (lane >= MOE_GROUPS) & (lane < MOE_GROUPS + MOE_EXPERTS) & (lane_grp == g_idx)
    le = jnp.where(in_grp, logits, ninf)
    v1 = jnp.max(le, axis=1, keepdims=True)
    i1 = jnp.min(jnp.where(le == v1, lane, far), axis=1, keepdims=True)
    le2 = jnp.where(lane == i1, ninf, le)
    v2 = jnp.max(le2, axis=1, keepdims=True)
    i2 = jnp.min(jnp.where(le2 == v2, lane, far), axis=1, keepdims=True)
    e2 = jnp.exp(v2 - v1)
    w1 = g_w / (1.0 + e2)
    w2 = g_w * e2 / (1.0 + e2)
    return lane, g_idx, i1 - MOE_GROUPS, i2 - MOE_GROUPS, w1, w2


def _moe_kernel(x_ref, g_ref, rw_ref, rb_ref, wg_ref, wu_ref, wd_ref, o_ref, xn_ref, gates_ref, acc_ref):
    e = pl.program_id(1)

    @pl.when(e == 0)
    def _():
        xb0 = _ffn_norm(x_ref[...], g_ref[...])
        xn_ref[...] = xb0
        lane, _, e1, e2, w1, w2 = _route(xb0, rw_ref[...], rb_ref[...])
        gates_ref[...] = jnp.where(lane == e1, w1, 0.0) + jnp.where(lane == e2, w2, 0.0)
        acc_ref[...] = jnp.zeros_like(acc_ref)

    xb = xn_ref[...]
    gates = gates_ref[...]
    lane_i = lax.broadcasted_iota(jnp.int32, gates.shape, 1)
    gate = jnp.sum(jnp.where(lane_i == e, gates, 0.0), axis=1, keepdims=True)
    hg = jnp.dot(xb, wg_ref[0], preferred_element_type=F32)
    hu = jnp.dot(xb, wu_ref[0], preferred_element_type=F32)
    hid = jax.nn.silu(hg) * hu * gate
    acc_ref[...] += jnp.dot(hid.astype(BF16), wd_ref[0], preferred_element_type=F32)

    @pl.when(e == MOE_EXPERTS - 1)
    def _():
        o_ref[...] = x_ref[...] + acc_ref[...]


def _moe(x, prm, name):
    n = x.shape[0]
    tm = MOE_TM
    return pl.pallas_call(
        _moe_kernel,
        grid=(n // tm, MOE_EXPERTS),
        in_specs=[pl.BlockSpec((tm, D_MODEL), lambda i, e: (i, 0)),
                  pl.BlockSpec((1, D_MODEL), lambda i, e: (0, 0)),
                  pl.BlockSpec((D_MODEL, ROUTER_LANES), lambda i, e: (0, 0)),
                  pl.BlockSpec((1, ROUTER_LANES), lambda i, e: (0, 0)),
                  pl.BlockSpec((1, D_MODEL, MOE_FF), lambda i, e: (e, 0, 0)),
                  pl.BlockSpec((1, D_MODEL, MOE_FF), lambda i, e: (e, 0, 0)),
                  pl.BlockSpec((1, MOE_FF, D_MODEL), lambda i, e: (e, 0, 0))],
        out_specs=pl.BlockSpec((tm, D_MODEL), lambda i, e: (i, 0)),
        out_shape=jax.ShapeDtypeStruct((n, D_MODEL), F32),
        scratch_shapes=[pltpu.VMEM((tm, D_MODEL), BF16),
                        pltpu.VMEM((tm, ROUTER_LANES), F32),
                        pltpu.VMEM((tm, D_MODEL), F32)],
        input_output_aliases={0: 0},
        compiler_params=_cparams("parallel", "arbitrary"),
        name=name,
    )(x, prm["g"], prm["rw"], prm["rb"], prm["wg"], prm["wu"], prm["wd"])


def _moe_params(norm_g, router_g, router_g_b, router_e, router_e_b, w_gate, w_up, w_down):
    used = MOE_GROUPS + MOE_EXPERTS
    rw = jnp.concatenate([router_g.astype(F32), router_e.astype(F32),
                          jnp.zeros((D_MODEL, ROUTER_LANES - used), F32)], axis=1)
    rb = jnp.concatenate([router_g_b.astype(F32), router_e_b.astype(F32),
                          jnp.zeros((ROUTER_LANES - used,), F32)]).reshape(1, ROUTER_LANES)
    return {"g": norm_g.astype(F32).reshape(1, D_MODEL), "rw": rw.astype(BF16), "rb": rb,
            "wg": w_gate.astype(BF16), "wu": w_up.astype(BF16), "wd": w_down.astype(BF16)}


MOE_AUG_W = D_MODEL + ROUTER_LANES
GATE_LANE0 = 1


def _moe_route_kernel(x_ref, g_ref, rw_ref, rb_ref, o_ref):
    x = x_ref[...]
    lane, g_idx, e1, e2, w1, w2 = _route(_ffn_norm(x, g_ref[...]), rw_ref[...], rb_ref[...])
    l1 = e1 - g_idx * MOE_PER_GROUP + GATE_LANE0
    l2 = e2 - g_idx * MOE_PER_GROUP + GATE_LANE0
    o_ref[:, 0:D_MODEL] = x
    o_ref[:, D_MODEL:] = (jnp.where(lane == 0.0, g_idx, 0.0)
                          + jnp.where(lane == l1, w1, 0.0) + jnp.where(lane == l2, w2, 0.0))


def _moe_route(x, prm, name):
    n = x.shape[0]
    tm = MOE_TM
    return pl.pallas_call(
        _moe_route_kernel,
        grid=(n // tm,),
        in_specs=[pl.BlockSpec((tm, D_MODEL), lambda i: (i, 0)),
                  pl.BlockSpec((1, D_MODEL), lambda i: (0, 0)),
                  pl.BlockSpec((D_MODEL, ROUTER_LANES), lambda i: (0, 0)),
                  pl.BlockSpec((1, ROUTER_LANES), lambda i: (0, 0))],
        out_specs=pl.BlockSpec((tm, MOE_AUG_W), lambda i: (i, 0)),
        out_shape=jax.ShapeDtypeStruct((n, MOE_AUG_W), F32),
        compiler_params=_cparams("parallel"),
        name=name,
    )(x, prm["g"], prm["rw"], prm["rb"])


def _moe_plan(group_col, n):
    tm = MOE_TM
    n_tiles_max = n // tm + MOE_GROUPS
    g = group_col.astype(jnp.int32)
    onehot = (g[:, None] == jnp.arange(MOE_GROUPS)[None, :]).astype(jnp.int32)
    counts = jnp.sum(onehot, axis=0)
    rank = jnp.sum((jnp.cumsum(onehot, axis=0) - onehot) * onehot, axis=1)
    padded = (counts + tm - 1) // tm * tm
    ends = jnp.cumsum(padded)
    starts = ends - padded
    pos = jnp.sum(onehot * starts[None, :], axis=1) + rank
    src = jnp.zeros((n_tiles_max * tm,), jnp.int32).at[pos].set(jnp.arange(n, dtype=jnp.int32))
    tile0 = jnp.arange(n_tiles_max, dtype=jnp.int32) * tm
    tile_group = jnp.minimum(jnp.sum((tile0[:, None] >= ends[None, :]).astype(jnp.int32), axis=1),
                             MOE_GROUPS - 1)
    real = jnp.clip(counts[tile_group] - (tile0 - starts[tile_group]), 0, tm)
    real = jnp.where(tile0 < ends[-1], real, 0).astype(jnp.int32)
    return src, tile_group.astype(jnp.int32), real, (ends[-1:] // tm).astype(jnp.int32)


def _moe_expert_kernel(src_ref, grp_ref, real_ref, nt_ref, xa_hbm, g_ref, wg_ref, wu_ref, wd_ref, out_hbm,
                       xbuf, ybuf, gsem, ssem):
    del grp_ref
    tm = MOE_TM
    i = pl.program_id(0)
    n_tiles = nt_ref[0]
    slot = lax.rem(i, 2)
    sub = SUBLANES

    def row_in(tile, s, j, u):
        row = src_ref[tile * tm + j * sub + u]
        return pltpu.make_async_copy(xa_hbm.at[row >> 3, pl.ds(row & 7, 1)],
                                     xbuf.at[s, j, pl.ds(u, 1)], gsem.at[s])

    def row_out(tile, s, j, u):
        row = src_ref[tile * tm + j * sub + u]
        return pltpu.make_async_copy(ybuf.at[s, j, pl.ds(u, 1)],
                                     out_hbm.at[row >> 3, pl.ds(row & 7, 1)], ssem.at[s])

    def start_gather(tile, s):
        def body(j, c):
            for u in range(sub):
                row_in(tile, s, j, u).start()
            return c
        lax.fori_loop(0, tm // sub, body, 0)

    def wait_gather(s):
        pltpu.make_async_copy(xa_hbm.at[pl.ds(0, tm // sub)], xbuf.at[s], gsem.at[s]).wait()

    def start_scatter(tile, s):
        real = real_ref[tile]
        full = real // sub

        def body(j, c):
            for u in range(sub):
                row_out(tile, s, j, u).start()
            return c
        lax.fori_loop(0, full, body, 0)

        def tail(r, c):
            row_out(tile, s, full, r - full * sub).start()
            return c
        lax.fori_loop(full * sub, real, tail, 0)

    def wait_scatter(tile, s):
        real = real_ref[tile]
        for k in range(tm.bit_length()):
            size = 1 << k

            @pl.when((real >> k) & 1 == 1)
            def _():
                if size >= sub:
                    piece = (ybuf.at[s, pl.ds(0, size // sub)], out_hbm.at[pl.ds(0, size // sub)])
                else:
                    piece = (ybuf.at[s, 0, pl.ds(0, size)], out_hbm.at[0, pl.ds(0, size)])
                pltpu.make_async_copy(piece[0], piece[1], ssem.at[s]).wait()

    @pl.when(i < n_tiles)
    def _():
        @pl.when(i == 0)
        def _():
            start_gather(0, 0)

        @pl.when(i + 1 < n_tiles)
        def _():
            start_gather(i + 1, 1 - slot)

        wait_gather(slot)

        @pl.when(i >= 2)
        def _():
            wait_scatter(i - 2, slot)

        xa = xbuf[slot].reshape(tm, MOE_AUG_W)
        x = xa[:, 0:D_MODEL]
        route = xa[:, D_MODEL:]
        xb = _ffn_norm(x, g_ref[...])
        acc = jnp.zeros((tm, D_MODEL), F32)
        for e in range(MOE_PER_GROUP):
            gate = route[:, GATE_LANE0 + e:GATE_LANE0 + e + 1]
            hg = jnp.dot(xb, wg_ref[0, e], preferred_element_type=F32)
            hu = jnp.dot(xb, wu_ref[0, e], preferred_element_type=F32)
            hid = jax.nn.silu(hg) * hu * gate
            acc = acc + jnp.dot(hid.astype(BF16), wd_ref[0, e], preferred_element_type=F32)
        ybuf[slot] = (x + acc).reshape(tm // sub, sub, D_MODEL)
        start_scatter(i, slot)

        @pl.when(i == n_tiles - 1)
        def _():
            wait_scatter(i, slot)

            @pl.when(i >= 1)
            def _():
                wait_scatter(i - 1, 1 - slot)


def _moe_experts(xa, plan, prm, name):
    n = xa.shape[0]
    tm = MOE_TM
    sub = SUBLANES
    src, tile_group, real, n_tiles = plan
    grouped = lambda w: w.reshape(MOE_GROUPS, MOE_PER_GROUP, w.shape[1], w.shape[2])
    w_spec = lambda a, b: pl.BlockSpec((1, MOE_PER_GROUP, a, b), lambda i, src, grp, real, nt: (grp[i], 0, 0, 0))
    grid_spec = pltpu.PrefetchScalarGridSpec(
        num_scalar_prefetch=4,
        grid=(n // tm + MOE_GROUPS,),
        in_specs=[pl.BlockSpec(memory_space=pl.ANY),
                  pl.BlockSpec((1, D_MODEL), lambda i, *_: (0, 0)),
                  w_spec(D_MODEL, MOE_FF), w_spec(D_MODEL, MOE_FF), w_spec(MOE_FF, D_MODEL)],
        out_specs=pl.BlockSpec(memory_space=pl.ANY),
        scratch_shapes=[pltpu.VMEM((2, tm // sub, sub, MOE_AUG_W), F32),
                        pltpu.VMEM((2, tm // sub, sub, D_MODEL), F32),
                        pltpu.SemaphoreType.DMA((2,)),
                        pltpu.SemaphoreType.DMA((2,))])
    out = pl.pallas_call(
        _moe_expert_kernel,
        grid_spec=grid_spec,
        out_shape=jax.ShapeDtypeStruct((n // sub, sub, D_MODEL), F32),
        compiler_params=_cparams("arbitrary"),
        name=name,
    )(src, tile_group, real, n_tiles, xa.reshape(n // sub, sub, MOE_AUG_W), prm["g"],
      grouped(prm["wg"]), grouped(prm["wu"]), grouped(prm["wd"]))
    return out.reshape(n, D_MODEL)


def _final_norm_kernel(x_ref, g_ref, o_ref):
    x = x_ref[...]
    y = x * lax.rsqrt(jnp.mean(x * x, axis=-1, keepdims=True) + RMS_EPS)
    o_ref[...] = y * g_ref[...]


def _final_norm(x, g, name):
    n = x.shape[0]
    tm = 512
    return pl.pallas_call(
        _final_norm_kernel,
        grid=(n // tm,),
        in_specs=[pl.BlockSpec((tm, D_MODEL), lambda i: (i, 0)),
                  pl.BlockSpec((1, D_MODEL), lambda i: (0, 0))],
        out_specs=pl.BlockSpec((tm, D_MODEL), lambda i: (i, 0)),
        out_shape=jax.ShapeDtypeStruct((n, D_MODEL), F32),
        compiler_params=_cparams("parallel"),
        name=name,
    )(x, g.astype(F32).reshape(1, D_MODEL))


def _t5_bucket(dist):
    n = jnp.maximum(dist, 0)
    max_exact = T5_BUCKETS // 2
    nf = jnp.maximum(n, 1).astype(F32)
    large = max_exact + (jnp.log(nf / max_exact) / math.log(T5_MAX_DIST / max_exact)
                         * (T5_BUCKETS - max_exact)).astype(jnp.int32)
    return jnp.where(n < max_exact, n, jnp.minimum(large, T5_BUCKETS - 1))


def _rel_bias(table, dist):
    hit = _t5_bucket(dist)[..., None] == jnp.arange(T5_BUCKETS)
    rows = jnp.sum(jnp.where(hit[..., None], table.astype(F32), 0.0), axis=-2)
    return jnp.moveaxis(rows, -1, 0)


def _band_bias_mask(table, max_dist, dil):
    r = jnp.arange(BLK)[:, None]
    s = jnp.arange(2 * BLK)[None, :]
    dist = BLK + r - s
    valid = (dist >= 0) & (dist <= max_dist)
    bias = _rel_bias(table, dist * dil)
    later = jnp.where(valid[None], bias, NEG)
    first = jnp.where((valid & (s >= BLK))[None], bias, NEG)
    return jnp.stack([first, later])


def _swa_sample_bias_mask(table):
    s = jnp.arange(DEC_SEQ)[:, None]
    col = jnp.arange(BLK)[None, :]
    dist_c = A_WINDOW + s - col
    dist_n = s - col
    tab = table[:, :A_Q_HEADS]
    bc = jnp.where(((dist_c >= 0) & (dist_c < A_WINDOW))[None], _rel_bias(tab, dist_c), NEG)
    bn = jnp.where(((dist_n >= 0) & (col < DEC_SEQ))[None], _rel_bias(tab, dist_n), NEG)
    return bc.reshape(A_Q_HEADS * DEC_SEQ, A_WINDOW), bn.reshape(A_Q_HEADS * DEC_SEQ, BLK)


def _dil_sample_bias_mask(table):
    s = jnp.arange(DEC_SEQ)[:, None]
    col = jnp.arange(BLK)[None, :]
    bcs, bns = [], []
    for g, (win, dil) in enumerate(D_PAIRS):
        lo = A_Q_HEADS + g * D_HEADS_PER
        tab = table[:, lo:lo + D_HEADS_PER]
        dist_c = win + s - jnp.arange(win)[None, :]
        valid_c = (dist_c >= 0) & (dist_c % dil == 0) & (dist_c <= win)
        bc = jnp.where(valid_c[None], _rel_bias(tab, dist_c), NEG)
        dist_n = s - col
        valid_n = (dist_n >= 0) & (dist_n % dil == 0) & (col < DEC_SEQ)
        bn = jnp.where(valid_n[None], _rel_bias(tab, dist_n), NEG)
        bcs.append(bc.reshape(1, D_HEADS_PER * DEC_SEQ, win))
        bns.append(bn.reshape(D_HEADS_PER * DEC_SEQ, BLK))
    return bcs, jnp.stack(bns)


def _native_cache(c):
    n_layers, nb, wc = c.shape[:3]
    return c.transpose(0, 1, 3, 4, 5, 2).reshape(n_layers, nb, -1, wc)


def _logical_cache(c, heads):
    n_layers, nb, _, wc = c.shape
    return c.reshape(n_layers, nb, 2, heads, HEAD_DIM, wc).transpose(0, 1, 5, 2, 3, 4)


def kernel(x_prompt, x_sample, cache_a_kv, state_b, state_c_h, state_c_conv, cache_d_g0, cache_d_g1, cache_d_g2, rel_table, norm_mix, norm_ffn, norm_final, w_in_even, w_out_even, sinks_a, s5_a_re, s5_a_im, s5_log_dt, s5_b_re, s5_b_im, s5_c_re, s5_c_im, s5_d, s5_glu_w, s5_glu_b, w_in_odd, w_out_odd, conv_w, conv_b, gate_a_w, gate_a_b, gate_x_w, gate_x_b, lru_lambda, moe_router_g, moe_router_g_b, moe_router_e, moe_router_e_b, moe_w_gate, moe_w_up, moe_w_down):
    xp = x_prompt.astype(F32).reshape(NP_TOK, D_MODEL)
    xs = x_sample.astype(F32).reshape(NS_TOK, D_MODEL)

    cache_a = _native_cache(cache_a_kv)
    caches_d = [_native_cache(c) for c in (cache_d_g0, cache_d_g1, cache_d_g2)]

    bm_a = _band_bias_mask(rel_table[:, :A_Q_HEADS], A_WINDOW - 1, 1)
    bm_d = [_band_bias_mask(rel_table[:, A_Q_HEADS + g * D_HEADS_PER:A_Q_HEADS + (g + 1) * D_HEADS_PER],
                            win // dil, dil) for g, (win, dil) in enumerate(D_PAIRS)]
    bmc_a_s, bmn_a_s = _swa_sample_bias_mask(rel_table)
    bcs_d_s, bn_d_s = _dil_sample_bias_mask(rel_table)

    a_p, a_new, b_p, b_s = [], [], [], []
    ch_p, ch_s, cc_p, cc_s = [], [], [], []
    d_p = [[], [], []]
    d_new = [[], [], []]

    for layer in range(DEPTH):
        if layer % 2 == 0:
            e = layer // 2
            w_in = w_in_even[e].astype(BF16)
            w_out = w_out_even[e].astype(BF16)
            splits = (A_Q_W, 2 * A_KV_W, B_WIDTH)
            s5p = _s5_params(s5_a_re[e], s5_a_im[e], s5_log_dt[e], s5_b_re[e], s5_b_im[e],
                             s5_c_re[e], s5_c_im[e], s5_d[e], s5_glu_w[e], s5_glu_b[e])
            sinks = sinks_a[e].astype(F32)

            q_p, kv_p, u_p = _norm_proj(xp, norm_mix[layer], w_in, splits, f"in_proj_p{layer}")
            q_s, kv_s, u_s = _norm_proj(xs, norm_mix[layer], w_in, splits, f"in_proj_s{layer}")

            oa_p = _swa_prompt(q_p, kv_p, sinks, bm_a)
            oa_s = _swa_sample(q_s, kv_s, cache_a, e, sinks, bmc_a_s, bmn_a_s)

            ob_p, hl_p = _s5_mixer(u_p, jnp.zeros((BATCH, B_STATE_W), F32), s5p, BATCH, SEQ, 64, "s5_prompt")
            ob_s, hl_s = _s5_mixer(u_s, _s5_state_to_cols(state_b[e]), s5p, DEC_BATCH, DEC_SEQ, DEC_SEQ,
                                   "s5_sample")

            xp = _out_proj(_out_proj_even_kernel, xp, [oa_p, ob_p], w_out, f"out_proj_p{layer}",
                           in_place=layer > 0)
            xs = _out_proj(_out_proj_even_kernel, xs, [oa_s, ob_s], w_out, f"out_proj_s{layer}",
                           in_place=layer > 0)

            a_p.append(kv_p.reshape(BATCH, SEQ, 2, A_KV_HEADS, HEAD_DIM)[:, SEQ - A_WINDOW:])
            a_new.append(kv_s.reshape(DEC_BATCH, DEC_SEQ, 2 * A_KV_W))
            b_p.append(_s5_cols_to_state(hl_p))
            b_s.append(_s5_cols_to_state(hl_s))
        else:
            o = layer // 2
            w_in = w_in_odd[o].astype(BF16)
            w_out = w_out_odd[o].astype(BF16)
            splits = (C_WIDTH, C_WIDTH, D_QKV_W, D_QKV_W, D_QKV_W)
            lrp = _rglru_params(conv_w[o], conv_b[o], gate_a_w[o], gate_a_b[o], gate_x_w[o], gate_x_b[o],
                                lru_lambda[o])

            xr_p, gate_p, q_p, k_p, v_p = _norm_proj(xp, norm_mix[layer], w_in, splits, f"in_proj_p{layer}")
            xr_s, gate_s, q_s, k_s, v_s = _norm_proj(xs, norm_mix[layer], w_in, splits, f"in_proj_s{layer}")

            od_p, lse_p = [], []
            for g, (win, dil) in enumerate(D_PAIRS):
                og, lg = _dilated_prompt_group(q_p, k_p, v_p, bm_d[g], g, dil)
                od_p.append(og)
                lse_p.append(lg)
            k_s4 = k_s.reshape(DEC_BATCH, DEC_SEQ, D_N_GROUPS, D_GROUP_W)
            v_s4 = v_s.reshape(DEC_BATCH, DEC_SEQ, D_N_GROUPS, D_GROUP_W)
            for g in range(D_N_GROUPS):
                d_new[g].append(jnp.concatenate([k_s4[:, :, g], v_s4[:, :, g]], axis=-1))
            if o < caches_d[0].shape[0] - 1:
                od_s, lse_s = _dilated_sample(q_s, k_s, v_s, caches_d, o, bcs_d_s, bn_d_s)
            else:
                od_s, lse_s, shifted_d = _dilated_sample_and_shift(
                    q_s, k_s, v_s, caches_d, [jnp.stack(rows) for rows in d_new], bcs_d_s, bn_d_s)

            oc_p, hc_p = _rglru_mixer(xr_p, gate_p, jnp.zeros(((C_CONV - 1) * BATCH, C_WIDTH), F32),
                                      jnp.zeros((BATCH, C_WIDTH), F32), lrp, BATCH, SEQ, 128, "rglru_prompt")
            conv_s = state_c_conv[o].astype(F32).transpose(1, 0, 2).reshape((C_CONV - 1) * DEC_BATCH, C_WIDTH)
            oc_s, hc_s = _rglru_mixer(xr_s, gate_s, conv_s, state_c_h[o].astype(F32), lrp,
                                      DEC_BATCH, DEC_SEQ, DEC_SEQ, "rglru_sample")

            xp = _out_proj(_out_proj_odd_kernel, xp, [oc_p] + od_p + lse_p, w_out, f"out_proj_p{layer}")
            xs = _out_proj(_out_proj_odd_kernel, xs, [oc_s] + list(od_s) + list(lse_s), w_out,
                           f"out_proj_s{layer}")

            ch_p.append(hc_p)
            ch_s.append(hc_s)
            cc_p.append(xr_p.reshape(BATCH, SEQ, C_WIDTH)[:, SEQ - (C_CONV - 1):])
            cc_s.append(xr_s.reshape(DEC_BATCH, DEC_SEQ, C_WIDTH)[:, DEC_SEQ - (C_CONV - 1):])
            k_p5 = k_p.reshape(BATCH, SEQ, D_N_GROUPS, D_HEADS_PER, HEAD_DIM)
            v_p5 = v_p.reshape(BATCH, SEQ, D_N_GROUPS, D_HEADS_PER, HEAD_DIM)
            for g, (win, dil) in enumerate(D_PAIRS):
                wc = min(win, SEQ)
                d_p[g].append(jnp.stack([k_p5[:, SEQ - wc:, g], v_p5[:, SEQ - wc:, g]], axis=2))

        mp = _moe_params(norm_ffn[layer], moe_router_g[layer], moe_router_g_b[layer], moe_router_e[layer],
                         moe_router_e_b[layer], moe_w_gate[layer], moe_w_up[layer], moe_w_down[layer])
        xa = _moe_route(xp, mp, f"moe_route_p{layer}")
        xp = _moe_experts(xa, _moe_plan(xa[:, D_MODEL], NP_TOK), mp, f"moe_experts_p{layer}")
        xs = _moe(xs, mp, f"moe_s{layer}")

    y_prompt = _final_norm(xp, norm_final, "final_norm_p").reshape(BATCH, SEQ, D_MODEL)
    y_sample = _final_norm(xs, norm_final, "final_norm_s").reshape(DEC_BATCH, DEC_SEQ, D_MODEL)

    new_a = _logical_cache(_shift_cache(cache_a, jnp.stack(a_new), "shift_cache_a"), A_KV_HEADS)
    new_d = [_logical_cache(c, D_HEADS_PER) for c in shifted_d]

    return (y_prompt, y_sample,
            jnp.stack(a_p), new_a, jnp.stack(b_p), jnp.stack(b_s),
            jnp.stack(ch_p), jnp.stack(ch_s), jnp.stack(cc_p), jnp.stack(cc_s),
            jnp.stack(d_p[0]), new_d[0], jnp.stack(d_p[1]), new_d[1], jnp.stack(d_p[2]), new_d[2])
```

```python
import functools
import math

import jax
import jax.numpy as jnp
from jax import lax
from jax.experimental import pallas as pl
from jax.experimental.pallas import tpu as pltpu

F32 = jnp.float32
BF16 = jnp.bfloat16

D_MODEL = 1024
BATCH = 8
SEQ = 2048
DEPTH = 4
DEC_BATCH = 128
DEC_SEQ = 8
HEAD_DIM = 64
BLK = 128
RMS_EPS = 1e-6
NEG = -1e30

A_Q_HEADS = 8
A_KV_HEADS = 2
A_GQA = 4
A_WINDOW = 128
A_Q_W = A_Q_HEADS * HEAD_DIM
A_KV_W = A_KV_HEADS * HEAD_DIM

B_WIDTH = 512
B_GROUP_CH = 16
B_GROUPS = 32
B_STATE = 64
B_LANE_BLOCKS = 4
B_GROUPS_PER_BLOCK = B_GROUPS // B_LANE_BLOCKS
B_HALF = B_GROUPS_PER_BLOCK * B_STATE
B_STATE_W = B_LANE_BLOCKS * 2 * B_HALF

C_WIDTH = 512
C_BLOCKS = 8
C_BLOCK_W = 64
C_CONV = 4
C_POWER = 8.0

D_PAIRS = ((128, 1), (512, 4), (2048, 16))
D_N_GROUPS = 3
D_HEADS_PER = 4
D_GROUP_W = D_HEADS_PER * HEAD_DIM
D_QKV_W = D_N_GROUPS * D_GROUP_W

T5_BUCKETS = 32
T5_MAX_DIST = 2048

MOE_GROUPS = 4
MOE_PER_GROUP = 4
MOE_EXPERTS = 16
MOE_FF = 256
ROUTER_LANES = 128

NP_TOK = BATCH * SEQ
NS_TOK = DEC_BATCH * DEC_SEQ

VMEM_LIMIT_BYTES = 52 * 2 ** 20


def _cparams(*sem):
    return pltpu.CompilerParams(dimension_semantics=sem, vmem_limit_bytes=VMEM_LIMIT_BYTES)


def _nt_dot(a, b):
    return lax.dot_general(a, b, (((1,), (1,)), ((), ())), preferred_element_type=F32)


LANES = 128
SUBLANES = 8


def _norm_proj_kernel(x_ref, g_ref, w_ref, *out_refs, splits):
    x = x_ref[...]
    y = x * lax.rsqrt(jnp.mean(x * x, axis=-1, keepdims=True) + RMS_EPS)
    xn = (y * g_ref[...]).astype(BF16)
    off = 0
    for o_ref, width in zip(out_refs, splits):
        o_ref[...] = jnp.dot(xn, w_ref[:, off:off + width], preferred_element_type=F32)
        off += width


def _norm_proj(x, g, w, splits, name):
    n = x.shape[0]
    tm = 512
    return pl.pallas_call(
        functools.partial(_norm_proj_kernel, splits=splits),
        grid=(n // tm,),
        in_specs=[pl.BlockSpec((tm, D_MODEL), lambda i: (i, 0)),
                  pl.BlockSpec((1, D_MODEL), lambda i: (0, 0)),
                  pl.BlockSpec(w.shape, lambda i: (0, 0))],
        out_specs=[pl.BlockSpec((tm, s), lambda i: (i, 0)) for s in splits],
        out_shape=[jax.ShapeDtypeStruct((n, s), F32) for s in splits],
        compiler_params=_cparams("parallel"),
        name=name,
    )(x, g.reshape(1, D_MODEL), w)


def _softmax_parts(scores, sink):
    m = jnp.max(scores[0], axis=1, keepdims=True)
    for s in scores[1:]:
        m = jnp.maximum(m, jnp.max(s, axis=1, keepdims=True))
    if sink is not None:
        m = jnp.maximum(m, sink)
    ps = [jnp.exp(s - m) for s in scores]
    den = jnp.sum(ps[0], axis=1, keepdims=True)
    for p in ps[1:]:
        den = den + jnp.sum(p, axis=1, keepdims=True)
    if sink is not None:
        den = den + jnp.exp(sink - m)
    inv = 1.0 / den
    return [(p * inv).astype(BF16) for p in ps], den, m


def _band_attn_kernel(*refs, n_heads, gqa, with_sink, dil, has_prev):
    refs = list(refs)
    sink_ref = refs.pop(0) if with_sink else None
    q_ref, kc_ref = refs.pop(0), refs.pop(0)
    kp_ref = refs.pop(0) if has_prev else None
    vc_ref = refs.pop(0)
    vp_ref = refs.pop(0) if has_prev else None
    bm_ref, o_ref = refs.pop(0), refs.pop(0)
    lse_ref = refs.pop(0) if refs else None
    n_kv = n_heads // gqa
    row_sel = [slice(None) if dil == 1 else pl.ds(r, BLK, stride=dil) for r in range(dil)]
    sc, sp, v_cur, v_prev = [], [], [], []
    for rows in row_sel:
        q = (q_ref[rows, :] * HEAD_DIM ** -0.5).astype(BF16)
        kc = kc_ref[rows, :].astype(BF16)
        vc = vc_ref[rows, :].astype(BF16)
        if has_prev:
            kp = kp_ref[rows, :].astype(BF16)
            vp = vp_ref[rows, :].astype(BF16)
        for hk in range(n_kv):
            ksl = slice(hk * HEAD_DIM, (hk + 1) * HEAD_DIM)
            qs = jnp.concatenate([q[:, h * HEAD_DIM:(h + 1) * HEAD_DIM]
                                  for h in range(hk * gqa, (hk + 1) * gqa)], axis=0)
            sc.append(_nt_dot(qs, kc[:, ksl]))
            v_cur.append(vc[:, ksl])
            if has_prev:
                sp.append(_nt_dot(qs, kp[:, ksl]))
                v_prev.append(vp[:, ksl])
    bias_c = jnp.concatenate([bm_ref[0, h, :, BLK:2 * BLK] for h in range(n_heads)] * dil, axis=0)
    scores = [jnp.concatenate(sc, axis=0) + bias_c]
    if has_prev:
        bias_p = jnp.concatenate([bm_ref[0, h, :, 0:BLK] for h in range(n_heads)] * dil, axis=0)
        scores.append(jnp.concatenate(sp, axis=0) + bias_p)
    sink = None
    if with_sink:
        sink = jnp.concatenate([jnp.full((BLK, 1), sink_ref[h], F32) for h in range(n_heads)] * dil, axis=0)
    ps, den, m = _softmax_parts(scores, sink)
    lse = m + jnp.log(den) if lse_ref is not None else None
    unit_rows = gqa * BLK
    for r, rows in enumerate(row_sel):
        outs = []
        for hk in range(n_kv):
            u = r * n_kv + hk
            usl = slice(u * unit_rows, (u + 1) * unit_rows)
            o = jnp.dot(ps[0][usl, :], v_cur[u], preferred_element_type=F32)
            if has_prev:
                o = o + jnp.dot(ps[1][usl, :], v_prev[u], preferred_element_type=F32)
            outs += [o[j * BLK:(j + 1) * BLK, :] for j in range(gqa)]
        o_ref[rows, :] = jnp.concatenate(outs, axis=1)
        if lse_ref is not None:
            base = r * n_heads * BLK
            lse_ref[rows, :] = jnp.concatenate(
                [jnp.broadcast_to(lse[base + h * BLK:base + (h + 1) * BLK, :], (BLK, HEAD_DIM))
                 for h in range(n_heads)], axis=1)


def _swa_prompt(q, kv, sinks, bm):
    nblk = SEQ // BLK
    row = lambda b, i: b * nblk + i
    prev = lambda b, i: b * nblk + jnp.maximum(i - 1, 0)
    return pl.pallas_call(
        functools.partial(_band_attn_kernel, n_heads=A_Q_HEADS, gqa=A_GQA, with_sink=True, dil=1,
                          has_prev=True),
        grid=(BATCH, nblk),
        in_specs=[pl.BlockSpec(memory_space=pltpu.SMEM),
                  pl.BlockSpec((BLK, A_Q_W), lambda b, i: (row(b, i), 0)),
                  pl.BlockSpec((BLK, A_KV_W), lambda b, i: (row(b, i), 0)),
                  pl.BlockSpec((BLK, A_KV_W), lambda b, i: (prev(b, i), 0)),
                  pl.BlockSpec((BLK, A_KV_W), lambda b, i: (row(b, i), 1)),
                  pl.BlockSpec((BLK, A_KV_W), lambda b, i: (prev(b, i), 1)),
                  pl.BlockSpec((1, A_Q_HEADS, BLK, 2 * BLK), lambda b, i: (jnp.minimum(i, 1), 0, 0, 0))],
        out_specs=pl.BlockSpec((BLK, A_Q_W), lambda b, i: (row(b, i), 0)),
        out_shape=jax.ShapeDtypeStruct((NP_TOK, A_Q_W), F32),
        compiler_params=_cparams("parallel", "parallel"),
        name="swa_prompt",
    )(sinks, q, kv, kv, kv, kv, bm)


def _dilated_prompt_group(q, k, v, bm, g, dil):
    rows = BLK * dil
    nchunk = SEQ // rows
    has_prev = nchunk > 1
    pair = LANES // HEAD_DIM if dil > 1 else D_HEADS_PER
    npair = D_HEADS_PER // pair
    width = pair * HEAD_DIM
    row = lambda b, i: b * nchunk + i
    prev = lambda b, i: b * nchunk + jnp.maximum(i - 1, 0)
    cur_spec = pl.BlockSpec((rows, width), lambda b, i, p: (row(b, i), g * npair + p))
    prev_spec = pl.BlockSpec((rows, width), lambda b, i, p: (prev(b, i), g * npair + p))
    out_spec = pl.BlockSpec((rows, width), lambda b, i, p: (row(b, i), p))
    bm_spec = pl.BlockSpec((1, pair, BLK, 2 * BLK), lambda b, i, p: (jnp.minimum(i, 1), p, 0, 0))
    if has_prev:
        in_specs, args = [cur_spec, cur_spec, prev_spec, cur_spec, prev_spec, bm_spec], (q, k, k, v, v, bm)
    else:
        in_specs, args = [cur_spec, cur_spec, cur_spec, bm_spec], (q, k, v, bm)
    return pl.pallas_call(
        functools.partial(_band_attn_kernel, n_heads=pair, gqa=1, with_sink=False, dil=dil,
                          has_prev=has_prev),
        grid=(BATCH, nchunk, npair),
        in_specs=in_specs,
        out_specs=[out_spec, out_spec],
        out_shape=[jax.ShapeDtypeStruct((NP_TOK, D_GROUP_W), F32)] * 2,
        compiler_params=_cparams("parallel", "parallel", "parallel"),
        name=f"dilated_prompt_g{g}",
    )(*args)


SWA_SAMPLE_BB = 8


def _pad_rows(x, rows):
    return jnp.concatenate([x, jnp.zeros((rows - x.shape[0], x.shape[1]), x.dtype)], axis=0)


def _swa_sample_kernel(sink_ref, q_ref, kv_ref, c_ref, bmc_ref, bmn_ref, o_ref):
    for bb in range(SWA_SAMPLE_BB):
        rs = slice(bb * DEC_SEQ, (bb + 1) * DEC_SEQ)
        q = (q_ref[rs, :] * HEAD_DIM ** -0.5).astype(BF16)
        kvn = _pad_rows(kv_ref[rs, :], BLK).astype(BF16)
        cache = c_ref[0, bb].astype(BF16)
        outs = [None] * A_Q_HEADS
        for hk in range(A_KV_HEADS):
            heads = range(hk * A_GQA, (hk + 1) * A_GQA)
            qs = jnp.concatenate([q[:, h * HEAD_DIM:(h + 1) * HEAD_DIM] for h in heads], axis=0)
            ksl = slice(hk * HEAD_DIM, (hk + 1) * HEAD_DIM)
            vsl = slice(A_KV_W + hk * HEAD_DIM, A_KV_W + (hk + 1) * HEAD_DIM)
            brow = slice(hk * A_GQA * DEC_SEQ, (hk + 1) * A_GQA * DEC_SEQ)
            s1 = jnp.dot(qs, cache[ksl, :], preferred_element_type=F32) + bmc_ref[brow, :]
            s2 = _nt_dot(qs, kvn[:, ksl]) + bmn_ref[brow, :]
            sink = jnp.concatenate(
                [jnp.full((DEC_SEQ, 1), sink_ref[h], F32) for h in heads], axis=0)
            (p1, p2), den, _ = _softmax_parts([s1, s2], sink)
            o = _nt_dot(p1, cache[vsl, :]) + jnp.dot(p2, kvn[:, vsl], preferred_element_type=F32)
            for j, h in enumerate(heads):
                outs[h] = o[j * DEC_SEQ:(j + 1) * DEC_SEQ, :]
        o_ref[rs, :] = jnp.concatenate(outs, axis=1)


def _swa_sample(q, kv, cache_all, layer, sinks, bmc, bmn):
    bb = SWA_SAMPLE_BB
    return pl.pallas_call(
        _swa_sample_kernel,
        grid=(DEC_BATCH // bb,),
        in_specs=[pl.BlockSpec(memory_space=pltpu.SMEM),
                  pl.BlockSpec((bb * DEC_SEQ, A_Q_W), lambda i: (i, 0)),
                  pl.BlockSpec((bb * DEC_SEQ, 2 * A_KV_W), lambda i: (i, 0)),
                  pl.BlockSpec((1, bb, 2 * A_KV_W, A_WINDOW), lambda i: (layer, i, 0, 0)),
                  pl.BlockSpec(bmc.shape, lambda i: (0, 0)),
                  pl.BlockSpec(bmn.shape, lambda i: (0, 0))],
        out_specs=pl.BlockSpec((bb * DEC_SEQ, A_Q_W), lambda i: (i, 0)),
        out_shape=jax.ShapeDtypeStruct((NS_TOK, A_Q_W), F32),
        compiler_params=_cparams("parallel"),
        name="swa_sample",
    )(sinks, q, kv, cache_all, bmc, bmn)


def _dil_sample_kernel(q_ref, k_ref, v_ref, c0_ref, c1_ref, c2_ref,
                       bc0_ref, bc1_ref, bc2_ref, bn_ref,
                       o0_ref, o1_ref, o2_ref, l0_ref, l1_ref, l2_ref):
    q = q_ref[...] * HEAD_DIM ** -0.5
    k = k_ref[...]
    v = v_ref[...]
    nrow = D_HEADS_PER * DEC_SEQ
    row_head = lax.broadcasted_iota(jnp.int32, (nrow, D_GROUP_W), 0) // DEC_SEQ
    lane_head = lax.broadcasted_iota(jnp.int32, (nrow, D_GROUP_W), 1) // HEAD_DIM
    head_mask = row_head == lane_head
    out_lane_head = lax.broadcasted_iota(jnp.int32, (DEC_SEQ, D_GROUP_W), 1) // HEAD_DIM
    groups = ((c0_ref, bc0_ref, o0_ref, l0_ref), (c1_ref, bc1_ref, o1_ref, l1_ref),
              (c2_ref, bc2_ref, o2_ref, l2_ref))
    for g, (c_ref, bc_ref, o_ref, l_ref) in enumerate(groups):
        gsl = slice(g * D_GROUP_W, (g + 1) * D_GROUP_W)
        qbd = jnp.where(head_mask, jnp.concatenate([q[:, gsl]] * D_HEADS_PER, axis=0), 0.0).astype(BF16)
        kn = _pad_rows(k[:, gsl], BLK).astype(BF16)
        vn = _pad_rows(v[:, gsl], BLK).astype(BF16)
        kt = c_ref[0, 0, 0:D_GROUP_W, :].astype(BF16)
        vt = c_ref[0, 0, D_GROUP_W:2 * D_GROUP_W, :].astype(BF16)
        s1 = jnp.dot(qbd, kt, preferred_element_type=F32) + bc_ref[0]
        s2 = _nt_dot(qbd, kn) + bn_ref[g]
        (p1, p2), den, m = _softmax_parts([s1, s2], None)
        of = _nt_dot(p1, vt) + jnp.dot(p2, vn, preferred_element_type=F32)
        lse = m + jnp.log(den)
        og = jnp.zeros((DEC_SEQ, D_GROUP_W), F32)
        lg = jnp.zeros((DEC_SEQ, D_GROUP_W), F32)
        for h in range(D_HEADS_PER):
            rs = slice(h * DEC_SEQ, (h + 1) * DEC_SEQ)
            sel = out_lane_head == h
            og = og + jnp.where(sel, of[rs, :], 0.0)
            lg = lg + jnp.where(sel, lse[rs, :], 0.0)
        o_ref[...] = og
        l_ref[...] = lg


def _shift_block(c_ref, n_ref, o_ref, i):
    width, wc = c_ref.shape[2], c_ref.shape[3]
    lane = lax.broadcasted_iota(jnp.int32, (width, BLK), 1)
    shifted = pltpu.roll(c_ref[0, i], wc - DEC_SEQ, 1)
    new_t = jnp.concatenate([jnp.zeros((BLK - DEC_SEQ, width), F32), n_ref[0, i]], axis=0).T
    if wc > BLK:
        o_ref[0, i, :, 0:wc - BLK] = shifted[:, 0:wc - BLK]
    o_ref[0, i, :, wc - BLK:wc] = jnp.where(lane >= BLK - DEC_SEQ, new_t, shifted[:, wc - BLK:wc])


def _shift_kernel(c_ref, n_ref, o_ref, *, bb):
    for i in range(bb):
        _shift_block(c_ref, n_ref, o_ref, i)


def _dilated_sample(q, k, v, caches, layer, bcs, bn):
    tok_spec = pl.BlockSpec((DEC_SEQ, D_QKV_W), lambda b: (b, 0))
    out_spec = pl.BlockSpec((DEC_SEQ, D_GROUP_W), lambda b: (b, 0))
    outs = pl.pallas_call(
        _dil_sample_kernel,
        grid=(DEC_BATCH,),
        in_specs=[tok_spec, tok_spec, tok_spec]
        + [pl.BlockSpec((1, 1, 2 * D_GROUP_W, win), lambda b: (layer, b, 0, 0)) for win, _ in D_PAIRS]
        + [pl.BlockSpec(bc.shape, lambda b: (0, 0, 0)) for bc in bcs]
        + [pl.BlockSpec(bn.shape, lambda b: (0, 0, 0))],
        out_specs=[out_spec] * 6,
        out_shape=[jax.ShapeDtypeStruct((NS_TOK, D_GROUP_W), F32)] * 6,
        compiler_params=_cparams("parallel"),
        name="dilated_sample",
    )(q, k, v, *caches, *bcs, bn)
    return outs[:3], outs[3:]


def _dil_sample_shift_kernel(q_ref, k_ref, v_ref, c0_ref, c1_ref, c2_ref, n0_ref, n1_ref, n2_ref,
                             bc0_ref, bc1_ref, bc2_ref, bn_ref,
                             o0_ref, o1_ref, o2_ref, l0_ref, l1_ref, l2_ref, s0_ref, s1_ref, s2_ref):
    for c_ref, n_ref, s_ref in ((c0_ref, n0_ref, s0_ref), (c1_ref, n1_ref, s1_ref), (c2_ref, n2_ref, s2_ref)):
        _shift_block(c_ref, n_ref, s_ref, 0)

    @pl.when(pl.program_id(0) == pl.num_programs(0) - 1)
    def _():
        _dil_sample_kernel(q_ref, k_ref, v_ref, c0_ref, c1_ref, c2_ref, bc0_ref, bc1_ref, bc2_ref, bn_ref,
                           o0_ref, o1_ref, o2_ref, l0_ref, l1_ref, l2_ref)


def _dilated_sample_and_shift(q, k, v, caches, new_rows, bcs, bn):
    n_layers = caches[0].shape[0]
    last = n_layers - 1
    tok_row = lambda l, b: jnp.where(l == last, b, 0)
    tok_spec = pl.BlockSpec((DEC_SEQ, D_QKV_W), lambda l, b: (tok_row(l, b), 0))
    out_spec = pl.BlockSpec((DEC_SEQ, D_GROUP_W), lambda l, b: (tok_row(l, b), 0))
    cache_specs = [pl.BlockSpec((1, 1, 2 * D_GROUP_W, win), lambda l, b: (l, b, 0, 0)) for win, _ in D_PAIRS]
    new_spec = pl.BlockSpec((1, 1, DEC_SEQ, 2 * D_GROUP_W), lambda l, b: (l, b, 0, 0))
    outs = pl.pallas_call(
        _dil_sample_shift_kernel,
        grid=(n_layers, DEC_BATCH),
        in_specs=[tok_spec, tok_spec, tok_spec] + cache_specs + [new_spec] * 3
        + [pl.BlockSpec(bc.shape, lambda l, b: (0, 0, 0)) for bc in bcs]
        + [pl.BlockSpec(bn.shape, lambda l, b: (0, 0, 0))],
        out_specs=[out_spec] * 6 + cache_specs,
        out_shape=[jax.ShapeDtypeStruct((NS_TOK, D_GROUP_W), F32)] * 6
        + [jax.ShapeDtypeStruct(c.shape, c.dtype) for c in caches],
        compiler_params=_cparams("arbitrary", "arbitrary"),
        name="dilated_sample_shift",
    )(q, k, v, *caches, *new_rows, *bcs, bn)
    return outs[:3], outs[3:6], outs[6:]


def _shift_cache(cache, new_rows, name):
    n_layers, _, width, wc = cache.shape
    bb = max(1, (4 * 2 ** 20) // (wc * width * 4))
    return pl.pallas_call(
        functools.partial(_shift_kernel, bb=bb),
        grid=(n_layers, DEC_BATCH // bb),
        in_specs=[pl.BlockSpec((1, bb, width, wc), lambda l, i: (l, i, 0, 0)),
                  pl.BlockSpec((1, bb, DEC_SEQ, width), lambda l, i: (l, i, 0, 0))],
        out_specs=pl.BlockSpec((1, bb, width, wc), lambda l, i: (l, i, 0, 0)),
        out_shape=jax.ShapeDtypeStruct(cache.shape, cache.dtype),
        compiler_params=_cparams("parallel", "parallel"),
        name=name,
    )(cache, new_rows)


def _to_time_major(src_ref, dst_ref, off, bt, lc):
    for b in range(bt):
        x = src_ref[b]
        for j in range(dst_ref.shape[0]):
            dst_ref[j, pl.ds(off + b, lc, stride=bt), :] = x[:, j * LANES:(j + 1) * LANES]


def _from_time_major(src_ref, dst_ref, bt, lc):
    for b in range(bt):
        dst_ref[b] = jnp.concatenate(
            [src_ref[j, pl.ds(b, lc, stride=bt), :] for j in range(src_ref.shape[0])], axis=1)


def _lane_blocks(ref, r0, nrows):
    return jnp.concatenate([ref[j, r0:r0 + nrows, :] for j in range(ref.shape[0])], axis=1)


def _store_lane_blocks(ref, r0, x):
    for j in range(ref.shape[0]):
        ref[j, r0:r0 + x.shape[0], :] = x[:, j * LANES:(j + 1) * LANES]


def _token_spec(bt, lc, width):
    return pl.BlockSpec((bt, lc, width), lambda c: (0, c, 0))


def _s5_kernel(u_ref, h0_ref, bm_ref, cm_ref, lam_ref, d_ref, gw_ref, gb_ref, o_ref, hl_ref,
               us_ref, hs_ref, *, bt, lc):
    rows = lc * bt
    blk_w = 2 * B_HALF

    @pl.when(pl.program_id(0) == 0)
    def _():
        hs_ref[0:bt, :] = h0_ref[...]

    _to_time_major(u_ref, us_ref, 0, bt, lc)
    u = _lane_blocks(us_ref, 0, rows)
    ub = u.astype(BF16)
    for j in range(B_LANE_BLOCKS):
        cols = jnp.dot(ub[:, j * LANES:(j + 1) * LANES], bm_ref[j], preferred_element_type=F32)
        hs_ref[bt:, j * blk_w:j * blk_w + B_HALF] = cols[:, :B_HALF]
        hs_ref[bt:, j * blk_w + B_HALF:(j + 1) * blk_w] = cols[:, :B_HALF] + cols[:, B_HALF:]

    def step(t, carry):
        r0 = pl.multiple_of(t * bt, bt)
        for j in range(B_LANE_BLOCKS):
            re = slice(j * blk_w, j * blk_w + B_HALF)
            im = slice(j * blk_w + B_HALF, (j + 1) * blk_w)
            lr = lam_ref[2 * j:2 * j + 1, :]
            li = lam_ref[2 * j + 1:2 * j + 2, :]
            pr = hs_ref[pl.ds(r0, bt), re]
            pi = hs_ref[pl.ds(r0, bt), im]
            hs_ref[pl.ds(r0 + bt, bt), re] = lr * pr - li * pi + hs_ref[pl.ds(r0 + bt, bt), re]
            hs_ref[pl.ds(r0 + bt, bt), im] = lr * pi + li * pr + hs_ref[pl.ds(r0 + bt, bt), im]
        return carry

    lax.fori_loop(0, lc, step, 0)

    ys = []
    for j in range(B_LANE_BLOCKS):
        h_re = hs_ref[bt:, j * blk_w:j * blk_w + B_HALF]
        h_im = hs_ref[bt:, j * blk_w + B_HALF:(j + 1) * blk_w]
        lhs = jnp.concatenate([h_re + h_im, h_im], axis=1).astype(BF16)
        ys.append(jnp.dot(lhs, cm_ref[j], preferred_element_type=F32))
    y = jax.nn.gelu(jnp.concatenate(ys, axis=1) + d_ref[...] * u)
    z = jnp.dot(y.astype(BF16), gw_ref[...], preferred_element_type=F32) + gb_ref[...]
    _store_lane_blocks(us_ref, 0, y * jax.nn.sigmoid(z))
    _from_time_major(us_ref, o_ref, bt, lc)
    last = hs_ref[rows:rows + bt, :]
    hl_ref[...] = last
    hs_ref[0:bt, :] = last


def _s5_mixer(u, h0, prm, bt, seq, lc, name):
    tok = _token_spec(bt, lc, B_WIDTH)
    u_in = u.reshape(bt, seq, B_WIDTH)
    full = lambda a: pl.BlockSpec(a.shape, lambda c: (0,) * a.ndim)
    out, hl = pl.pallas_call(
        functools.partial(_s5_kernel, bt=bt, lc=lc),
        grid=(seq // lc,),
        in_specs=[tok, full(h0), full(prm["bm"]), full(prm["cm"]), full(prm["lam"]), full(prm["d"]),
                  full(prm["glu_w"]), full(prm["glu_b"])],
        out_specs=[tok, pl.BlockSpec((bt, B_STATE_W), lambda c: (0, 0))],
        out_shape=[jax.ShapeDtypeStruct(u_in.shape, F32), jax.ShapeDtypeStruct((bt, B_STATE_W), F32)],
        scratch_shapes=[pltpu.VMEM((B_WIDTH // LANES, lc * bt, LANES), F32),
                        pltpu.VMEM((lc * bt + bt, B_STATE_W), F32)],
        compiler_params=_cparams("arbitrary"),
        name=name,
    )(u_in, h0, prm["bm"], prm["cm"], prm["lam"], prm["d"], prm["glu_w"], prm["glu_b"])
    return out.reshape(bt * seq, B_WIDTH), hl


def _s5_params(a_re, a_im, log_dt, b_re, b_im, c_re, c_im, d, glu_w, glu_b):
    lam = lax.complex(a_re.astype(F32), a_im.astype(F32))
    dt = jnp.exp(log_dt.astype(F32))[:, None]
    lam_bar = jnp.exp(lam * dt)
    b_bar = ((lam_bar - 1.0) / lam)[..., None] * lax.complex(b_re.astype(F32), b_im.astype(F32))
    nb, gb = B_LANE_BLOCKS, B_GROUPS_PER_BLOCK
    eye = jnp.eye(gb, dtype=F32)

    def in_mat(part):
        p = part.reshape(nb, gb, B_STATE, B_GROUP_CH)
        return jnp.einsum("jgnc,gh->jgchn", p, eye).reshape(nb, gb * B_GROUP_CH, gb * B_STATE)

    def out_mat(part):
        p = part.reshape(nb, gb, B_GROUP_CH, B_STATE)
        return jnp.einsum("jgcn,gh->jgnhc", p, eye).reshape(nb, gb * B_STATE, gb * B_GROUP_CH)

    bm = jnp.concatenate([in_mat(b_bar.real), in_mat(b_bar.imag - b_bar.real)], axis=2).astype(BF16)
    c_re32, c_im32 = c_re.astype(F32), c_im.astype(F32)
    cm = jnp.concatenate([out_mat(c_re32), out_mat(-(c_re32 + c_im32))], axis=1).astype(BF16)
    lam_rows = jnp.stack([lam_bar.real.reshape(nb, B_HALF), lam_bar.imag.reshape(nb, B_HALF)],
                         axis=1).reshape(2 * nb, B_HALF)
    return {"bm": bm, "cm": cm, "lam": lam_rows, "d": d.astype(F32).reshape(1, B_WIDTH),
            "glu_w": glu_w.astype(BF16), "glu_b": glu_b.astype(F32).reshape(1, B_WIDTH)}


def _s5_state_to_cols(state):
    bt = state.shape[0]
    s = state.astype(F32).reshape(bt, B_LANE_BLOCKS, B_GROUPS_PER_BLOCK, B_STATE, 2)
    return s.transpose(0, 1, 4, 2, 3).reshape(bt, B_STATE_W)


def _s5_cols_to_state(cols):
    bt = cols.shape[0]
    s = cols.reshape(bt, B_LANE_BLOCKS, 2, B_GROUPS_PER_BLOCK, B_STATE)
    return s.transpose(0, 1, 3, 4, 2).reshape(bt, B_GROUPS, B_STATE, 2)


def _rglru_kernel(xr_ref, gate_ref, cb_ref, h0_ref, cw_ref, cbias_ref, wa_ref, ba_ref, wx_ref, bx_ref,
                  nsp_ref, o_ref, hl_ref, xp_ref, gs_ref, a_ref, hs_ref, *, bt, lc):
    rows = lc * bt
    pad = (C_CONV - 1) * bt

    @pl.when(pl.program_id(0) == 0)
    def _():
        _store_lane_blocks(xp_ref, 0, cb_ref[...])
        hs_ref[0:bt, :] = h0_ref[...]

    _to_time_major(xr_ref, xp_ref, pad, bt, lc)
    _to_time_major(gate_ref, gs_ref, 0, bt, lc)
    xc = _lane_blocks(xp_ref, 0, rows) * cw_ref[0:1, :]
    for tap in range(1, C_CONV):
        xc = xc + _lane_blocks(xp_ref, tap * bt, rows) * cw_ref[tap:tap + 1, :]
    xcf = xc + cbias_ref[...]
    xb = xcf.astype(BF16)
    r = jax.nn.sigmoid(jnp.dot(xb, wa_ref[...], preferred_element_type=F32) + ba_ref[...])
    i = jax.nn.sigmoid(jnp.dot(xb, wx_ref[...], preferred_element_type=F32) + bx_ref[...])
    log_a = nsp_ref[...] * r
    a = jnp.exp(log_a)
    a_ref[...] = a
    hs_ref[bt:, :] = jnp.sqrt(-jnp.tanh(log_a) * (a * a + 1.0)) * (i * xcf)

    def step(t, carry):
        r0 = pl.multiple_of(t * bt, bt)
        hs_ref[pl.ds(r0 + bt, bt), :] = (a_ref[pl.ds(r0, bt), :] * hs_ref[pl.ds(r0, bt), :]
                                         + hs_ref[pl.ds(r0 + bt, bt), :])
        return carry

    lax.fori_loop(0, lc, step, 0)

    _store_lane_blocks(gs_ref, 0, hs_ref[bt:, :] * jax.nn.gelu(_lane_blocks(gs_ref, 0, rows)))
    _from_time_major(gs_ref, o_ref, bt, lc)
    last = hs_ref[rows:rows + bt, :]
    hl_ref[...] = last
    hs_ref[0:bt, :] = last
    _store_lane_blocks(xp_ref, 0, _lane_blocks(xp_ref, rows, pad))


def _rglru_mixer(xr, gate, conv_buf, h0, prm, bt, seq, lc, name):
    rows = lc * bt
    tok = _token_spec(bt, lc, C_WIDTH)
    shape3 = lambda x: x.reshape(bt, seq, C_WIDTH)
    full = lambda a: pl.BlockSpec(a.shape, lambda c: (0,) * a.ndim)
    names = ("conv_w", "conv_b", "wa", "ba", "wx", "bx", "nsp")
    out, hl = pl.pallas_call(
        functools.partial(_rglru_kernel, bt=bt, lc=lc),
        grid=(seq // lc,),
        in_specs=[tok, tok, full(conv_buf), full(h0)] + [full(prm[k]) for k in names],
        out_specs=[tok, pl.BlockSpec((bt, C_WIDTH), lambda c: (0, 0))],
        out_shape=[jax.ShapeDtypeStruct(shape3(xr).shape, F32), jax.ShapeDtypeStruct((bt, C_WIDTH), F32)],
        scratch_shapes=[pltpu.VMEM((C_WIDTH // LANES, rows + (C_CONV - 1) * bt, LANES), F32),
                        pltpu.VMEM((C_WIDTH // LANES, rows, LANES), F32),
                        pltpu.VMEM((rows, C_WIDTH), F32),
                        pltpu.VMEM((rows + bt, C_WIDTH), F32)],
        compiler_params=_cparams("arbitrary"),
        name=name,
    )(shape3(xr), shape3(gate), conv_buf, h0, *[prm[k] for k in names])
    return out.reshape(bt * seq, C_WIDTH), hl


def _rglru_params(conv_w, conv_b, gate_a_w, gate_a_b, gate_x_w, gate_x_b, lru_lambda):
    eye = jnp.eye(C_BLOCKS, dtype=F32)

    def block_diag(w):
        return jnp.einsum("njk,nm->njmk", w.astype(F32), eye).reshape(C_WIDTH, C_WIDTH).astype(BF16)

    row = lambda x: x.astype(F32).reshape(1, C_WIDTH)
    return {"conv_w": conv_w.astype(F32), "conv_b": row(conv_b),
            "wa": block_diag(gate_a_w), "ba": row(gate_a_b),
            "wx": block_diag(gate_x_w), "bx": row(gate_x_b),
            "nsp": row(-C_POWER * jax.nn.softplus(-lru_lambda.astype(F32)))}


def _residual_out(x_ref, acc, tail_refs):
    *route_refs, o_ref = tail_refs
    x1 = x_ref[...] + acc
    if not route_refs:
        o_ref[...] = x1
        return
    g_ref, rw_ref, rb_ref = route_refs
    o_ref[:, 0:D_MODEL] = x1
    o_ref[:, D_MODEL:] = _route_lanes(x1, g_ref[...], rw_ref[...], rb_ref[...])


def _out_proj_even_kernel(x_ref, oa_ref, ob_ref, w_ref, *tail_refs):
    acc = jnp.dot(oa_ref[...].astype(BF16), w_ref[0:A_Q_W, :], preferred_element_type=F32)
    acc = acc + jnp.dot(ob_ref[...].astype(BF16), w_ref[A_Q_W:, :], preferred_element_type=F32)
    _residual_out(x_ref, acc, tail_refs)


def _out_proj_odd_kernel(x_ref, oc_ref, o0_ref, o1_ref, o2_ref, l0_ref, l1_ref, l2_ref, w_ref, *tail_refs):
    l0, l1, l2 = l0_ref[...], l1_ref[...], l2_ref[...]
    m = jnp.maximum(jnp.maximum(l0, l1), l2)
    e0, e1, e2 = jnp.exp(l0 - m), jnp.exp(l1 - m), jnp.exp(l2 - m)
    od = (o0_ref[...] * e0 + o1_ref[...] * e1 + o2_ref[...] * e2) / (e0 + e1 + e2)
    acc = jnp.dot(oc_ref[...].astype(BF16), w_ref[0:C_WIDTH, :], preferred_element_type=F32)
    acc = acc + jnp.dot(od.astype(BF16), w_ref[C_WIDTH:, :], preferred_element_type=F32)
    _residual_out(x_ref, acc, tail_refs)


def _out_proj(kernel, x, parts, w, name, in_place=True, route_prm=None):
    n = x.shape[0]
    tm = 512
    in_specs = ([pl.BlockSpec((tm, D_MODEL), lambda i: (i, 0))]
                + [pl.BlockSpec((tm, p.shape[1]), lambda i: (i, 0)) for p in parts]
                + [pl.BlockSpec(w.shape, lambda i: (0, 0))])
    args = [x, *parts, w]
    width = D_MODEL
    if route_prm is not None:
        for key in ("g", "rw", "rb"):
            in_specs.append(pl.BlockSpec(route_prm[key].shape, lambda i: (0, 0)))
            args.append(route_prm[key])
        width = MOE_AUG_W
    return pl.pallas_call(
        kernel,
        grid=(n // tm,),
        in_specs=in_specs,
        out_specs=pl.BlockSpec((tm, width), lambda i: (i, 0)),
        out_shape=jax.ShapeDtypeStruct((n, width), F32),
        input_output_aliases={0: 0} if in_place and route_prm is None else {},
        compiler_params=_cparams("parallel"),
        name=name,
    )(*args)


MOE_TM = 512


def _ffn_norm(x, g):
    y = x * lax.rsqrt(jnp.mean(x * x, axis=-1, keepdims=True) + RMS_EPS)
    return (y * g).astype(BF16)


def _route(xb, rw, rb):
    logits = jnp.dot(xb, rw, preferred_element_type=F32) + rb
    lane = lax.broadcasted_iota(jnp.int32, logits.shape, 1).astype(F32)
    ninf = float("-inf")
    far = float(ROUTER_LANES)
    lg = jnp.where(lane < MOE_GROUPS, logits, ninf)
    gmax = jnp.max(lg, axis=1, keepdims=True)
    g_idx = jnp.min(jnp.where(lg == gmax, lane, far), axis=1, keepdims=True)
    g_w = 1.0 / jnp.sum(jnp.exp(lg - gmax), axis=1, keepdims=True)
    lane_grp = jnp.floor((lane - MOE_GROUPS) * (1.0 / MOE_PER_GROUP))
    in_grp = (lane >= MOE_GROUPS) & (lane < MOE_GROUPS + MOE_EXPERTS) & (lane_grp == g_idx)
    le = jnp.where(in_grp, logits, ninf)
    v1 = jnp.max(le, axis=1, keepdims=True)
    i1 = jnp.min(jnp.where(le == v1, lane, far), axis=1, keepdims=True)
    le2 = jnp.where(lane == i1, ninf, le)
    v2 = jnp.max(le2, axis=1, keepdims=True)
    i2 = jnp.min(jnp.where(le2 == v2, lane, far), axis=1, keepdims=True)
    e2 = jnp.exp(v2 - v1)
    w1 = g_w / (1.0 + e2)
    w2 = g_w * e2 / (1.0 + e2)
    return lane, g_idx, i1 - MOE_GROUPS, i2 - MOE_GROUPS, w1, w2


def _moe_kernel(x_ref, g_ref, rw_ref, rb_ref, wg_ref, wu_ref, wd_ref, o_ref, xn_ref, gates_ref, acc_ref):
    e = pl.program_id(1)

    @pl.when(e == 0)
    def _():
        xb0 = _ffn_norm(x_ref[...], g_ref[...])
        xn_ref[...] = xb0
        lane, _, e1, e2, w1, w2 = _route(xb0, rw_ref[...], rb_ref[...])
        gates_ref[...] = jnp.where(lane == e1, w1, 0.0) + jnp.where(lane == e2, w2, 0.0)
        acc_ref[...] = jnp.zeros_like(acc_ref)

    xb = xn_ref[...]
    gates = gates_ref[...]
    lane_i = lax.broadcasted_iota(jnp.int32, gates.shape, 1)
    gate = jnp.sum(jnp.where(lane_i == e, gates, 0.0), axis=1, keepdims=True)
    hg = jnp.dot(xb, wg_ref[0], preferred_element_type=F32)
    hu = jnp.dot(xb, wu_ref[0], preferred_element_type=F32)
    hid = jax.nn.silu(hg) * hu * gate
    acc_ref[...] += jnp.dot(hid.astype(BF16), wd_ref[0], preferred_element_type=F32)

    @pl.when(e == MOE_EXPERTS - 1)
    def _():
        o_ref[...] = x_ref[...] + acc_ref[...]


def _moe(x, prm, name):
    n = x.shape[0]
    tm = 2 * MOE_TM if n % (2 * MOE_TM) == 0 else MOE_TM
    return pl.pallas_call(
        _moe_kernel,
        grid=(n // tm, MOE_EXPERTS),
        in_specs=[pl.BlockSpec((tm, D_MODEL), lambda i, e: (i, 0)),
                  pl.BlockSpec((1, D_MODEL), lambda i, e: (0, 0)),
                  pl.BlockSpec((D_MODEL, ROUTER_LANES), lambda i, e: (0, 0)),
                  pl.BlockSpec((1, ROUTER_LANES), lambda i, e: (0, 0)),
                  pl.BlockSpec((1, D_MODEL, MOE_FF), lambda i, e: (e, 0, 0)),
                  pl.BlockSpec((1, D_MODEL, MOE_FF), lambda i, e: (e, 0, 0)),
                  pl.BlockSpec((1, MOE_FF, D_MODEL), lambda i, e: (e, 0, 0))],
        out_specs=pl.BlockSpec((tm, D_MODEL), lambda i, e: (i, 0)),
        out_shape=jax.ShapeDtypeStruct((n, D_MODEL), F32),
        scratch_shapes=[pltpu.VMEM((tm, D_MODEL), BF16),
                        pltpu.VMEM((tm, ROUTER_LANES), F32),
                        pltpu.VMEM((tm, D_MODEL), F32)],
        input_output_aliases={0: 0},
        compiler_params=_cparams("parallel", "arbitrary"),
        name=name,
    )(x, prm["g"], prm["rw"], prm["rb"], prm["wg"], prm["wu"], prm["wd"])


def _moe_params(norm_g, router_g, router_g_b, router_e, router_e_b, w_gate, w_up, w_down):
    used = MOE_GROUPS + MOE_EXPERTS
    rw = jnp.concatenate([router_g.astype(F32), router_e.astype(F32),
                          jnp.zeros((D_MODEL, ROUTER_LANES - used), F32)], axis=1)
    rb = jnp.concatenate([router_g_b.astype(F32), router_e_b.astype(F32),
                          jnp.zeros((ROUTER_LANES - used,), F32)]).reshape(1, ROUTER_LANES)
    return {"g": norm_g.astype(F32).reshape(1, D_MODEL), "rw": rw.astype(BF16), "rb": rb,
            "wg": w_gate.astype(BF16), "wu": w_up.astype(BF16), "wd": w_down.astype(BF16)}


MOE_AUG_W = D_MODEL + ROUTER_LANES
GATE_LANE0 = 1


def _route_lanes(x, g, rw, rb):
    lane, g_idx, e1, e2, w1, w2 = _route(_ffn_norm(x, g), rw, rb)
    l1 = e1 - g_idx * MOE_PER_GROUP + GATE_LANE0
    l2 = e2 - g_idx * MOE_PER_GROUP + GATE_LANE0
    return jnp.where(lane == 0.0, g_idx, 0.0) + jnp.where(lane == l1, w1, 0.0) + jnp.where(lane == l2, w2, 0.0)


def _moe_plan(group_col, n):
    tm = MOE_TM
    n_tiles_max = n // tm + MOE_GROUPS
    g = group_col.astype(jnp.int32)
    onehot = (g[:, None] == jnp.arange(MOE_GROUPS)[None, :]).astype(jnp.int32)
    counts = jnp.sum(onehot, axis=0)
    rank = jnp.sum((jnp.cumsum(onehot, axis=0) - onehot) * onehot, axis=1)
    padded = (counts + tm - 1) // tm * tm
    ends = jnp.cumsum(padded)
    starts = ends - padded
    pos = jnp.sum(onehot * starts[None, :], axis=1) + rank
    src = jnp.zeros((n_tiles_max * tm,), jnp.int32).at[pos].set(jnp.arange(n, dtype=jnp.int32))
    tile0 = jnp.arange(n_tiles_max, dtype=jnp.int32) * tm
    tile_group = jnp.minimum(jnp.sum((tile0[:, None] >= ends[None, :]).astype(jnp.int32), axis=1),
                             MOE_GROUPS - 1)
    real = jnp.clip(counts[tile_group] - (tile0 - starts[tile_group]), 0, tm)
    real = jnp.where(tile0 < ends[-1], real, 0).astype(jnp.int32)
    return src, tile_group.astype(jnp.int32), real, (ends[-1:] // tm).astype(jnp.int32)


def _moe_expert_kernel(src_ref, grp_ref, real_ref, nt_ref, xa_hbm, g_ref, wg_ref, wu_ref, wd_ref, out_hbm,
                       xbuf, ybuf, gsem, ssem):
    del grp_ref
    tm = MOE_TM
    i = pl.program_id(0)
    n_tiles = nt_ref[0]
    slot = lax.rem(i, 2)
    sub = SUBLANES

    def row_in(tile, s, j, u):
        row = src_ref[tile * tm + j * sub + u]
        return pltpu.make_async_copy(xa_hbm.at[row >> 3, pl.ds(row & 7, 1)],
                                     xbuf.at[s, j, pl.ds(u, 1)], gsem.at[s])

    def row_out(tile, s, j, u):
        row = src_ref[tile * tm + j * sub + u]
        return pltpu.make_async_copy(ybuf.at[s, j, pl.ds(u, 1)],
                                     out_hbm.at[row >> 3, pl.ds(row & 7, 1)], ssem.at[s])

    def start_gather(tile, s):
        def body(j, c):
            for u in range(sub):
                row_in(tile, s, j, u).start()
            return c
        lax.fori_loop(0, tm // sub, body, 0)

    def wait_gather(s):
        pltpu.make_async_copy(xa_hbm.at[pl.ds(0, tm // sub)], xbuf.at[s], gsem.at[s]).wait()

    def start_scatter(tile, s):
        real = real_ref[tile]
        full = real // sub

        def body(j, c):
            for u in range(sub):
                row_out(tile, s, j, u).start()
            return c
        lax.fori_loop(0, full, body, 0)

        def tail(r, c):
            row_out(tile, s, full, r - full * sub).start()
            return c
        lax.fori_loop(full * sub, real, tail, 0)

    def wait_scatter(tile, s):
        real = real_ref[tile]
        for k in range(tm.bit_length()):
            size = 1 << k

            @pl.when((real >> k) & 1 == 1)
            def _():
                if size >= sub:
                    piece = (ybuf.at[s, pl.ds(0, size // sub)], out_hbm.at[pl.ds(0, size // sub)])
                else:
                    piece = (ybuf.at[s, 0, pl.ds(0, size)], out_hbm.at[0, pl.ds(0, size)])
                pltpu.make_async_copy(piece[0], piece[1], ssem.at[s]).wait()

    @pl.when(i < n_tiles)
    def _():
        @pl.when(i == 0)
        def _():
            start_gather(0, 0)

        @pl.when(i + 1 < n_tiles)
        def _():
            start_gather(i + 1, 1 - slot)

        wait_gather(slot)

        @pl.when(i >= 2)
        def _():
            wait_scatter(i - 2, slot)

        xa = xbuf[slot].reshape(tm, MOE_AUG_W)
        x = xa[:, 0:D_MODEL]
        route = xa[:, D_MODEL:]
        xb = _ffn_norm(x, g_ref[...])
        acc = jnp.zeros((tm, D_MODEL), F32)
        for e in range(MOE_PER_GROUP):
            gate = route[:, GATE_LANE0 + e:GATE_LANE0 + e + 1]
            hg = jnp.dot(xb, wg_ref[0, e], preferred_element_type=F32)
            hu = jnp.dot(xb, wu_ref[0, e], preferred_element_type=F32)
            hid = jax.nn.silu(hg) * hu * gate
            acc = acc + jnp.dot(hid.astype(BF16), wd_ref[0, e], preferred_element_type=F32)
        ybuf[slot] = (x + acc).reshape(tm // sub, sub, D_MODEL)
        start_scatter(i, slot)

        @pl.when(i == n_tiles - 1)
        def _():
            wait_scatter(i, slot)

            @pl.when(i >= 1)
            def _():
                wait_scatter(i - 1, 1 - slot)


def _moe_experts(xa, plan, prm, name):
    n = xa.shape[0]
    tm = MOE_TM
    sub = SUBLANES
    src, tile_group, real, n_tiles = plan
    grouped = lambda w: w.reshape(MOE_GROUPS, MOE_PER_GROUP, w.shape[1], w.shape[2])
    w_spec = lambda a, b: pl.BlockSpec((1, MOE_PER_GROUP, a, b), lambda i, src, grp, real, nt: (grp[i], 0, 0, 0))
    grid_spec = pltpu.PrefetchScalarGridSpec(
        num_scalar_prefetch=4,
        grid=(n // tm + MOE_GROUPS,),
        in_specs=[pl.BlockSpec(memory_space=pl.ANY),
                  pl.BlockSpec((1, D_MODEL), lambda i, *_: (0, 0)),
                  w_spec(D_MODEL, MOE_FF), w_spec(D_MODEL, MOE_FF), w_spec(MOE_FF, D_MODEL)],
        out_specs=pl.BlockSpec(memory_space=pl.ANY),
        scratch_shapes=[pltpu.VMEM((2, tm // sub, sub, MOE_AUG_W), F32),
                        pltpu.VMEM((2, tm // sub, sub, D_MODEL), F32),
                        pltpu.SemaphoreType.DMA((2,)),
                        pltpu.SemaphoreType.DMA((2,))])
    out = pl.pallas_call(
        _moe_expert_kernel,
        grid_spec=grid_spec,
        out_shape=jax.ShapeDtypeStruct((n // sub, sub, D_MODEL), F32),
        compiler_params=_cparams("arbitrary"),
        name=name,
    )(src, tile_group, real, n_tiles, xa.reshape(n // sub, sub, MOE_AUG_W), prm["g"],
      grouped(prm["wg"]), grouped(prm["wu"]), grouped(prm["wd"]))
    return out.reshape(n, D_MODEL)


def _final_norm_kernel(x_ref, g_ref, o_ref):
    x = x_ref[...]
    y = x * lax.rsqrt(jnp.mean(x * x, axis=-1, keepdims=True) + RMS_EPS)
    o_ref[...] = y * g_ref[...]


def _final_norm(x, g, name):
    n = x.shape[0]
    tm = 512
    return pl.pallas_call(
        _final_norm_kernel,
        grid=(n // tm,),
        in_specs=[pl.BlockSpec((tm, D_MODEL), lambda i: (i, 0)),
                  pl.BlockSpec((1, D_MODEL), lambda i: (0, 0))],
        out_specs=pl.BlockSpec((tm, D_MODEL), lambda i: (i, 0)),
        out_shape=jax.ShapeDtypeStruct((n, D_MODEL), F32),
        compiler_params=_cparams("parallel"),
        name=name,
    )(x, g.astype(F32).reshape(1, D_MODEL))


def _t5_bucket(dist):
    n = jnp.maximum(dist, 0)
    max_exact = T5_BUCKETS // 2
    nf = jnp.maximum(n, 1).astype(F32)
    large = max_exact + (jnp.log(nf / max_exact) / math.log(T5_MAX_DIST / max_exact)
                         * (T5_BUCKETS - max_exact)).astype(jnp.int32)
    return jnp.where(n < max_exact, n, jnp.minimum(large, T5_BUCKETS - 1))


def _rel_bias(table, dist):
    hit = _t5_bucket(dist)[..., None] == jnp.arange(T5_BUCKETS)
    rows = jnp.sum(jnp.where(hit[..., None], table.astype(F32), 0.0), axis=-2)
    return jnp.moveaxis(rows, -1, 0)


def _band_bias_mask(table, max_dist, dil):
    r = jnp.arange(BLK)[:, None]
    s = jnp.arange(2 * BLK)[None, :]
    dist = BLK + r - s
    valid = (dist >= 0) & (dist <= max_dist)
    bias = _rel_bias(table, dist * dil)
    later = jnp.where(valid[None], bias, NEG)
    first = jnp.where((valid & (s >= BLK))[None], bias, NEG)
    return jnp.stack([first, later])


def _swa_sample_bias_mask(table):
    s = jnp.arange(DEC_SEQ)[:, None]
    col = jnp.arange(BLK)[None, :]
    dist_c = A_WINDOW + s - col
    dist_n = s - col
    tab = table[:, :A_Q_HEADS]
    bc = jnp.where(((dist_c >= 0) & (dist_c < A_WINDOW))[None], _rel_bias(tab, dist_c), NEG)
    bn = jnp.where(((dist_n >= 0) & (col < DEC_SEQ))[None], _rel_bias(tab, dist_n), NEG)
    return bc.reshape(A_Q_HEADS * DEC_SEQ, A_WINDOW), bn.reshape(A_Q_HEADS * DEC_SEQ, BLK)


def _dil_sample_bias_mask(table):
    s = jnp.arange(DEC_SEQ)[:, None]
    col = jnp.arange(BLK)[None, :]
    bcs, bns = [], []
    for g, (win, dil) in enumerate(D_PAIRS):
        lo = A_Q_HEADS + g * D_HEADS_PER
        tab = table[:, lo:lo + D_HEADS_PER]
        dist_c = win + s - jnp.arange(win)[None, :]
        valid_c = (dist_c >= 0) & (dist_c % dil == 0) & (dist_c <= win)
        bc = jnp.where(valid_c[None], _rel_bias(tab, dist_c), NEG)
        dist_n = s - col
        valid_n = (dist_n >= 0) & (dist_n % dil == 0) & (col < DEC_SEQ)
        bn = jnp.where(valid_n[None], _rel_bias(tab, dist_n), NEG)
        bcs.append(bc.reshape(1, D_HEADS_PER * DEC_SEQ, win))
        bns.append(bn.reshape(D_HEADS_PER * DEC_SEQ, BLK))
    return bcs, jnp.stack(bns)


def _native_cache(c):
    n_layers, nb, wc = c.shape[:3]
    return c.transpose(0, 1, 3, 4, 5, 2).reshape(n_layers, nb, -1, wc)


def _logical_cache(c, heads):
    n_layers, nb, _, wc = c.shape
    return c.reshape(n_layers, nb, 2, heads, HEAD_DIM, wc).transpose(0, 1, 5, 2, 3, 4)


def kernel(x_prompt, x_sample, cache_a_kv, state_b, state_c_h, state_c_conv, cache_d_g0, cache_d_g1, cache_d_g2, rel_table, norm_mix, norm_ffn, norm_final, w_in_even, w_out_even, sinks_a, s5_a_re, s5_a_im, s5_log_dt, s5_b_re, s5_b_im, s5_c_re, s5_c_im, s5_d, s5_glu_w, s5_glu_b, w_in_odd, w_out_odd, conv_w, conv_b, gate_a_w, gate_a_b, gate_x_w, gate_x_b, lru_lambda, moe_router_g, moe_router_g_b, moe_router_e, moe_router_e_b, moe_w_gate, moe_w_up, moe_w_down):
    xp = x_prompt.astype(F32).reshape(NP_TOK, D_MODEL)
    xs = x_sample.astype(F32).reshape(NS_TOK, D_MODEL)

    cache_a = _native_cache(cache_a_kv)
    caches_d = [_native_cache(c) for c in (cache_d_g0, cache_d_g1, cache_d_g2)]

    bm_a = _band_bias_mask(rel_table[:, :A_Q_HEADS], A_WINDOW - 1, 1)
    bm_d = [_band_bias_mask(rel_table[:, A_Q_HEADS + g * D_HEADS_PER:A_Q_HEADS + (g + 1) * D_HEADS_PER],
                            win // dil, dil) for g, (win, dil) in enumerate(D_PAIRS)]
    bmc_a_s, bmn_a_s = _swa_sample_bias_mask(rel_table)
    bcs_d_s, bn_d_s = _dil_sample_bias_mask(rel_table)

    a_p, a_new, b_p, b_s = [], [], [], []
    ch_p, ch_s, cc_p, cc_s = [], [], [], []
    d_p = [[], [], []]
    d_new = [[], [], []]

    for layer in range(DEPTH):
        mp = _moe_params(norm_ffn[layer], moe_router_g[layer], moe_router_g_b[layer], moe_router_e[layer],
                         moe_router_e_b[layer], moe_w_gate[layer], moe_w_up[layer], moe_w_down[layer])
        if layer % 2 == 0:
            e = layer // 2
            w_in = w_in_even[e].astype(BF16)
            w_out = w_out_even[e].astype(BF16)
            splits = (A_Q_W, 2 * A_KV_W, B_WIDTH)
            s5p = _s5_params(s5_a_re[e], s5_a_im[e], s5_log_dt[e], s5_b_re[e], s5_b_im[e],
                             s5_c_re[e], s5_c_im[e], s5_d[e], s5_glu_w[e], s5_glu_b[e])
            sinks = sinks_a[e].astype(F32)

            q_p, kv_p, u_p = _norm_proj(xp, norm_mix[layer], w_in, splits, f"in_proj_p{layer}")
            q_s, kv_s, u_s = _norm_proj(xs, norm_mix[layer], w_in, splits, f"in_proj_s{layer}")

            oa_p = _swa_prompt(q_p, kv_p, sinks, bm_a)
            oa_s = _swa_sample(q_s, kv_s, cache_a, e, sinks, bmc_a_s, bmn_a_s)

            ob_p, hl_p = _s5_mixer(u_p, jnp.zeros((BATCH, B_STATE_W), F32), s5p, BATCH, SEQ, 64, "s5_prompt")
            ob_s, hl_s = _s5_mixer(u_s, _s5_state_to_cols(state_b[e]), s5p, DEC_BATCH, DEC_SEQ, DEC_SEQ,
                                   "s5_sample")

            xa_p = _out_proj(_out_proj_even_kernel, xp, [oa_p, ob_p], w_out, f"out_proj_p{layer}",
                             route_prm=mp)
            xs = _out_proj(_out_proj_even_kernel, xs, [oa_s, ob_s], w_out, f"out_proj_s{layer}",
                           in_place=layer > 0)

            a_p.append(kv_p.reshape(BATCH, SEQ, 2, A_KV_HEADS, HEAD_DIM)[:, SEQ - A_WINDOW:])
            a_new.append(kv_s.reshape(DEC_BATCH, DEC_SEQ, 2 * A_KV_W))
            b_p.append(_s5_cols_to_state(hl_p))
            b_s.append(_s5_cols_to_state(hl_s))
        else:
            o = layer // 2
            w_in = w_in_odd[o].astype(BF16)
            w_out = w_out_odd[o].astype(BF16)
            splits = (C_WIDTH, C_WIDTH, D_QKV_W, D_QKV_W, D_QKV_W)
            lrp = _rglru_params(conv_w[o], conv_b[o], gate_a_w[o], gate_a_b[o], gate_x_w[o], gate_x_b[o],
                                lru_lambda[o])

            xr_p, gate_p, q_p, k_p, v_p = _norm_proj(xp, norm_mix[layer], w_in, splits, f"in_proj_p{layer}")
            xr_s, gate_s, q_s, k_s, v_s = _norm_proj(xs, norm_mix[layer], w_in, splits, f"in_proj_s{layer}")

            od_p, lse_p = [], []
            for g, (win, dil) in enumerate(D_PAIRS):
                og, lg = _dilated_prompt_group(q_p, k_p, v_p, bm_d[g], g, dil)
                od_p.append(og)
                lse_p.append(lg)
            k_s4 = k_s.reshape(DEC_BATCH, DEC_SEQ, D_N_GROUPS, D_GROUP_W)
            v_s4 = v_s.reshape(DEC_BATCH, DEC_SEQ, D_N_GROUPS, D_GROUP_W)
            for g in range(D_N_GROUPS):
                d_new[g].append(jnp.concatenate([k_s4[:, :, g], v_s4[:, :, g]], axis=-1))
            if o < caches_d[0].shape[0] - 1:
                od_s, lse_s = _dilated_sample(q_s, k_s, v_s, caches_d, o, bcs_d_s, bn_d_s)
            else:
                od_s, lse_s, shifted_d = _dilated_sample_and_shift(
                    q_s, k_s, v_s, caches_d, [jnp.stack(rows) for rows in d_new], bcs_d_s, bn_d_s)

            oc_p, hc_p = _rglru_mixer(xr_p, gate_p, jnp.zeros(((C_CONV - 1) * BATCH, C_WIDTH), F32),
                                      jnp.zeros((BATCH, C_WIDTH), F32), lrp, BATCH, SEQ, 128, "rglru_prompt")
            conv_s = state_c_conv[o].astype(F32).transpose(1, 0, 2).reshape((C_CONV - 1) * DEC_BATCH, C_WIDTH)
            oc_s, hc_s = _rglru_mixer(xr_s, gate_s, conv_s, state_c_h[o].astype(F32), lrp,
                                      DEC_BATCH, DEC_SEQ, DEC_SEQ, "rglru_sample")

            xa_p = _out_proj(_out_proj_odd_kernel, xp, [oc_p] + od_p + lse_p, w_out, f"out_proj_p{layer}",
                             route_prm=mp)
            xs = _out_proj(_out_proj_odd_kernel, xs, [oc_s] + list(od_s) + list(lse_s), w_out,
                           f"out_proj_s{layer}")

            ch_p.append(hc_p)
            ch_s.append(hc_s)
            cc_p.append(xr_p.reshape(BATCH, SEQ, C_WIDTH)[:, SEQ - (C_CONV - 1):])
            cc_s.append(xr_s.reshape(DEC_BATCH, DEC_SEQ, C_WIDTH)[:, DEC_SEQ - (C_CONV - 1):])
            k_p5 = k_p.reshape(BATCH, SEQ, D_N_GROUPS, D_HEADS_PER, HEAD_DIM)
            v_p5 = v_p.reshape(BATCH, SEQ, D_N_GROUPS, D_HEADS_PER, HEAD_DIM)
            for g, (win, dil) in enumerate(D_PAIRS):
                wc = min(win, SEQ)
                d_p[g].append(jnp.stack([k_p5[:, SEQ - wc:, g], v_p5[:, SEQ - wc:, g]], axis=2))

        xp = _moe_experts(xa_p, _moe_plan(xa_p[:, D_MODEL], NP_TOK), mp, f"moe_experts_p{layer}")
        xs = _moe(xs, mp, f"moe_s{layer}")

    y_prompt = _final_norm(xp, norm_final, "final_norm_p").reshape(BATCH, SEQ, D_MODEL)
    y_sample = _final_norm(xs, norm_final, "final_norm_s").reshape(DEC_BATCH, DEC_SEQ, D_MODEL)

    new_a = _logical_cache(_shift_cache(cache_a, jnp.stack(a_new), "shift_cache_a"), A_KV_HEADS)
    new_d = [_logical_cache(c, D_HEADS_PER) for c in shifted_d]

    return (y_prompt, y_sample,
            jnp.stack(a_p), new_a, jnp.stack(b_p), jnp.stack(b_s),
            jnp.stack(ch_p), jnp.stack(ch_s), jnp.stack(cc_p), jnp.stack(cc_s),
            jnp.stack(d_p[0]), new_d[0], jnp.stack(d_p[1]), new_d[1], jnp.stack(d_p[2]), new_d[2])
```

```python
import functools
import math

import jax
import jax.numpy as jnp
from jax import lax
from jax.experimental import pallas as pl
from jax.experimental.pallas import tpu as pltpu

F32 = jnp.float32
BF16 = jnp.bfloat16

D_MODEL = 1024
BATCH = 8
SEQ = 2048
DEPTH = 4
DEC_BATCH = 128
DEC_SEQ = 8
HEAD_DIM = 64
BLK = 128
RMS_EPS = 1e-6
NEG = -1e30

A_Q_HEADS = 8
A_KV_HEADS = 2
A_GQA = 4
A_WINDOW = 128
A_Q_W = A_Q_HEADS * HEAD_DIM
A_KV_W = A_KV_HEADS * HEAD_DIM

B_WIDTH = 512
B_GROUP_CH = 16
B_GROUPS = 32
B_STATE = 64
B_LANE_BLOCKS = 4
B_GROUPS_PER_BLOCK = B_GROUPS // B_LANE_BLOCKS
B_HALF = B_GROUPS_PER_BLOCK * B_STATE
B_STATE_W = B_LANE_BLOCKS * 2 * B_HALF

C_WIDTH = 512
C_BLOCKS = 8
C_BLOCK_W = 64
C_CONV = 4
C_POWER = 8.0

D_PAIRS = ((128, 1), (512, 4), (2048, 16))
D_N_GROUPS = 3
D_HEADS_PER = 4
D_GROUP_W = D_HEADS_PER * HEAD_DIM
D_QKV_W = D_N_GROUPS * D_GROUP_W

T5_BUCKETS = 32
T5_MAX_DIST = 2048

MOE_GROUPS = 4
MOE_PER_GROUP = 4
MOE_EXPERTS = 16
MOE_FF = 256
ROUTER_LANES = 128

NP_TOK = BATCH * SEQ
NS_TOK = DEC_BATCH * DEC_SEQ

VMEM_LIMIT_BYTES = 52 * 2 ** 20


def _cparams(*sem):
    return pltpu.CompilerParams(dimension_semantics=sem, vmem_limit_bytes=VMEM_LIMIT_BYTES)


def _nt_dot(a, b):
    return lax.dot_general(a, b, (((1,), (1,)), ((), ())), preferred_element_type=F32)


LANES = 128
SUBLANES = 8


def _norm_proj_kernel(x_ref, g_ref, w_ref, *out_refs, splits):
    x = x_ref[...]
    y = x * lax.rsqrt(jnp.mean(x * x, axis=-1, keepdims=True) + RMS_EPS)
    xn = (y * g_ref[...]).astype(BF16)
    off = 0
    for o_ref, width in zip(out_refs, splits):
        o_ref[...] = jnp.dot(xn, w_ref[:, off:off + width], preferred_element_type=F32)
        off += width


def _norm_proj(x, g, w, splits, name):
    n = x.shape[0]
    tm = 512
    return pl.pallas_call(
        functools.partial(_norm_proj_kernel, splits=splits),
        grid=(n // tm,),
        in_specs=[pl.BlockSpec((tm, D_MODEL), lambda i: (i, 0)),
                  pl.BlockSpec((1, D_MODEL), lambda i: (0, 0)),
                  pl.BlockSpec(w.shape, lambda i: (0, 0))],
        out_specs=[pl.BlockSpec((tm, s), lambda i: (i, 0)) for s in splits],
        out_shape=[jax.ShapeDtypeStruct((n, s), F32) for s in splits],
        compiler_params=_cparams("parallel"),
        name=name,
    )(x, g.reshape(1, D_MODEL), w)


BAND_NSUB = 2


def _softmax_parts(scores, sink):
    m = jnp.max(scores[0], axis=1, keepdims=True)
    for s in scores[1:]:
        m = jnp.maximum(m, jnp.max(s, axis=1, keepdims=True))
    if sink is not None:
        m = jnp.maximum(m, sink)
    ps = [jnp.exp(s - m) for s in scores]
    den = jnp.sum(ps[0], axis=1, keepdims=True)
    for p in ps[1:]:
        den = den + jnp.sum(p, axis=1, keepdims=True)
    if sink is not None:
        den = den + jnp.exp(sink - m)
    inv = 1.0 / den
    return [(p * inv).astype(BF16) for p in ps], den, m


def _band_attn_kernel(*refs, n_heads, gqa, with_sink, dil, has_prev, nsub=1):
    refs = list(refs)
    sink_ref = refs.pop(0) if with_sink else None
    q_ref, kc_ref = refs.pop(0), refs.pop(0)
    kp_ref = refs.pop(0) if has_prev else None
    vc_ref = refs.pop(0)
    vp_ref = refs.pop(0) if has_prev else None
    bm_ref = refs.pop(0)
    bm_later_ref = refs.pop(0) if nsub > 1 else bm_ref
    o_ref = refs.pop(0)
    lse_ref = refs.pop(0) if refs else None
    n_kv = n_heads // gqa
    if dil == 1:
        row_sel = [slice(s * BLK, (s + 1) * BLK) for s in range(nsub)]
    else:
        row_sel = [pl.ds(r, BLK, stride=dil) for r in range(dil)]
    bm_of = [bm_ref if (dil > 1 or s == 0) else bm_later_ref for s in range(len(row_sel))]
    sc, sp, v_cur, v_prev = [], [], [], []
    kc = vc = None
    for s, rows in enumerate(row_sel):
        q = (q_ref[rows, :] * HEAD_DIM ** -0.5).astype(BF16)
        if has_prev:
            if dil == 1 and s > 0:
                kp, vp = kc, vc
            elif dil == 1:
                kp, vp = kp_ref[...].astype(BF16), vp_ref[...].astype(BF16)
            else:
                kp, vp = kp_ref[rows, :].astype(BF16), vp_ref[rows, :].astype(BF16)
        kc = kc_ref[rows, :].astype(BF16)
        vc = vc_ref[rows, :].astype(BF16)
        for hk in range(n_kv):
            ksl = slice(hk * HEAD_DIM, (hk + 1) * HEAD_DIM)
            qs = jnp.concatenate([q[:, h * HEAD_DIM:(h + 1) * HEAD_DIM]
                                  for h in range(hk * gqa, (hk + 1) * gqa)], axis=0)
            sc.append(_nt_dot(qs, kc[:, ksl]))
            v_cur.append(vc[:, ksl])
            if has_prev:
                sp.append(_nt_dot(qs, kp[:, ksl]))
                v_prev.append(vp[:, ksl])
    n_units = len(row_sel)
    bias_c = jnp.concatenate([bm[0, h, :, BLK:2 * BLK] for bm in bm_of for h in range(n_heads)], axis=0)
    scores = [jnp.concatenate(sc, axis=0) + bias_c]
    if has_prev:
        bias_p = jnp.concatenate([bm[0, h, :, 0:BLK] for bm in bm_of for h in range(n_heads)], axis=0)
        scores.append(jnp.concatenate(sp, axis=0) + bias_p)
    sink = None
    if with_sink:
        sink = jnp.concatenate([jnp.full((BLK, 1), sink_ref[h], F32) for h in range(n_heads)] * n_units,
                               axis=0)
    ps, den, m = _softmax_parts(scores, sink)
    lse = m + jnp.log(den) if lse_ref is not None else None
    unit_rows = gqa * BLK
    for r, rows in enumerate(row_sel):
        outs = []
        for hk in range(n_kv):
            u = r * n_kv + hk
            usl = slice(u * unit_rows, (u + 1) * unit_rows)
            o = jnp.dot(ps[0][usl, :], v_cur[u], preferred_element_type=F32)
            if has_prev:
                o = o + jnp.dot(ps[1][usl, :], v_prev[u], preferred_element_type=F32)
            outs += [o[j * BLK:(j + 1) * BLK, :] for j in range(gqa)]
        o_ref[rows, :] = jnp.concatenate(outs, axis=1)
        if lse_ref is not None:
            base = r * n_heads * BLK
            lse_ref[rows, :] = jnp.concatenate(
                [jnp.broadcast_to(lse[base + h * BLK:base + (h + 1) * BLK, :], (BLK, HEAD_DIM))
                 for h in range(n_heads)], axis=1)


def _swa_prompt(q, kv, sinks, bm):
    nsub = BAND_NSUB
    nstep = SEQ // (BLK * nsub)
    row = lambda b, i: b * nstep + i
    prev = lambda b, i: (b * nstep + i) * nsub - jnp.minimum(i, 1)
    bm_spec = lambda pick: pl.BlockSpec((1, A_Q_HEADS, BLK, 2 * BLK), lambda b, i: (pick(i), 0, 0, 0))
    return pl.pallas_call(
        functools.partial(_band_attn_kernel, n_heads=A_Q_HEADS, gqa=A_GQA, with_sink=True, dil=1,
                          has_prev=True, nsub=nsub),
        grid=(BATCH, nstep),
        in_specs=[pl.BlockSpec(memory_space=pltpu.SMEM),
                  pl.BlockSpec((BLK * nsub, A_Q_W), lambda b, i: (row(b, i), 0)),
                  pl.BlockSpec((BLK * nsub, A_KV_W), lambda b, i: (row(b, i), 0)),
                  pl.BlockSpec((BLK, A_KV_W), lambda b, i: (prev(b, i), 0)),
                  pl.BlockSpec((BLK * nsub, A_KV_W), lambda b, i: (row(b, i), 1)),
                  pl.BlockSpec((BLK, A_KV_W), lambda b, i: (prev(b, i), 1)),
                  bm_spec(lambda i: jnp.minimum(i, 1)), bm_spec(lambda i: 1)],
        out_specs=pl.BlockSpec((BLK * nsub, A_Q_W), lambda b, i: (row(b, i), 0)),
        out_shape=jax.ShapeDtypeStruct((NP_TOK, A_Q_W), F32),
        compiler_params=_cparams("parallel", "parallel"),
        name="swa_prompt",
    )(sinks, q, kv, kv, kv, kv, bm, bm)


def _dilated_prompt_group(q, k, v, bm, g, dil):
    nsub = BAND_NSUB if dil == 1 else 1
    rows = BLK * dil * nsub
    nchunk = SEQ // rows
    has_prev = SEQ // dil > BLK
    pair = LANES // HEAD_DIM if dil > 1 else D_HEADS_PER
    npair = D_HEADS_PER // pair
    width = pair * HEAD_DIM
    row = lambda b, i: b * nchunk + i
    prev_rows = rows // nsub
    prev = lambda b, i: (b * nchunk + i) * nsub - jnp.minimum(i, 1)
    cur_spec = pl.BlockSpec((rows, width), lambda b, i, p: (row(b, i), g * npair + p))
    prev_spec = pl.BlockSpec((prev_rows, width), lambda b, i, p: (prev(b, i), g * npair + p))
    out_spec = pl.BlockSpec((rows, width), lambda b, i, p: (row(b, i), p))
    bm_spec = lambda pick: pl.BlockSpec((1, pair, BLK, 2 * BLK), lambda b, i, p: (pick(i), p, 0, 0))
    bm_specs, bm_args = [bm_spec(lambda i: jnp.minimum(i, 1))], [bm]
    if nsub > 1:
        bm_specs.append(bm_spec(lambda i: 1))
        bm_args.append(bm)
    if has_prev:
        in_specs, args = [cur_spec, cur_spec, prev_spec, cur_spec, prev_spec] + bm_specs, (q, k, k, v, v, *bm_args)
    else:
        in_specs, args = [cur_spec, cur_spec, cur_spec] + bm_specs, (q, k, v, *bm_args)
    return pl.pallas_call(
        functools.partial(_band_attn_kernel, n_heads=pair, gqa=1, with_sink=False, dil=dil,
                          has_prev=has_prev, nsub=nsub),
        grid=(BATCH, nchunk, npair),
        in_specs=in_specs,
        out_specs=[out_spec, out_spec],
        out_shape=[jax.ShapeDtypeStruct((NP_TOK, D_GROUP_W), F32)] * 2,
        compiler_params=_cparams("parallel", "parallel", "parallel"),
        name=f"dilated_prompt_g{g}",
    )(*args)


SWA_SAMPLE_BB = 8


def _pad_rows(x, rows):
    return jnp.concatenate([x, jnp.zeros((rows - x.shape[0], x.shape[1]), x.dtype)], axis=0)


def _swa_sample_kernel(sink_ref, q_ref, kv_ref, c_ref, bmc_ref, bmn_ref, o_ref):
    for bb in range(SWA_SAMPLE_BB):
        rs = slice(bb * DEC_SEQ, (bb + 1) * DEC_SEQ)
        q = (q_ref[rs, :] * HEAD_DIM ** -0.5).astype(BF16)
        kvn = _pad_rows(kv_ref[rs, :], BLK).astype(BF16)
        cache = c_ref[0, bb].astype(BF16)
        outs = [None] * A_Q_HEADS
        for hk in range(A_KV_HEADS):
            heads = range(hk * A_GQA, (hk + 1) * A_GQA)
            qs = jnp.concatenate([q[:, h * HEAD_DIM:(h + 1) * HEAD_DIM] for h in heads], axis=0)
            ksl = slice(hk * HEAD_DIM, (hk + 1) * HEAD_DIM)
            vsl = slice(A_KV_W + hk * HEAD_DIM, A_KV_W + (hk + 1) * HEAD_DIM)
            brow = slice(hk * A_GQA * DEC_SEQ, (hk + 1) * A_GQA * DEC_SEQ)
            s1 = jnp.dot(qs, cache[ksl, :], preferred_element_type=F32) + bmc_ref[brow, :]
            s2 = _nt_dot(qs, kvn[:, ksl]) + bmn_ref[brow, :]
            sink = jnp.concatenate(
                [jnp.full((DEC_SEQ, 1), sink_ref[h], F32) for h in heads], axis=0)
            (p1, p2), den, _ = _softmax_parts([s1, s2], sink)
            o = _nt_dot(p1, cache[vsl, :]) + jnp.dot(p2, kvn[:, vsl], preferred_element_type=F32)
            for j, h in enumerate(heads):
                outs[h] = o[j * DEC_SEQ:(j + 1) * DEC_SEQ, :]
        o_ref[rs, :] = jnp.concatenate(outs, axis=1)


def _swa_sample(q, kv, cache_all, layer, sinks, bmc, bmn):
    bb = SWA_SAMPLE_BB
    return pl.pallas_call(
        _swa_sample_kernel,
        grid=(DEC_BATCH // bb,),
        in_specs=[pl.BlockSpec(memory_space=pltpu.SMEM),
                  pl.BlockSpec((bb * DEC_SEQ, A_Q_W), lambda i: (i, 0)),
                  pl.BlockSpec((bb * DEC_SEQ, 2 * A_KV_W), lambda i: (i, 0)),
                  pl.BlockSpec((1, bb, 2 * A_KV_W, A_WINDOW), lambda i: (layer, i, 0, 0)),
                  pl.BlockSpec(bmc.shape, lambda i: (0, 0)),
                  pl.BlockSpec(bmn.shape, lambda i: (0, 0))],
        out_specs=pl.BlockSpec((bb * DEC_SEQ, A_Q_W), lambda i: (i, 0)),
        out_shape=jax.ShapeDtypeStruct((NS_TOK, A_Q_W), F32),
        compiler_params=_cparams("parallel"),
        name="swa_sample",
    )(sinks, q, kv, cache_all, bmc, bmn)


def _dil_sample_kernel(q_ref, k_ref, v_ref, c0_ref, c1_ref, c2_ref,
                       bc0_ref, bc1_ref, bc2_ref, bn_ref,
                       o0_ref, o1_ref, o2_ref, l0_ref, l1_ref, l2_ref):
    q = q_ref[...] * HEAD_DIM ** -0.5
    k = k_ref[...]
    v = v_ref[...]
    nrow = D_HEADS_PER * DEC_SEQ
    row_head = lax.broadcasted_iota(jnp.int32, (nrow, D_GROUP_W), 0) // DEC_SEQ
    lane_head = lax.broadcasted_iota(jnp.int32, (nrow, D_GROUP_W), 1) // HEAD_DIM
    head_mask = row_head == lane_head
    out_lane_head = lax.broadcasted_iota(jnp.int32, (DEC_SEQ, D_GROUP_W), 1) // HEAD_DIM
    groups = ((c0_ref, bc0_ref, o0_ref, l0_ref), (c1_ref, bc1_ref, o1_ref, l1_ref),
              (c2_ref, bc2_ref, o2_ref, l2_ref))
    for g, (c_ref, bc_ref, o_ref, l_ref) in enumerate(groups):
        gsl = slice(g * D_GROUP_W, (g + 1) * D_GROUP_W)
        qbd = jnp.where(head_mask, jnp.concatenate([q[:, gsl]] * D_HEADS_PER, axis=0), 0.0).astype(BF16)
        kn = _pad_rows(k[:, gsl], BLK).astype(BF16)
        vn = _pad_rows(v[:, gsl], BLK).astype(BF16)
        kt = c_ref[0, 0, 0:D_GROUP_W, :].astype(BF16)
        vt = c_ref[0, 0, D_GROUP_W:2 * D_GROUP_W, :].astype(BF16)
        s1 = jnp.dot(qbd, kt, preferred_element_type=F32) + bc_ref[0]
        s2 = _nt_dot(qbd, kn) + bn_ref[g]
        (p1, p2), den, m = _softmax_parts([s1, s2], None)
        of = _nt_dot(p1, vt) + jnp.dot(p2, vn, preferred_element_type=F32)
        lse = m + jnp.log(den)
        og = jnp.zeros((DEC_SEQ, D_GROUP_W), F32)
        lg = jnp.zeros((DEC_SEQ, D_GROUP_W), F32)
        for h in range(D_HEADS_PER):
            rs = slice(h * DEC_SEQ, (h + 1) * DEC_SEQ)
            sel = out_lane_head == h
            og = og + jnp.where(sel, of[rs, :], 0.0)
            lg = lg + jnp.where(sel, lse[rs, :], 0.0)
        o_ref[...] = og
        l_ref[...] = lg


def _shift_block(c_ref, n_ref, o_ref, i):
    width, wc = c_ref.shape[2], c_ref.shape[3]
    lane = lax.broadcasted_iota(jnp.int32, (width, BLK), 1)
    shifted = pltpu.roll(c_ref[0, i], wc - DEC_SEQ, 1)
    new_t = jnp.concatenate([jnp.zeros((BLK - DEC_SEQ, width), F32), n_ref[0, i]], axis=0).T
    if wc > BLK:
        o_ref[0, i, :, 0:wc - BLK] = shifted[:, 0:wc - BLK]
    o_ref[0, i, :, wc - BLK:wc] = jnp.where(lane >= BLK - DEC_SEQ, new_t, shifted[:, wc - BLK:wc])


def _shift_kernel(c_ref, n_ref, o_ref, *, bb):
    for i in range(bb):
        _shift_block(c_ref, n_ref, o_ref, i)


def _dilated_sample(q, k, v, caches, layer, bcs, bn):
    tok_spec = pl.BlockSpec((DEC_SEQ, D_QKV_W), lambda b: (b, 0))
    out_spec = pl.BlockSpec((DEC_SEQ, D_GROUP_W), lambda b: (b, 0))
    outs = pl.pallas_call(
        _dil_sample_kernel,
        grid=(DEC_BATCH,),
        in_specs=[tok_spec, tok_spec, tok_spec]
        + [pl.BlockSpec((1, 1, 2 * D_GROUP_W, win), lambda b: (layer, b, 0, 0)) for win, _ in D_PAIRS]
        + [pl.BlockSpec(bc.shape, lambda b: (0, 0, 0)) for bc in bcs]
        + [pl.BlockSpec(bn.shape, lambda b: (0, 0, 0))],
        out_specs=[out_spec] * 6,
        out_shape=[jax.ShapeDtypeStruct((NS_TOK, D_GROUP_W), F32)] * 6,
        compiler_params=_cparams("parallel"),
        name="dilated_sample",
    )(q, k, v, *caches, *bcs, bn)
    return outs[:3], outs[3:]


def _dil_sample_shift_kernel(q_ref, k_ref, v_ref, c0_ref, c1_ref, c2_ref, n0_ref, n1_ref, n2_ref,
                             bc0_ref, bc1_ref, bc2_ref, bn_ref,
                             o0_ref, o1_ref, o2_ref, l0_ref, l1_ref, l2_ref, s0_ref, s1_ref, s2_ref):
    for c_ref, n_ref, s_ref in ((c0_ref, n0_ref, s0_ref), (c1_ref, n1_ref, s1_ref), (c2_ref, n2_ref, s2_ref)):
        _shift_block(c_ref, n_ref, s_ref, 0)

    @pl.when(pl.program_id(0) == pl.num_programs(0) - 1)
    def _():
        _dil_sample_kernel(q_ref, k_ref, v_ref, c0_ref, c1_ref, c2_ref, bc0_ref, bc1_ref, bc2_ref, bn_ref,
                           o0_ref, o1_ref, o2_ref, l0_ref, l1_ref, l2_ref)


def _dilated_sample_and_shift(q, k, v, caches, new_rows, bcs, bn):
    n_layers = caches[0].shape[0]
    last = n_layers - 1
    tok_row = lambda l, b: jnp.where(l == last, b, 0)
    tok_spec = pl.BlockSpec((DEC_SEQ, D_QKV_W), lambda l, b: (tok_row(l, b), 0))
    out_spec = pl.BlockSpec((DEC_SEQ, D_GROUP_W), lambda l, b: (tok_row(l, b), 0))
    cache_specs = [pl.BlockSpec((1, 1, 2 * D_GROUP_W, win), lambda l, b: (l, b, 0, 0)) for win, _ in D_PAIRS]
    new_spec = pl.BlockSpec((1, 1, DEC_SEQ, 2 * D_GROUP_W), lambda l, b: (l, b, 0, 0))
    outs = pl.pallas_call(
        _dil_sample_shift_kernel,
        grid=(n_layers, DEC_BATCH),
        in_specs=[tok_spec, tok_spec, tok_spec] + cache_specs + [new_spec] * 3
        + [pl.BlockSpec(bc.shape, lambda l, b: (0, 0, 0)) for bc in bcs]
        + [pl.BlockSpec(bn.shape, lambda l, b: (0, 0, 0))],
        out_specs=[out_spec] * 6 + cache_specs,
        out_shape=[jax.ShapeDtypeStruct((NS_TOK, D_GROUP_W), F32)] * 6
        + [jax.ShapeDtypeStruct(c.shape, c.dtype) for c in caches],
        compiler_params=_cparams("arbitrary", "arbitrary"),
        name="dilated_sample_shift",
    )(q, k, v, *caches, *new_rows, *bcs, bn)
    return outs[:3], outs[3:6], outs[6:]


def _shift_cache(cache, new_rows, name):
    n_layers, _, width, wc = cache.shape
    bb = max(1, (4 * 2 ** 20) // (wc * width * 4))
    return pl.pallas_call(
        functools.partial(_shift_kernel, bb=bb),
        grid=(n_layers, DEC_BATCH // bb),
        in_specs=[pl.BlockSpec((1, bb, width, wc), lambda l, i: (l, i, 0, 0)),
                  pl.BlockSpec((1, bb, DEC_SEQ, width), lambda l, i: (l, i, 0, 0))],
        out_specs=pl.BlockSpec((1, bb, width, wc), lambda l, i: (l, i, 0, 0)),
        out_shape=jax.ShapeDtypeStruct(cache.shape, cache.dtype),
        compiler_params=_cparams("parallel", "parallel"),
        name=name,
    )(cache, new_rows)


def _to_time_major(src_ref, dst_ref, off, bt, lc):
    for b in range(bt):
        x = src_ref[b]
        for j in range(dst_ref.shape[0]):
            dst_ref[j, pl.ds(off + b, lc, stride=bt), :] = x[:, j * LANES:(j + 1) * LANES]


def _from_time_major(src_ref, dst_ref, bt, lc):
    for b in range(bt):
        dst_ref[b] = jnp.concatenate(
            [src_ref[j, pl.ds(b, lc, stride=bt), :] for j in range(src_ref.shape[0])], axis=1)


def _lane_blocks(ref, r0, nrows):
    return jnp.concatenate([ref[j, r0:r0 + nrows, :] for j in range(ref.shape[0])], axis=1)


def _store_lane_blocks(ref, r0, x):
    for j in range(ref.shape[0]):
        ref[j, r0:r0 + x.shape[0], :] = x[:, j * LANES:(j + 1) * LANES]


def _token_spec(bt, lc, width):
    return pl.BlockSpec((bt, lc, width), lambda c: (0, c, 0))


def _s5_kernel(u_ref, h0_ref, bm_ref, cm_ref, lam_ref, d_ref, gw_ref, gb_ref, o_ref, hl_ref,
               us_ref, hs_ref, *, bt, lc):
    rows = lc * bt
    blk_w = 2 * B_HALF

    @pl.when(pl.program_id(0) == 0)
    def _():
        hs_ref[0:bt, :] = h0_ref[...]

    _to_time_major(u_ref, us_ref, 0, bt, lc)
    u = _lane_blocks(us_ref, 0, rows)
    ub = u.astype(BF16)
    for j in range(B_LANE_BLOCKS):
        cols = jnp.dot(ub[:, j * LANES:(j + 1) * LANES], bm_ref[j], preferred_element_type=F32)
        hs_ref[bt:, j * blk_w:j * blk_w + B_HALF] = cols[:, :B_HALF]
        hs_ref[bt:, j * blk_w + B_HALF:(j + 1) * blk_w] = cols[:, :B_HALF] + cols[:, B_HALF:]

    def step(t, carry):
        r0 = pl.multiple_of(t * bt, bt)
        for j in range(B_LANE_BLOCKS):
            re = slice(j * blk_w, j * blk_w + B_HALF)
            im = slice(j * blk_w + B_HALF, (j + 1) * blk_w)
            lr = lam_ref[2 * j:2 * j + 1, :]
            li = lam_ref[2 * j + 1:2 * j + 2, :]
            pr = hs_ref[pl.ds(r0, bt), re]
            pi = hs_ref[pl.ds(r0, bt), im]
            hs_ref[pl.ds(r0 + bt, bt), re] = lr * pr - li * pi + hs_ref[pl.ds(r0 + bt, bt), re]
            hs_ref[pl.ds(r0 + bt, bt), im] = lr * pi + li * pr + hs_ref[pl.ds(r0 + bt, bt), im]
        return carry

    lax.fori_loop(0, lc, step, 0)

    ys = []
    for j in range(B_LANE_BLOCKS):
        h_re = hs_ref[bt:, j * blk_w:j * blk_w + B_HALF]
        h_im = hs_ref[bt:, j * blk_w + B_HALF:(j + 1) * blk_w]
        lhs = jnp.concatenate([h_re + h_im, h_im], axis=1).astype(BF16)
        ys.append(jnp.dot(lhs, cm_ref[j], preferred_element_type=F32))
    y = jax.nn.gelu(jnp.concatenate(ys, axis=1) + d_ref[...] * u)
    z = jnp.dot(y.astype(BF16), gw_ref[...], preferred_element_type=F32) + gb_ref[...]
    _store_lane_blocks(us_ref, 0, y * jax.nn.sigmoid(z))
    _from_time_major(us_ref, o_ref, bt, lc)
    last = hs_ref[rows:rows + bt, :]
    hl_ref[...] = last
    hs_ref[0:bt, :] = last


def _s5_mixer(u, h0, prm, bt, seq, lc, name):
    tok = _token_spec(bt, lc, B_WIDTH)
    u_in = u.reshape(bt, seq, B_WIDTH)
    full = lambda a: pl.BlockSpec(a.shape, lambda c: (0,) * a.ndim)
    out, hl = pl.pallas_call(
        functools.partial(_s5_kernel, bt=bt, lc=lc),
        grid=(seq // lc,),
        in_specs=[tok, full(h0), full(prm["bm"]), full(prm["cm"]), full(prm["lam"]), full(prm["d"]),
                  full(prm["glu_w"]), full(prm["glu_b"])],
        out_specs=[tok, pl.BlockSpec((bt, B_STATE_W), lambda c: (0, 0))],
        out_shape=[jax.ShapeDtypeStruct(u_in.shape, F32), jax.ShapeDtypeStruct((bt, B_STATE_W), F32)],
        scratch_shapes=[pltpu.VMEM((B_WIDTH // LANES, lc * bt, LANES), F32),
                        pltpu.VMEM((lc * bt + bt, B_STATE_W), F32)],
        compiler_params=_cparams("arbitrary"),
        name=name,
    )(u_in, h0, prm["bm"], prm["cm"], prm["lam"], prm["d"], prm["glu_w"], prm["glu_b"])
    return out.reshape(bt * seq, B_WIDTH), hl


def _s5_params(a_re, a_im, log_dt, b_re, b_im, c_re, c_im, d, glu_w, glu_b):
    lam = lax.complex(a_re.astype(F32), a_im.astype(F32))
    dt = jnp.exp(log_dt.astype(F32))[:, None]
    lam_bar = jnp.exp(lam * dt)
    b_bar = ((lam_bar - 1.0) / lam)[..., None] * lax.complex(b_re.astype(F32), b_im.astype(F32))
    nb, gb = B_LANE_BLOCKS, B_GROUPS_PER_BLOCK
    eye = jnp.eye(gb, dtype=F32)

    def in_mat(part):
        p = part.reshape(nb, gb, B_STATE, B_GROUP_CH)
        return jnp.einsum("jgnc,gh->jgchn", p, eye).reshape(nb, gb * B_GROUP_CH, gb * B_STATE)

    def out_mat(part):
        p = part.reshape(nb, gb, B_GROUP_CH, B_STATE)
        return jnp.einsum("jgcn,gh->jgnhc", p, eye).reshape(nb, gb * B_STATE, gb * B_GROUP_CH)

    bm = jnp.concatenate([in_mat(b_bar.real), in_mat(b_bar.imag - b_bar.real)], axis=2).astype(BF16)
    c_re32, c_im32 = c_re.astype(F32), c_im.astype(F32)
    cm = jnp.concatenate([out_mat(c_re32), out_mat(-(c_re32 + c_im32))], axis=1).astype(BF16)
    lam_rows = jnp.stack([lam_bar.real.reshape(nb, B_HALF), lam_bar.imag.reshape(nb, B_HALF)],
                         axis=1).reshape(2 * nb, B_HALF)
    return {"bm": bm, "cm": cm, "lam": lam_rows, "d": d.astype(F32).reshape(1, B_WIDTH),
            "glu_w": glu_w.astype(BF16), "glu_b": glu_b.astype(F32).reshape(1, B_WIDTH)}


def _s5_state_to_cols(state):
    bt = state.shape[0]
    s = state.astype(F32).reshape(bt, B_LANE_BLOCKS, B_GROUPS_PER_BLOCK, B_STATE, 2)
    return s.transpose(0, 1, 4, 2, 3).reshape(bt, B_STATE_W)


def _s5_cols_to_state(cols):
    bt = cols.shape[0]
    s = cols.reshape(bt, B_LANE_BLOCKS, 2, B_GROUPS_PER_BLOCK, B_STATE)
    return s.transpose(0, 1, 3, 4, 2).reshape(bt, B_GROUPS, B_STATE, 2)


def _rglru_kernel(xr_ref, gate_ref, cb_ref, h0_ref, cw_ref, cbias_ref, wa_ref, ba_ref, wx_ref, bx_ref,
                  nsp_ref, o_ref, hl_ref, xp_ref, gs_ref, a_ref, hs_ref, *, bt, lc):
    rows = lc * bt
    pad = (C_CONV - 1) * bt

    @pl.when(pl.program_id(0) == 0)
    def _():
        _store_lane_blocks(xp_ref, 0, cb_ref[...])
        hs_ref[0:bt, :] = h0_ref[...]

    _to_time_major(xr_ref, xp_ref, pad, bt, lc)
    _to_time_major(gate_ref, gs_ref, 0, bt, lc)
    xc = _lane_blocks(xp_ref, 0, rows) * cw_ref[0:1, :]
    for tap in range(1, C_CONV):
        xc = xc + _lane_blocks(xp_ref, tap * bt, rows) * cw_ref[tap:tap + 1, :]
    xcf = xc + cbias_ref[...]
    xb = xcf.astype(BF16)
    r = jax.nn.sigmoid(jnp.dot(xb, wa_ref[...], preferred_element_type=F32) + ba_ref[...])
    i = jax.nn.sigmoid(jnp.dot(xb, wx_ref[...], preferred_element_type=F32) + bx_ref[...])
    log_a = nsp_ref[...] * r
    a = jnp.exp(log_a)
    a_ref[...] = a
    hs_ref[bt:, :] = jnp.sqrt(-jnp.tanh(log_a) * (a * a + 1.0)) * (i * xcf)

    def step(t, carry):
        r0 = pl.multiple_of(t * bt, bt)
        hs_ref[pl.ds(r0 + bt, bt), :] = (a_ref[pl.ds(r0, bt), :] * hs_ref[pl.ds(r0, bt), :]
                                         + hs_ref[pl.ds(r0 + bt, bt), :])
        return carry

    lax.fori_loop(0, lc, step, 0)

    _store_lane_blocks(gs_ref, 0, hs_ref[bt:, :] * jax.nn.gelu(_lane_blocks(gs_ref, 0, rows)))
    _from_time_major(gs_ref, o_ref, bt, lc)
    last = hs_ref[rows:rows + bt, :]
    hl_ref[...] = last
    hs_ref[0:bt, :] = last
    _store_lane_blocks(xp_ref, 0, _lane_blocks(xp_ref, rows, pad))


def _rglru_mixer(xr, gate, conv_buf, h0, prm, bt, seq, lc, name):
    rows = lc * bt
    tok = _token_spec(bt, lc, C_WIDTH)
    shape3 = lambda x: x.reshape(bt, seq, C_WIDTH)
    full = lambda a: pl.BlockSpec(a.shape, lambda c: (0,) * a.ndim)
    names = ("conv_w", "conv_b", "wa", "ba", "wx", "bx", "nsp")
    out, hl = pl.pallas_call(
        functools.partial(_rglru_kernel, bt=bt, lc=lc),
        grid=(seq // lc,),
        in_specs=[tok, tok, full(conv_buf), full(h0)] + [full(prm[k]) for k in names],
        out_specs=[tok, pl.BlockSpec((bt, C_WIDTH), lambda c: (0, 0))],
        out_shape=[jax.ShapeDtypeStruct(shape3(xr).shape, F32), jax.ShapeDtypeStruct((bt, C_WIDTH), F32)],
        scratch_shapes=[pltpu.VMEM((C_WIDTH // LANES, rows + (C_CONV - 1) * bt, LANES), F32),
                        pltpu.VMEM((C_WIDTH // LANES, rows, LANES), F32),
                        pltpu.VMEM((rows, C_WIDTH), F32),
                        pltpu.VMEM((rows + bt, C_WIDTH), F32)],
        compiler_params=_cparams("arbitrary"),
        name=name,
    )(shape3(xr), shape3(gate), conv_buf, h0, *[prm[k] for k in names])
    return out.reshape(bt * seq, C_WIDTH), hl


def _rglru_params(conv_w, conv_b, gate_a_w, gate_a_b, gate_x_w, gate_x_b, lru_lambda):
    eye = jnp.eye(C_BLOCKS, dtype=F32)

    def block_diag(w):
        return jnp.einsum("njk,nm->njmk", w.astype(F32), eye).reshape(C_WIDTH, C_WIDTH).astype(BF16)

    row = lambda x: x.astype(F32).reshape(1, C_WIDTH)
    return {"conv_w": conv_w.astype(F32), "conv_b": row(conv_b),
            "wa": block_diag(gate_a_w), "ba": row(gate_a_b),
            "wx": block_diag(gate_x_w), "bx": row(gate_x_b),
            "nsp": row(-C_POWER * jax.nn.softplus(-lru_lambda.astype(F32)))}


def _residual_out(x_ref, acc, tail_refs):
    *route_refs, o_ref = tail_refs
    x1 = x_ref[...] + acc
    if not route_refs:
        o_ref[...] = x1
        return
    g_ref, rw_ref, rb_ref = route_refs
    o_ref[:, 0:D_MODEL] = x1
    o_ref[:, D_MODEL:] = _route_lanes(x1, g_ref[...], rw_ref[...], rb_ref[...])


def _out_proj_even_kernel(x_ref, oa_ref, ob_ref, w_ref, *tail_refs):
    acc = jnp.dot(oa_ref[...].astype(BF16), w_ref[0:A_Q_W, :], preferred_element_type=F32)
    acc = acc + jnp.dot(ob_ref[...].astype(BF16), w_ref[A_Q_W:, :], preferred_element_type=F32)
    _residual_out(x_ref, acc, tail_refs)


def _out_proj_odd_kernel(x_ref, oc_ref, o0_ref, o1_ref, o2_ref, l0_ref, l1_ref, l2_ref, w_ref, *tail_refs):
    l0, l1, l2 = l0_ref[...], l1_ref[...], l2_ref[...]
    m = jnp.maximum(jnp.maximum(l0, l1), l2)
    e0, e1, e2 = jnp.exp(l0 - m), jnp.exp(l1 - m), jnp.exp(l2 - m)
    od = (o0_ref[...] * e0 + o1_ref[...] * e1 + o2_ref[...] * e2) / (e0 + e1 + e2)
    acc = jnp.dot(oc_ref[...].astype(BF16), w_ref[0:C_WIDTH, :], preferred_element_type=F32)
    acc = acc + jnp.dot(od.astype(BF16), w_ref[C_WIDTH:, :], preferred_element_type=F32)
    _residual_out(x_ref, acc, tail_refs)


def _out_proj(kernel, x, parts, w, name, in_place=True, route_prm=None):
    n = x.shape[0]
    tm = 512
    in_specs = ([pl.BlockSpec((tm, D_MODEL), lambda i: (i, 0))]
                + [pl.BlockSpec((tm, p.shape[1]), lambda i: (i, 0)) for p in parts]
                + [pl.BlockSpec(w.shape, lambda i: (0, 0))])
    args = [x, *parts, w]
    width = D_MODEL
    if route_prm is not None:
        for key in ("g", "rw", "rb"):
            in_specs.append(pl.BlockSpec(route_prm[key].shape, lambda i: (0, 0)))
            args.append(route_prm[key])
        width = MOE_AUG_W
    return pl.pallas_call(
        kernel,
        grid=(n // tm,),
        in_specs=in_specs,
        out_specs=pl.BlockSpec((tm, width), lambda i: (i, 0)),
        out_shape=jax.ShapeDtypeStruct((n, width), F32),
        input_output_aliases={0: 0} if in_place and route_prm is None else {},
        compiler_params=_cparams("parallel"),
        name=name,
    )(*args)


MOE_TM = 512


def _ffn_norm(x, g):
    y = x * lax.rsqrt(jnp.mean(x * x, axis=-1, keepdims=True) + RMS_EPS)
    return (y * g).astype(BF16)


def _route(xb, rw, rb):
    logits = jnp.dot(xb, rw, preferred_element_type=F32) + rb
    lane = lax.broadcasted_iota(jnp.int32, logits.shape, 1).astype(F32)
    ninf = float("-inf")
    far = float(ROUTER_LANES)
    lg = jnp.where(lane < MOE_GROUPS, logits, ninf)
    gmax = jnp.max(lg, axis=1, keepdims=True)
    g_idx = jnp.min(jnp.where(lg == gmax, lane, far), axis=1, keepdims=True)
    g_w = 1.0 / jnp.sum(jnp.exp(lg - gmax), axis=1, keepdims=True)
    lane_grp = jnp.floor((lane - MOE_GROUPS) * (1.0 / MOE_PER_GROUP))
    in_grp = (lane >= MOE_GROUPS) & (lane < MOE_GROUPS + MOE_EXPERTS) & (lane_grp == g_idx)
    le = jnp.where(in_grp, logits, ninf)
    v1 = jnp.max(le, axis=1, keepdims=True)
    i1 = jnp.min(jnp.where(le == v1, lane, far), axis=1, keepdims=True)
    le2 = jnp.where(lane == i1, ninf, le)
    v2 = jnp.max(le2, axis=1, keepdims=True)
    i2 = jnp.min(jnp.where(le2 == v2, lane, far), axis=1, keepdims=True)
    e2 = jnp.exp(v2 - v1)
    w1 = g_w / (1.0 + e2)
    w2 = g_w * e2 / (1.0 + e2)
    return lane, g_idx, i1 - MOE_GROUPS, i2 - MOE_GROUPS, w1, w2


def _moe_kernel(x_ref, g_ref, rw_ref, rb_ref, wg_ref, wu_ref, wd_ref, o_ref, xn_ref, gates_ref, acc_ref):
    e = pl.program_id(1)

    @pl.when(e == 0)
    def _():
        xb0 = _ffn_norm(x_ref[...], g_ref[...])
        xn_ref[...] = xb0
        lane, _, e1, e2, w1, w2 = _route(xb0, rw_ref[...], rb_ref[...])
        gates_ref[...] = jnp.where(lane == e1, w1, 0.0) + jnp.where(lane == e2, w2, 0.0)
        acc_ref[...] = jnp.zeros_like(acc_ref)

    xb = xn_ref[...]
    gates = gates_ref[...]
    lane_i = lax.broadcasted_iota(jnp.int32, gates.shape, 1)
    gate = jnp.sum(jnp.where(lane_i == e, gates, 0.0), axis=1, keepdims=True)
    hg = jnp.dot(xb, wg_ref[0], preferred_element_type=F32)
    hu = jnp.dot(xb, wu_ref[0], preferred_element_type=F32)
    hid = jax.nn.silu(hg) * hu * gate
    acc_ref[...] += jnp.dot(hid.astype(BF16), wd_ref[0], preferred_element_type=F32)

    @pl.when(e == MOE_EXPERTS - 1)
    def _():
        o_ref[...] = x_ref[...] + acc_ref[...]


def _moe(x, prm, name):
    n = x.shape[0]
    tm = 2 * MOE_TM if n % (2 * MOE_TM) == 0 else MOE_TM
    return pl.pallas_call(
        _moe_kernel,
        grid=(n // tm, MOE_EXPERTS),
        in_specs=[pl.BlockSpec((tm, D_MODEL), lambda i, e: (i, 0)),
                  pl.BlockSpec((1, D_MODEL), lambda i, e: (0, 0)),
                  pl.BlockSpec((D_MODEL, ROUTER_LANES), lambda i, e: (0, 0)),
                  pl.BlockSpec((1, ROUTER_LANES), lambda i, e: (0, 0)),
                  pl.BlockSpec((1, D_MODEL, MOE_FF), lambda i, e: (e, 0, 0)),
                  pl.BlockSpec((1, D_MODEL, MOE_FF), lambda i, e: (e, 0, 0)),
                  pl.BlockSpec((1, MOE_FF, D_MODEL), lambda i, e: (e, 0, 0))],
        out_specs=pl.BlockSpec((tm, D_MODEL), lambda i, e: (i, 0)),
        out_shape=jax.ShapeDtypeStruct((n, D_MODEL), F32),
        scratch_shapes=[pltpu.VMEM((tm, D_MODEL), BF16),
                        pltpu.VMEM((tm, ROUTER_LANES), F32),
                        pltpu.VMEM((tm, D_MODEL), F32)],
        input_output_aliases={0: 0},
        compiler_params=_cparams("parallel", "arbitrary"),
        name=name,
    )(x, prm["g"], prm["rw"], prm["rb"], prm["wg"], prm["wu"], prm["wd"])


def _moe_params(norm_g, router_g, router_g_b, router_e, router_e_b, w_gate, w_up, w_down):
    used = MOE_GROUPS + MOE_EXPERTS
    rw = jnp.concatenate([router_g.astype(F32), router_e.astype(F32),
                          jnp.zeros((D_MODEL, ROUTER_LANES - used), F32)], axis=1)
    rb = jnp.concatenate([router_g_b.astype(F32), router_e_b.astype(F32),
                          jnp.zeros((ROUTER_LANES - used,), F32)]).reshape(1, ROUTER_LANES)
    return {"g": norm_g.astype(F32).reshape(1, D_MODEL), "rw": rw.astype(BF16), "rb": rb,
            "wg": w_gate.astype(BF16), "wu": w_up.astype(BF16), "wd": w_down.astype(BF16)}


MOE_AUG_W = D_MODEL + ROUTER_LANES
GATE_LANE0 = 1


def _route_lanes(x, g, rw, rb):
    lane, g_idx, e1, e2, w1, w2 = _route(_ffn_norm(x, g), rw, rb)
    l1 = e1 - g_idx * MOE_PER_GROUP + GATE_LANE0
    l2 = e2 - g_idx * MOE_PER_GROUP + GATE_LANE0
    return jnp.where(lane == 0.0, g_idx, 0.0) + jnp.where(lane == l1, w1, 0.0) + jnp.where(lane == l2, w2, 0.0)


def _moe_plan(group_col, n):
    tm = MOE_TM
    n_tiles_max = n // tm + MOE_GROUPS
    g = group_col.astype(jnp.int32)
    onehot = (g[:, None] == jnp.arange(MOE_GROUPS)[None, :]).astype(jnp.int32)
    counts = jnp.sum(onehot, axis=0)
    rank = jnp.sum((jnp.cumsum(onehot, axis=0) - onehot) * onehot, axis=1)
    padded = (counts + tm - 1) // tm * tm
    ends = jnp.cumsum(padded)
    starts = ends - padded
    pos = jnp.sum(onehot * starts[None, :], axis=1) + rank
    src = jnp.zeros((n_tiles_max * tm,), jnp.int32).at[pos].set(jnp.arange(n, dtype=jnp.int32))
    tile0 = jnp.arange(n_tiles_max, dtype=jnp.int32) * tm
    tile_group = jnp.minimum(jnp.sum((tile0[:, None] >= ends[None, :]).astype(jnp.int32), axis=1),
                             MOE_GROUPS - 1)
    real = jnp.clip(counts[tile_group] - (tile0 - starts[tile_group]), 0, tm)
    real = jnp.where(tile0 < ends[-1], real, 0).astype(jnp.int32)
    return src, tile_group.astype(jnp.int32), real, (ends[-1:] // tm).astype(jnp.int32)


def _moe_expert_kernel(src_ref, grp_ref, real_ref, nt_ref, xa_hbm, g_ref, wg_ref, wu_ref, wd_ref, out_hbm,
                       xbuf, ybuf, gsem, ssem):
    del grp_ref
    tm = MOE_TM
    i = pl.program_id(0)
    n_tiles = nt_ref[0]
    slot = lax.rem(i, 2)
    sub = SUBLANES

    def row_in(tile, s, j, u):
        row = src_ref[tile * tm + j * sub + u]
        return pltpu.make_async_copy(xa_hbm.at[row >> 3, pl.ds(row & 7, 1)],
                                     xbuf.at[s, j, pl.ds(u, 1)], gsem.at[s])

    def row_out(tile, s, j, u):
        row = src_ref[tile * tm + j * sub + u]
        return pltpu.make_async_copy(ybuf.at[s, j, pl.ds(u, 1)],
                                     out_hbm.at[row >> 3, pl.ds(row & 7, 1)], ssem.at[s])

    def start_gather(tile, s):
        def body(j, c):
            for u in range(sub):
                row_in(tile, s, j, u).start()
            return c
        lax.fori_loop(0, tm // sub, body, 0)

    def wait_gather(s):
        pltpu.make_async_copy(xa_hbm.at[pl.ds(0, tm // sub)], xbuf.at[s], gsem.at[s]).wait()

    def start_scatter(tile, s):
        real = real_ref[tile]
        full = real // sub

        def body(j, c):
            for u in range(sub):
                row_out(tile, s, j, u).start()
            return c
        lax.fori_loop(0, full, body, 0)

        def tail(r, c):
            row_out(tile, s, full, r - full * sub).start()
            return c
        lax.fori_loop(full * sub, real, tail, 0)

    def wait_scatter(tile, s):
        real = real_ref[tile]
        for k in range(tm.bit_length()):
            size = 1 << k

            @pl.when((real >> k) & 1 == 1)
            def _():
                if size >= sub:
                    piece = (ybuf.at[s, pl.ds(0, size // sub)], out_hbm.at[pl.ds(0, size // sub)])
                else:
                    piece = (ybuf.at[s, 0, pl.ds(0, size)], out_hbm.at[0, pl.ds(0, size)])
                pltpu.make_async_copy(piece[0], piece[1], ssem.at[s]).wait()

    @pl.when(i < n_tiles)
    def _():
        @pl.when(i == 0)
        def _():
            start_gather(0, 0)

        @pl.when(i + 1 < n_tiles)
        def _():
            start_gather(i + 1, 1 - slot)

        wait_gather(slot)

        @pl.when(i >= 2)
        def _():
            wait_scatter(i - 2, slot)

        xa = xbuf[slot].reshape(tm, MOE_AUG_W)
        x = xa[:, 0:D_MODEL]
        route = xa[:, D_MODEL:]
        xb = _ffn_norm(x, g_ref[...])
        acc = jnp.zeros((tm, D_MODEL), F32)
        for e in range(MOE_PER_GROUP):
            gate = route[:, GATE_LANE0 + e:GATE_LANE0 + e + 1]
            hg = jnp.dot(xb, wg_ref[0, e], preferred_element_type=F32)
            hu = jnp.dot(xb, wu_ref[0, e], preferred_element_type=F32)
            hid = jax.nn.silu(hg) * hu * gate
            acc = acc + jnp.dot(hid.astype(BF16), wd_ref[0, e], preferred_element_type=F32)
        ybuf[slot] = (x + acc).reshape(tm // sub, sub, D_MODEL)
        start_scatter(i, slot)

        @pl.when(i == n_tiles - 1)
        def _():
            wait_scatter(i, slot)

            @pl.when(i >= 1)
            def _():
                wait_scatter(i - 1, 1 - slot)


def _moe_experts(xa, plan, prm, name):
    n = xa.shape[0]
    tm = MOE_TM
    sub = SUBLANES
    src, tile_group, real, n_tiles = plan
    grouped = lambda w: w.reshape(MOE_GROUPS, MOE_PER_GROUP, w.shape[1], w.shape[2])
    w_spec = lambda a, b: pl.BlockSpec((1, MOE_PER_GROUP, a, b), lambda i, src, grp, real, nt: (grp[i], 0, 0, 0))
    grid_spec = pltpu.PrefetchScalarGridSpec(
        num_scalar_prefetch=4,
        grid=(n // tm + MOE_GROUPS,),
        in_specs=[pl.BlockSpec(memory_space=pl.ANY),
                  pl.BlockSpec((1, D_MODEL), lambda i, *_: (0, 0)),
                  w_spec(D_MODEL, MOE_FF), w_spec(D_MODEL, MOE_FF), w_spec(MOE_FF, D_MODEL)],
        out_specs=pl.BlockSpec(memory_space=pl.ANY),
        scratch_shapes=[pltpu.VMEM((2, tm // sub, sub, MOE_AUG_W), F32),
                        pltpu.VMEM((2, tm // sub, sub, D_MODEL), F32),
                        pltpu.SemaphoreType.DMA((2,)),
                        pltpu.SemaphoreType.DMA((2,))])
    out = pl.pallas_call(
        _moe_expert_kernel,
        grid_spec=grid_spec,
        out_shape=jax.ShapeDtypeStruct((n // sub, sub, D_MODEL), F32),
        compiler_params=_cparams("arbitrary"),
        name=name,
    )(src, tile_group, real, n_tiles, xa.reshape(n // sub, sub, MOE_AUG_W), prm["g"],
      grouped(prm["wg"]), grouped(prm["wu"]), grouped(prm["wd"]))
    return out.reshape(n, D_MODEL)


def _final_norm_kernel(x_ref, g_ref, o_ref):
    x = x_ref[...]
    y = x * lax.rsqrt(jnp.mean(x * x, axis=-1, keepdims=True) + RMS_EPS)
    o_ref[...] = y * g_ref[...]


def _final_norm(x, g, name):
    n = x.shape[0]
    tm = 512
    return pl.pallas_call(
        _final_norm_kernel,
        grid=(n // tm,),
        in_specs=[pl.BlockSpec((tm, D_MODEL), lambda i: (i, 0)),
                  pl.BlockSpec((1, D_MODEL), lambda i: (0, 0))],
        out_specs=pl.BlockSpec((tm, D_MODEL), lambda i: (i, 0)),
        out_shape=jax.ShapeDtypeStruct((n, D_MODEL), F32),
        compiler_params=_cparams("parallel"),
        name=name,
    )(x, g.astype(F32).reshape(1, D_MODEL))


def _t5_bucket(dist):
    n = jnp.maximum(dist, 0)
    max_exact = T5_BUCKETS // 2
    nf = jnp.maximum(n, 1).astype(F32)
    large = max_exact + (jnp.log(nf / max_exact) / math.log(T5_MAX_DIST / max_exact)
                         * (T5_BUCKETS - max_exact)).astype(jnp.int32)
    return jnp.where(n < max_exact, n, jnp.minimum(large, T5_BUCKETS - 1))


def _rel_bias(table, dist):
    hit = _t5_bucket(dist)[..., None] == jnp.arange(T5_BUCKETS)
    rows = jnp.sum(jnp.where(hit[..., None], table.astype(F32), 0.0), axis=-2)
    return jnp.moveaxis(rows, -1, 0)


def _band_bias_mask(table, max_dist, dil):
    r = jnp.arange(BLK)[:, None]
    s = jnp.arange(2 * BLK)[None, :]
    dist = BLK + r - s
    valid = (dist >= 0) & (dist <= max_dist)
    bias = _rel_bias(table, dist * dil)
    later = jnp.where(valid[None], bias, NEG)
    first = jnp.where((valid & (s >= BLK))[None], bias, NEG)
    return jnp.stack([first, later])


def _swa_sample_bias_mask(table):
    s = jnp.arange(DEC_SEQ)[:, None]
    col = jnp.arange(BLK)[None, :]
    dist_c = A_WINDOW + s - col
    dist_n = s - col
    tab = table[:, :A_Q_HEADS]
    bc = jnp.where(((dist_c >= 0) & (dist_c < A_WINDOW))[None], _rel_bias(tab, dist_c), NEG)
    bn = jnp.where(((dist_n >= 0) & (col < DEC_SEQ))[None], _rel_bias(tab, dist_n), NEG)
    return bc.reshape(A_Q_HEADS * DEC_SEQ, A_WINDOW), bn.reshape(A_Q_HEADS * DEC_SEQ, BLK)


def _dil_sample_bias_mask(table):
    s = jnp.arange(DEC_SEQ)[:, None]
    col = jnp.arange(BLK)[None, :]
    bcs, bns = [], []
    for g, (win, dil) in enumerate(D_PAIRS):
        lo = A_Q_HEADS + g * D_HEADS_PER
        tab = table[:, lo:lo + D_HEADS_PER]
        dist_c = win + s - jnp.arange(win)[None, :]
        valid_c = (dist_c >= 0) & (dist_c % dil == 0) & (dist_c <= win)
        bc = jnp.where(valid_c[None], _rel_bias(tab, dist_c), NEG)
        dist_n = s - col
        valid_n = (dist_n >= 0) & (dist_n % dil == 0) & (col < DEC_SEQ)
        bn = jnp.where(valid_n[None], _rel_bias(tab, dist_n), NEG)
        bcs.append(bc.reshape(1, D_HEADS_PER * DEC_SEQ, win))
        bns.append(bn.reshape(D_HEADS_PER * DEC_SEQ, BLK))
    return bcs, jnp.stack(bns)


def _native_cache(c):
    n_layers, nb, wc = c.shape[:3]
    return c.transpose(0, 1, 3, 4, 5, 2).reshape(n_layers, nb, -1, wc)


def _logical_cache(c, heads):
    n_layers, nb, _, wc = c.shape
    return c.reshape(n_layers, nb, 2, heads, HEAD_DIM, wc).transpose(0, 1, 5, 2, 3, 4)


def kernel(x_prompt, x_sample, cache_a_kv, state_b, state_c_h, state_c_conv, cache_d_g0, cache_d_g1, cache_d_g2, rel_table, norm_mix, norm_ffn, norm_final, w_in_even, w_out_even, sinks_a, s5_a_re, s5_a_im, s5_log_dt, s5_b_re, s5_b_im, s5_c_re, s5_c_im, s5_d, s5_glu_w, s5_glu_b, w_in_odd, w_out_odd, conv_w, conv_b, gate_a_w, gate_a_b, gate_x_w, gate_x_b, lru_lambda, moe_router_g, moe_router_g_b, moe_router_e, moe_router_e_b, moe_w_gate, moe_w_up, moe_w_down):
    xp = x_prompt.astype(F32).reshape(NP_TOK, D_MODEL)
    xs = x_sample.astype(F32).reshape(NS_TOK, D_MODEL)

    cache_a = _native_cache(cache_a_kv)
    caches_d = [_native_cache(c) for c in (cache_d_g0, cache_d_g1, cache_d_g2)]

    bm_a = _band_bias_mask(rel_table[:, :A_Q_HEADS], A_WINDOW - 1, 1)
    bm_d = [_band_bias_mask(rel_table[:, A_Q_HEADS + g * D_HEADS_PER:A_Q_HEADS + (g + 1) * D_HEADS_PER],
                            win // dil, dil) for g, (win, dil) in enumerate(D_PAIRS)]
    bmc_a_s, bmn_a_s = _swa_sample_bias_mask(rel_table)
    bcs_d_s, bn_d_s = _dil_sample_bias_mask(rel_table)

    a_p, a_new, b_p, b_s = [], [], [], []
    ch_p, ch_s, cc_p, cc_s = [], [], [], []
    d_p = [[], [], []]
    d_new = [[], [], []]

    for layer in range(DEPTH):
        mp = _moe_params(norm_ffn[layer], moe_router_g[layer], moe_router_g_b[layer], moe_router_e[layer],
                         moe_router_e_b[layer], moe_w_gate[layer], moe_w_up[layer], moe_w_down[layer])
        if layer % 2 == 0:
            e = layer // 2
            w_in = w_in_even[e].astype(BF16)
            w_out = w_out_even[e].astype(BF16)
            splits = (A_Q_W, 2 * A_KV_W, B_WIDTH)
            s5p = _s5_params(s5_a_re[e], s5_a_im[e], s5_log_dt[e], s5_b_re[e], s5_b_im[e],
                             s5_c_re[e], s5_c_im[e], s5_d[e], s5_glu_w[e], s5_glu_b[e])
            sinks = sinks_a[e].astype(F32)

            q_p, kv_p, u_p = _norm_proj(xp, norm_mix[layer], w_in, splits, f"in_proj_p{layer}")
            q_s, kv_s, u_s = _norm_proj(xs, norm_mix[layer], w_in, splits, f"in_proj_s{layer}")

            oa_p = _swa_prompt(q_p, kv_p, sinks, bm_a)
            oa_s = _swa_sample(q_s, kv_s, cache_a, e, sinks, bmc_a_s, bmn_a_s)

            ob_p, hl_p = _s5_mixer(u_p, jnp.zeros((BATCH, B_STATE_W), F32), s5p, BATCH, SEQ, 64, "s5_prompt")
            ob_s, hl_s = _s5_mixer(u_s, _s5_state_to_cols(state_b[e]), s5p, DEC_BATCH, DEC_SEQ, DEC_SEQ,
                                   "s5_sample")

            xa_p = _out_proj(_out_proj_even_kernel, xp, [oa_p, ob_p], w_out, f"out_proj_p{layer}",
                             route_prm=mp)
            xs = _out_proj(_out_proj_even_kernel, xs, [oa_s, ob_s], w_out, f"out_proj_s{layer}",
                           in_place=layer > 0)

            a_p.append(kv_p.reshape(BATCH, SEQ, 2, A_KV_HEADS, HEAD_DIM)[:, SEQ - A_WINDOW:])
            a_new.append(kv_s.reshape(DEC_BATCH, DEC_SEQ, 2 * A_KV_W))
            b_p.append(_s5_cols_to_state(hl_p))
            b_s.append(_s5_cols_to_state(hl_s))
        else:
            o = layer // 2
            w_in = w_in_odd[o].astype(BF16)
            w_out = w_out_odd[o].astype(BF16)
            splits = (C_WIDTH, C_WIDTH, D_QKV_W, D_QKV_W, D_QKV_W)
            lrp = _rglru_params(conv_w[o], conv_b[o], gate_a_w[o], gate_a_b[o], gate_x_w[o], gate_x_b[o],
                                lru_lambda[o])

            xr_p, gate_p, q_p, k_p, v_p = _norm_proj(xp, norm_mix[layer], w_in, splits, f"in_proj_p{layer}")
            xr_s, gate_s, q_s, k_s, v_s = _norm_proj(xs, norm_mix[layer], w_in, splits, f"in_proj_s{layer}")

            od_p, lse_p = [], []
            for g, (win, dil) in enumerate(D_PAIRS):
                og, lg = _dilated_prompt_group(q_p, k_p, v_p, bm_d[g], g, dil)
                od_p.append(og)
                lse_p.append(lg)
            k_s4 = k_s.reshape(DEC_BATCH, DEC_SEQ, D_N_GROUPS, D_GROUP_W)
            v_s4 = v_s.reshape(DEC_BATCH, DEC_SEQ, D_N_GROUPS, D_GROUP_W)
            for g in range(D_N_GROUPS):
                d_new[g].append(jnp.concatenate([k_s4[:, :, g], v_s4[:, :, g]], axis=-1))
            if o < caches_d[0].shape[0] - 1:
                od_s, lse_s = _dilated_sample(q_s, k_s, v_s, caches_d, o, bcs_d_s, bn_d_s)
            else:
                od_s, lse_s, shifted_d = _dilated_sample_and_shift(
                    q_s, k_s, v_s, caches_d, [jnp.stack(rows) for rows in d_new], bcs_d_s, bn_d_s)

            oc_p, hc_p = _rglru_mixer(xr_p, gate_p, jnp.zeros(((C_CONV - 1) * BATCH, C_WIDTH), F32),
                                      jnp.zeros((BATCH, C_WIDTH), F32), lrp, BATCH, SEQ, 128, "rglru_prompt")
            conv_s = state_c_conv[o].astype(F32).transpose(1, 0, 2).reshape((C_CONV - 1) * DEC_BATCH, C_WIDTH)
            oc_s, hc_s = _rglru_mixer(xr_s, gate_s, conv_s, state_c_h[o].astype(F32), lrp,
                                      DEC_BATCH, DEC_SEQ, DEC_SEQ, "rglru_sample")

            xa_p = _out_proj(_out_proj_odd_kernel, xp, [oc_p] + od_p + lse_p, w_out, f"out_proj_p{layer}",
                             route_prm=mp)
            xs = _out_proj(_out_proj_odd_kernel, xs, [oc_s] + list(od_s) + list(lse_s), w_out,
                           f"out_proj_s{layer}")

            ch_p.append(hc_p)
            ch_s.append(hc_s)
            cc_p.append(xr_p.reshape(BATCH, SEQ, C_WIDTH)[:, SEQ - (C_CONV - 1):])
            cc_s.append(xr_s.reshape(DEC_BATCH, DEC_SEQ, C_WIDTH)[:, DEC_SEQ - (C_CONV - 1):])
            k_p5 = k_p.reshape(BATCH, SEQ, D_N_GROUPS, D_HEADS_PER, HEAD_DIM)
            v_p5 = v_p.reshape(BATCH, SEQ, D_N_GROUPS, D_HEADS_PER, HEAD_DIM)
            for g, (win, dil) in enumerate(D_PAIRS):
                wc = min(win, SEQ)
                d_p[g].append(jnp.stack([k_p5[:, SEQ - wc:, g], v_p5[:, SEQ - wc:, g]], axis=2))

        xp = _moe_experts(xa_p, _moe_plan(xa_p[:, D_MODEL], NP_TOK), mp, f"moe_experts_p{layer}")
        xs = _moe(xs, mp, f"moe_s{layer}")

    y_prompt = _final_norm(xp, norm_final, "final_norm_p").reshape(BATCH, SEQ, D_MODEL)
    y_sample = _final_norm(xs, norm_final, "final_norm_s").reshape(DEC_BATCH, DEC_SEQ, D_MODEL)

    new_a = _logical_cache(_shift_cache(cache_a, jnp.stack(a_new), "shift_cache_a"), A_KV_HEADS)
    new_d = [_logical_cache(c, D_HEADS_PER) for c in shifted_d]

    return (y_prompt, y_sample,
            jnp.stack(a_p), new_a, jnp.stack(b_p), jnp.stack(b_s),
            jnp.stack(ch_p), jnp.stack(ch_s), jnp.stack(cc_p), jnp.stack(cc_s),
            jnp.stack(d_p[0]), new_d[0], jnp.stack(d_p[1]), new_d[1], jnp.stack(d_p[2]), new_d[2])
```

```python
import functools
import math

import jax
import jax.numpy as jnp
from jax import lax
from jax.experimental import pallas as pl
from jax.experimental.pallas import tpu as pltpu

F32 = jnp.float32
BF16 = jnp.bfloat16

D_MODEL = 1024
BATCH = 8
SEQ = 2048
DEPTH = 4
DEC_BATCH = 128
DEC_SEQ = 8
HEAD_DIM = 64
BLK = 128
RMS_EPS = 1e-6
NEG = -1e30

A_Q_HEADS = 8
A_KV_HEADS = 2
A_GQA = 4
A_WINDOW = 128
A_Q_W = A_Q_HEADS * HEAD_DIM
A_KV_W = A_KV_HEADS * HEAD_DIM

B_WIDTH = 512
B_GROUP_CH = 16
B_GROUPS = 32
B_STATE = 64
B_LANE_BLOCKS = 4
B_GROUPS_PER_BLOCK = B_GROUPS // B_LANE_BLOCKS
B_HALF = B_GROUPS_PER_BLOCK * B_STATE
B_STATE_W = B_LANE_BLOCKS * 2 * B_HALF

C_WIDTH = 512
C_BLOCKS = 8
C_BLOCK_W = 64
C_CONV = 4
C_POWER = 8.0

D_PAIRS = ((128, 1), (512, 4), (2048, 16))
D_N_GROUPS = 3
D_HEADS_PER = 4
D_GROUP_W = D_HEADS_PER * HEAD_DIM
D_QKV_W = D_N_GROUPS * D_GROUP_W

T5_BUCKETS = 32
T5_MAX_DIST = 2048

MOE_GROUPS = 4
MOE_PER_GROUP = 4
MOE_EXPERTS = 16
MOE_FF = 256
ROUTER_LANES = 128

NP_TOK = BATCH * SEQ
NS_TOK = DEC_BATCH * DEC_SEQ

VMEM_LIMIT_BYTES = 52 * 2 ** 20


def _cparams(*sem):
    return pltpu.CompilerParams(dimension_semantics=sem, vmem_limit_bytes=VMEM_LIMIT_BYTES)


def _nt_dot(a, b):
    return lax.dot_general(a, b, (((1,), (1,)), ((), ())), preferred_element_type=F32)


LANES = 128
SUBLANES = 8


def _norm_proj_kernel(x_ref, g_ref, w_ref, *out_refs, splits):
    x = x_ref[...]
    y = x * lax.rsqrt(jnp.mean(x * x, axis=-1, keepdims=True) + RMS_EPS)
    xn = (y * g_ref[...]).astype(BF16)
    off = 0
    for o_ref, width in zip(out_refs, splits):
        o_ref[...] = jnp.dot(xn, w_ref[:, off:off + width], preferred_element_type=F32)
        off += width


def _norm_proj(x, g, w, splits, name):
    n = x.shape[0]
    tm = 512
    return pl.pallas_call(
        functools.partial(_norm_proj_kernel, splits=splits),
        grid=(n // tm,),
        in_specs=[pl.BlockSpec((tm, D_MODEL), lambda i: (i, 0)),
                  pl.BlockSpec((1, D_MODEL), lambda i: (0, 0)),
                  pl.BlockSpec(w.shape, lambda i: (0, 0))],
        out_specs=[pl.BlockSpec((tm, s), lambda i: (i, 0)) for s in splits],
        out_shape=[jax.ShapeDtypeStruct((n, s), F32) for s in splits],
        compiler_params=_cparams("parallel"),
        name=name,
    )(x, g.reshape(1, D_MODEL), w)


BAND_NSUB = 4


def _softmax_parts(scores, sink):
    m = jnp.max(scores[0], axis=1, keepdims=True)
    for s in scores[1:]:
        m = jnp.maximum(m, jnp.max(s, axis=1, keepdims=True))
    if sink is not None:
        m = jnp.maximum(m, sink)
    ps = [jnp.exp(s - m) for s in scores]
    den = jnp.sum(ps[0], axis=1, keepdims=True)
    for p in ps[1:]:
        den = den + jnp.sum(p, axis=1, keepdims=True)
    if sink is not None:
        den = den + jnp.exp(sink - m)
    inv = 1.0 / den
    return [(p * inv).astype(BF16) for p in ps], den, m


def _band_attn_kernel(*refs, n_heads, gqa, with_sink, dil, has_prev, nsub=1):
    refs = list(refs)
    sink_ref = refs.pop(0) if with_sink else None
    q_ref, kc_ref = refs.pop(0), refs.pop(0)
    kp_ref = refs.pop(0) if has_prev else None
    vc_ref = refs.pop(0)
    vp_ref = refs.pop(0) if has_prev else None
    bm_ref = refs.pop(0)
    bm_later_ref = refs.pop(0) if nsub > 1 else bm_ref
    o_ref = refs.pop(0)
    lse_ref = refs.pop(0) if refs else None
    n_kv = n_heads // gqa
    if dil == 1:
        row_sel = [slice(s * BLK, (s + 1) * BLK) for s in range(nsub)]
    else:
        row_sel = [pl.ds(r, BLK, stride=dil) for r in range(dil)]
    bm_of = [bm_ref if (dil > 1 or s == 0) else bm_later_ref for s in range(len(row_sel))]
    sc, sp, v_cur, v_prev = [], [], [], []
    kc = vc = None
    for s, rows in enumerate(row_sel):
        q = (q_ref[rows, :] * HEAD_DIM ** -0.5).astype(BF16)
        if has_prev:
            if dil == 1 and s > 0:
                kp, vp = kc, vc
            elif dil == 1:
                kp, vp = kp_ref[...].astype(BF16), vp_ref[...].astype(BF16)
            else:
                kp, vp = kp_ref[rows, :].astype(BF16), vp_ref[rows, :].astype(BF16)
        kc = kc_ref[rows, :].astype(BF16)
        vc = vc_ref[rows, :].astype(BF16)
        for hk in range(n_kv):
            ksl = slice(hk * HEAD_DIM, (hk + 1) * HEAD_DIM)
            qs = jnp.concatenate([q[:, h * HEAD_DIM:(h + 1) * HEAD_DIM]
                                  for h in range(hk * gqa, (hk + 1) * gqa)], axis=0)
            sc.append(_nt_dot(qs, kc[:, ksl]))
            v_cur.append(vc[:, ksl])
            if has_prev:
                sp.append(_nt_dot(qs, kp[:, ksl]))
                v_prev.append(vp[:, ksl])
    n_units = len(row_sel)
    bias_c = jnp.concatenate([bm[0, h, :, BLK:2 * BLK] for bm in bm_of for h in range(n_heads)], axis=0)
    scores = [jnp.concatenate(sc, axis=0) + bias_c]
    if has_prev:
        bias_p = jnp.concatenate([bm[0, h, :, 0:BLK] for bm in bm_of for h in range(n_heads)], axis=0)
        scores.append(jnp.concatenate(sp, axis=0) + bias_p)
    sink = None
    if with_sink:
        sink = jnp.concatenate([jnp.full((BLK, 1), sink_ref[h], F32) for h in range(n_heads)] * n_units,
                               axis=0)
    ps, den, m = _softmax_parts(scores, sink)
    lse = m + jnp.log(den) if lse_ref is not None else None
    unit_rows = gqa * BLK
    for r, rows in enumerate(row_sel):
        outs = []
        for hk in range(n_kv):
            u = r * n_kv + hk
            usl = slice(u * unit_rows, (u + 1) * unit_rows)
            o = jnp.dot(ps[0][usl, :], v_cur[u], preferred_element_type=F32)
            if has_prev:
                o = o + jnp.dot(ps[1][usl, :], v_prev[u], preferred_element_type=F32)
            outs += [o[j * BLK:(j + 1) * BLK, :] for j in range(gqa)]
        o_ref[rows, :] = jnp.concatenate(outs, axis=1)
        if lse_ref is not None:
            base = r * n_heads * BLK
            lse_ref[rows, :] = jnp.concatenate(
                [jnp.broadcast_to(lse[base + h * BLK:base + (h + 1) * BLK, :], (BLK, HEAD_DIM))
                 for h in range(n_heads)], axis=1)


def _swa_prompt(q, kv, sinks, bm):
    nsub = BAND_NSUB
    nstep = SEQ // (BLK * nsub)
    row = lambda b, i: b * nstep + i
    prev = lambda b, i: (b * nstep + i) * nsub - jnp.minimum(i, 1)
    bm_spec = lambda pick: pl.BlockSpec((1, A_Q_HEADS, BLK, 2 * BLK), lambda b, i: (pick(i), 0, 0, 0))
    return pl.pallas_call(
        functools.partial(_band_attn_kernel, n_heads=A_Q_HEADS, gqa=A_GQA, with_sink=True, dil=1,
                          has_prev=True, nsub=nsub),
        grid=(BATCH, nstep),
        in_specs=[pl.BlockSpec(memory_space=pltpu.SMEM),
                  pl.BlockSpec((BLK * nsub, A_Q_W), lambda b, i: (row(b, i), 0)),
                  pl.BlockSpec((BLK * nsub, A_KV_W), lambda b, i: (row(b, i), 0)),
                  pl.BlockSpec((BLK, A_KV_W), lambda b, i: (prev(b, i), 0)),
                  pl.BlockSpec((BLK * nsub, A_KV_W), lambda b, i: (row(b, i), 1)),
                  pl.BlockSpec((BLK, A_KV_W), lambda b, i: (prev(b, i), 1)),
                  bm_spec(lambda i: jnp.minimum(i, 1)), bm_spec(lambda i: 1)],
        out_specs=pl.BlockSpec((BLK * nsub, A_Q_W), lambda b, i: (row(b, i), 0)),
        out_shape=jax.ShapeDtypeStruct((NP_TOK, A_Q_W), F32),
        compiler_params=_cparams("parallel", "parallel"),
        name="swa_prompt",
    )(sinks, q, kv, kv, kv, kv, bm, bm)


def _dilated_prompt_group(q, k, v, bm, g, dil):
    nsub = BAND_NSUB if dil == 1 else 1
    rows = BLK * dil * nsub
    nchunk = SEQ // rows
    has_prev = SEQ // dil > BLK
    pair = LANES // HEAD_DIM if dil > 1 else D_HEADS_PER
    npair = D_HEADS_PER // pair
    width = pair * HEAD_DIM
    row = lambda b, i: b * nchunk + i
    prev_rows = rows // nsub
    prev = lambda b, i: (b * nchunk + i) * nsub - jnp.minimum(i, 1)
    cur_spec = pl.BlockSpec((rows, width), lambda b, i, p: (row(b, i), g * npair + p))
    prev_spec = pl.BlockSpec((prev_rows, width), lambda b, i, p: (prev(b, i), g * npair + p))
    out_spec = pl.BlockSpec((rows, width), lambda b, i, p: (row(b, i), p))
    bm_spec = lambda pick: pl.BlockSpec((1, pair, BLK, 2 * BLK), lambda b, i, p: (pick(i), p, 0, 0))
    bm_specs, bm_args = [bm_spec(lambda i: jnp.minimum(i, 1))], [bm]
    if nsub > 1:
        bm_specs.append(bm_spec(lambda i: 1))
        bm_args.append(bm)
    if has_prev:
        in_specs, args = [cur_spec, cur_spec, prev_spec, cur_spec, prev_spec] + bm_specs, (q, k, k, v, v, *bm_args)
    else:
        in_specs, args = [cur_spec, cur_spec, cur_spec] + bm_specs, (q, k, v, *bm_args)
    return pl.pallas_call(
        functools.partial(_band_attn_kernel, n_heads=pair, gqa=1, with_sink=False, dil=dil,
                          has_prev=has_prev, nsub=nsub),
        grid=(BATCH, nchunk, npair),
        in_specs=in_specs,
        out_specs=[out_spec, out_spec],
        out_shape=[jax.ShapeDtypeStruct((NP_TOK, D_GROUP_W), F32)] * 2,
        compiler_params=_cparams("parallel", "parallel", "parallel"),
        name=f"dilated_prompt_g{g}",
    )(*args)


SWA_SAMPLE_BB = 8


def _pad_rows(x, rows):
    return jnp.concatenate([x, jnp.zeros((rows - x.shape[0], x.shape[1]), x.dtype)], axis=0)


def _swa_sample_kernel(sink_ref, q_ref, kv_ref, c_ref, bmc_ref, bmn_ref, o_ref):
    for bb in range(SWA_SAMPLE_BB):
        rs = slice(bb * DEC_SEQ, (bb + 1) * DEC_SEQ)
        q = (q_ref[rs, :] * HEAD_DIM ** -0.5).astype(BF16)
        kvn = _pad_rows(kv_ref[rs, :], BLK).astype(BF16)
        cache = c_ref[0, bb].astype(BF16)
        outs = [None] * A_Q_HEADS
        for hk in range(A_KV_HEADS):
            heads = range(hk * A_GQA, (hk + 1) * A_GQA)
            qs = jnp.concatenate([q[:, h * HEAD_DIM:(h + 1) * HEAD_DIM] for h in heads], axis=0)
            ksl = slice(hk * HEAD_DIM, (hk + 1) * HEAD_DIM)
            vsl = slice(A_KV_W + hk * HEAD_DIM, A_KV_W + (hk + 1) * HEAD_DIM)
            brow = slice(hk * A_GQA * DEC_SEQ, (hk + 1) * A_GQA * DEC_SEQ)
            s1 = jnp.dot(qs, cache[ksl, :], preferred_element_type=F32) + bmc_ref[brow, :]
            s2 = _nt_dot(qs, kvn[:, ksl]) + bmn_ref[brow, :]
            sink = jnp.concatenate(
                [jnp.full((DEC_SEQ, 1), sink_ref[h], F32) for h in heads], axis=0)
            (p1, p2), den, _ = _softmax_parts([s1, s2], sink)
            o = _nt_dot(p1, cache[vsl, :]) + jnp.dot(p2, kvn[:, vsl], preferred_element_type=F32)
            for j, h in enumerate(heads):
                outs[h] = o[j * DEC_SEQ:(j + 1) * DEC_SEQ, :]
        o_ref[rs, :] = jnp.concatenate(outs, axis=1)


def _swa_sample(q, kv, cache_all, layer, sinks, bmc, bmn):
    bb = SWA_SAMPLE_BB
    return pl.pallas_call(
        _swa_sample_kernel,
        grid=(DEC_BATCH // bb,),
        in_specs=[pl.BlockSpec(memory_space=pltpu.SMEM),
                  pl.BlockSpec((bb * DEC_SEQ, A_Q_W), lambda i: (i, 0)),
                  pl.BlockSpec((bb * DEC_SEQ, 2 * A_KV_W), lambda i: (i, 0)),
                  pl.BlockSpec((1, bb, 2 * A_KV_W, A_WINDOW), lambda i: (layer, i, 0, 0)),
                  pl.BlockSpec(bmc.shape, lambda i: (0, 0)),
                  pl.BlockSpec(bmn.shape, lambda i: (0, 0))],
        out_specs=pl.BlockSpec((bb * DEC_SEQ, A_Q_W), lambda i: (i, 0)),
        out_shape=jax.ShapeDtypeStruct((NS_TOK, A_Q_W), F32),
        compiler_params=_cparams("parallel"),
        name="swa_sample",
    )(sinks, q, kv, cache_all, bmc, bmn)


def _dil_sample_kernel(q_ref, k_ref, v_ref, c0_ref, c1_ref, c2_ref,
                       bc0_ref, bc1_ref, bc2_ref, bn_ref,
                       o0_ref, o1_ref, o2_ref, l0_ref, l1_ref, l2_ref):
    q = q_ref[...] * HEAD_DIM ** -0.5
    k = k_ref[...]
    v = v_ref[...]
    nrow = D_HEADS_PER * DEC_SEQ
    row_head = lax.broadcasted_iota(jnp.int32, (nrow, D_GROUP_W), 0) // DEC_SEQ
    lane_head = lax.broadcasted_iota(jnp.int32, (nrow, D_GROUP_W), 1) // HEAD_DIM
    head_mask = row_head == lane_head
    out_lane_head = lax.broadcasted_iota(jnp.int32, (DEC_SEQ, D_GROUP_W), 1) // HEAD_DIM
    groups = ((c0_ref, bc0_ref, o0_ref, l0_ref), (c1_ref, bc1_ref, o1_ref, l1_ref),
              (c2_ref, bc2_ref, o2_ref, l2_ref))
    for g, (c_ref, bc_ref, o_ref, l_ref) in enumerate(groups):
        gsl = slice(g * D_GROUP_W, (g + 1) * D_GROUP_W)
        qbd = jnp.where(head_mask, jnp.concatenate([q[:, gsl]] * D_HEADS_PER, axis=0), 0.0).astype(BF16)
        kn = _pad_rows(k[:, gsl], BLK).astype(BF16)
        vn = _pad_rows(v[:, gsl], BLK).astype(BF16)
        kt = c_ref[0, 0, 0:D_GROUP_W, :].astype(BF16)
        vt = c_ref[0, 0, D_GROUP_W:2 * D_GROUP_W, :].astype(BF16)
        s1 = jnp.dot(qbd, kt, preferred_element_type=F32) + bc_ref[0]
        s2 = _nt_dot(qbd, kn) + bn_ref[g]
        (p1, p2), den, m = _softmax_parts([s1, s2], None)
        of = _nt_dot(p1, vt) + jnp.dot(p2, vn, preferred_element_type=F32)
        lse = m + jnp.log(den)
        og = jnp.zeros((DEC_SEQ, D_GROUP_W), F32)
        lg = jnp.zeros((DEC_SEQ, D_GROUP_W), F32)
        for h in range(D_HEADS_PER):
            rs = slice(h * DEC_SEQ, (h + 1) * DEC_SEQ)
            sel = out_lane_head == h
            og = og + jnp.where(sel, of[rs, :], 0.0)
            lg = lg + jnp.where(sel, lse[rs, :], 0.0)
        o_ref[...] = og
        l_ref[...] = lg


def _shift_block(c_ref, n_ref, o_ref, i):
    width, wc = c_ref.shape[2], c_ref.shape[3]
    lane = lax.broadcasted_iota(jnp.int32, (width, BLK), 1)
    shifted = pltpu.roll(c_ref[0, i], wc - DEC_SEQ, 1)
    new_t = jnp.concatenate([jnp.zeros((BLK - DEC_SEQ, width), F32), n_ref[0, i]], axis=0).T
    if wc > BLK:
        o_ref[0, i, :, 0:wc - BLK] = shifted[:, 0:wc - BLK]
    o_ref[0, i, :, wc - BLK:wc] = jnp.where(lane >= BLK - DEC_SEQ, new_t, shifted[:, wc - BLK:wc])


def _shift_kernel(c_ref, n_ref, o_ref, *, bb):
    for i in range(bb):
        _shift_block(c_ref, n_ref, o_ref, i)


def _dilated_sample(q, k, v, caches, layer, bcs, bn):
    tok_spec = pl.BlockSpec((DEC_SEQ, D_QKV_W), lambda b: (b, 0))
    out_spec = pl.BlockSpec((DEC_SEQ, D_GROUP_W), lambda b: (b, 0))
    outs = pl.pallas_call(
        _dil_sample_kernel,
        grid=(DEC_BATCH,),
        in_specs=[tok_spec, tok_spec, tok_spec]
        + [pl.BlockSpec((1, 1, 2 * D_GROUP_W, win), lambda b: (layer, b, 0, 0)) for win, _ in D_PAIRS]
        + [pl.BlockSpec(bc.shape, lambda b: (0, 0, 0)) for bc in bcs]
        + [pl.BlockSpec(bn.shape, lambda b: (0, 0, 0))],
        out_specs=[out_spec] * 6,
        out_shape=[jax.ShapeDtypeStruct((NS_TOK, D_GROUP_W), F32)] * 6,
        compiler_params=_cparams("parallel"),
        name="dilated_sample",
    )(q, k, v, *caches, *bcs, bn)
    return outs[:3], outs[3:]


def _dil_sample_shift_kernel(q_ref, k_ref, v_ref, c0_ref, c1_ref, c2_ref, n0_ref, n1_ref, n2_ref,
                             bc0_ref, bc1_ref, bc2_ref, bn_ref,
                             o0_ref, o1_ref, o2_ref, l0_ref, l1_ref, l2_ref, s0_ref, s1_ref, s2_ref):
    for c_ref, n_ref, s_ref in ((c0_ref, n0_ref, s0_ref), (c1_ref, n1_ref, s1_ref), (c2_ref, n2_ref, s2_ref)):
        _shift_block(c_ref, n_ref, s_ref, 0)

    @pl.when(pl.program_id(0) == pl.num_programs(0) - 1)
    def _():
        _dil_sample_kernel(q_ref, k_ref, v_ref, c0_ref, c1_ref, c2_ref, bc0_ref, bc1_ref, bc2_ref, bn_ref,
                           o0_ref, o1_ref, o2_ref, l0_ref, l1_ref, l2_ref)


def _dilated_sample_and_shift(q, k, v, caches, new_rows, bcs, bn):
    n_layers = caches[0].shape[0]
    last = n_layers - 1
    tok_row = lambda l, b: jnp.where(l == last, b, 0)
    tok_spec = pl.BlockSpec((DEC_SEQ, D_QKV_W), lambda l, b: (tok_row(l, b), 0))
    out_spec = pl.BlockSpec((DEC_SEQ, D_GROUP_W), lambda l, b: (tok_row(l, b), 0))
    cache_specs = [pl.BlockSpec((1, 1, 2 * D_GROUP_W, win), lambda l, b: (l, b, 0, 0)) for win, _ in D_PAIRS]
    new_spec = pl.BlockSpec((1, 1, DEC_SEQ, 2 * D_GROUP_W), lambda l, b: (l, b, 0, 0))
    outs = pl.pallas_call(
        _dil_sample_shift_kernel,
        grid=(n_layers, DEC_BATCH),
        in_specs=[tok_spec, tok_spec, tok_spec] + cache_specs + [new_spec] * 3
        + [pl.BlockSpec(bc.shape, lambda l, b: (0, 0, 0)) for bc in bcs]
        + [pl.BlockSpec(bn.shape, lambda l, b: (0, 0, 0))],
        out_specs=[out_spec] * 6 + cache_specs,
        out_shape=[jax.ShapeDtypeStruct((NS_TOK, D_GROUP_W), F32)] * 6
        + [jax.ShapeDtypeStruct(c.shape, c.dtype) for c in caches],
        compiler_params=_cparams("arbitrary", "arbitrary"),
        name="dilated_sample_shift",
    )(q, k, v, *caches, *new_rows, *bcs, bn)
    return outs[:3], outs[3:6], outs[6:]


def _shift_cache(cache, new_rows, name):
    n_layers, _, width, wc = cache.shape
    bb = max(1, (4 * 2 ** 20) // (wc * width * 4))
    return pl.pallas_call(
        functools.partial(_shift_kernel, bb=bb),
        grid=(n_layers, DEC_BATCH // bb),
        in_specs=[pl.BlockSpec((1, bb, width, wc), lambda l, i: (l, i, 0, 0)),
                  pl.BlockSpec((1, bb, DEC_SEQ, width), lambda l, i: (l, i, 0, 0))],
        out_specs=pl.BlockSpec((1, bb, width, wc), lambda l, i: (l, i, 0, 0)),
        out_shape=jax.ShapeDtypeStruct(cache.shape, cache.dtype),
        compiler_params=_cparams("parallel", "parallel"),
        name=name,
    )(cache, new_rows)


def _to_time_major(src_ref, dst_ref, off, bt, lc):
    for b in range(bt):
        x = src_ref[b]
        for j in range(dst_ref.shape[0]):
            dst_ref[j, pl.ds(off + b, lc, stride=bt), :] = x[:, j * LANES:(j + 1) * LANES]


def _from_time_major(src_ref, dst_ref, bt, lc):
    for b in range(bt):
        dst_ref[b] = jnp.concatenate(
            [src_ref[j, pl.ds(b, lc, stride=bt), :] for j in range(src_ref.shape[0])], axis=1)


def _lane_blocks(ref, r0, nrows):
    return jnp.concatenate([ref[j, r0:r0 + nrows, :] for j in range(ref.shape[0])], axis=1)


def _store_lane_blocks(ref, r0, x):
    for j in range(ref.shape[0]):
        ref[j, r0:r0 + x.shape[0], :] = x[:, j * LANES:(j + 1) * LANES]


def _token_spec(bt, lc, width):
    return pl.BlockSpec((bt, lc, width), lambda c: (0, c, 0))


def _s5_kernel(u_ref, h0_ref, bm_ref, cm_ref, lam_ref, d_ref, gw_ref, gb_ref, o_ref, hl_ref,
               us_ref, hs_ref, *, bt, lc):
    rows = lc * bt
    blk_w = 2 * B_HALF

    @pl.when(pl.program_id(0) == 0)
    def _():
        hs_ref[0:bt, :] = h0_ref[...]

    _to_time_major(u_ref, us_ref, 0, bt, lc)
    u = _lane_blocks(us_ref, 0, rows)
    ub = u.astype(BF16)
    for j in range(B_LANE_BLOCKS):
        cols = jnp.dot(ub[:, j * LANES:(j + 1) * LANES], bm_ref[j], preferred_element_type=F32)
        hs_ref[bt:, j * blk_w:j * blk_w + B_HALF] = cols[:, :B_HALF]
        hs_ref[bt:, j * blk_w + B_HALF:(j + 1) * blk_w] = cols[:, :B_HALF] + cols[:, B_HALF:]

    def step(t, carry):
        r0 = pl.multiple_of(t * bt, bt)
        for j in range(B_LANE_BLOCKS):
            re = slice(j * blk_w, j * blk_w + B_HALF)
            im = slice(j * blk_w + B_HALF, (j + 1) * blk_w)
            lr = lam_ref[2 * j:2 * j + 1, :]
            li = lam_ref[2 * j + 1:2 * j + 2, :]
            pr = hs_ref[pl.ds(r0, bt), re]
            pi = hs_ref[pl.ds(r0, bt), im]
            hs_ref[pl.ds(r0 + bt, bt), re] = lr * pr - li * pi + hs_ref[pl.ds(r0 + bt, bt), re]
            hs_ref[pl.ds(r0 + bt, bt), im] = lr * pi + li * pr + hs_ref[pl.ds(r0 + bt, bt), im]
        return carry

    lax.fori_loop(0, lc, step, 0)

    ys = []
    for j in range(B_LANE_BLOCKS):
        h_re = hs_ref[bt:, j * blk_w:j * blk_w + B_HALF]
        h_im = hs_ref[bt:, j * blk_w + B_HALF:(j + 1) * blk_w]
        lhs = jnp.concatenate([h_re + h_im, h_im], axis=1).astype(BF16)
        ys.append(jnp.dot(lhs, cm_ref[j], preferred_element_type=F32))
    y = jax.nn.gelu(jnp.concatenate(ys, axis=1) + d_ref[...] * u)
    z = jnp.dot(y.astype(BF16), gw_ref[...], preferred_element_type=F32) + gb_ref[...]
    _store_lane_blocks(us_ref, 0, y * jax.nn.sigmoid(z))
    _from_time_major(us_ref, o_ref, bt, lc)
    last = hs_ref[rows:rows + bt, :]
    hl_ref[...] = last
    hs_ref[0:bt, :] = last


def _s5_mixer(u, h0, prm, bt, seq, lc, name):
    tok = _token_spec(bt, lc, B_WIDTH)
    u_in = u.reshape(bt, seq, B_WIDTH)
    full = lambda a: pl.BlockSpec(a.shape, lambda c: (0,) * a.ndim)
    out, hl = pl.pallas_call(
        functools.partial(_s5_kernel, bt=bt, lc=lc),
        grid=(seq // lc,),
        in_specs=[tok, full(h0), full(prm["bm"]), full(prm["cm"]), full(prm["lam"]), full(prm["d"]),
                  full(prm["glu_w"]), full(prm["glu_b"])],
        out_specs=[tok, pl.BlockSpec((bt, B_STATE_W), lambda c: (0, 0))],
        out_shape=[jax.ShapeDtypeStruct(u_in.shape, F32), jax.ShapeDtypeStruct((bt, B_STATE_W), F32)],
        scratch_shapes=[pltpu.VMEM((B_WIDTH // LANES, lc * bt, LANES), F32),
                        pltpu.VMEM((lc * bt + bt, B_STATE_W), F32)],
        compiler_params=_cparams("arbitrary"),
        name=name,
    )(u_in, h0, prm["bm"], prm["cm"], prm["lam"], prm["d"], prm["glu_w"], prm["glu_b"])
    return out.reshape(bt * seq, B_WIDTH), hl


def _s5_params(a_re, a_im, log_dt, b_re, b_im, c_re, c_im, d, glu_w, glu_b):
    lam = lax.complex(a_re.astype(F32), a_im.astype(F32))
    dt = jnp.exp(log_dt.astype(F32))[:, None]
    lam_bar = jnp.exp(lam * dt)
    b_bar = ((lam_bar - 1.0) / lam)[..., None] * lax.complex(b_re.astype(F32), b_im.astype(F32))
    nb, gb = B_LANE_BLOCKS, B_GROUPS_PER_BLOCK
    eye = jnp.eye(gb, dtype=F32)

    def in_mat(part):
        p = part.reshape(nb, gb, B_STATE, B_GROUP_CH)
        return jnp.einsum("jgnc,gh->jgchn", p, eye).reshape(nb, gb * B_GROUP_CH, gb * B_STATE)

    def out_mat(part):
        p = part.reshape(nb, gb, B_GROUP_CH, B_STATE)
        return jnp.einsum("jgcn,gh->jgnhc", p, eye).reshape(nb, gb * B_STATE, gb * B_GROUP_CH)

    bm = jnp.concatenate([in_mat(b_bar.real), in_mat(b_bar.imag - b_bar.real)], axis=2).astype(BF16)
    c_re32, c_im32 = c_re.astype(F32), c_im.astype(F32)
    cm = jnp.concatenate([out_mat(c_re32), out_mat(-(c_re32 + c_im32))], axis=1).astype(BF16)
    lam_rows = jnp.stack([lam_bar.real.reshape(nb, B_HALF), lam_bar.imag.reshape(nb, B_HALF)],
                         axis=1).reshape(2 * nb, B_HALF)
    return {"bm": bm, "cm": cm, "lam": lam_rows, "d": d.astype(F32).reshape(1, B_WIDTH),
            "glu_w": glu_w.astype(BF16), "glu_b": glu_b.astype(F32).reshape(1, B_WIDTH)}


def _s5_state_to_cols(state):
    bt = state.shape[0]
    s = state.astype(F32).reshape(bt, B_LANE_BLOCKS, B_GROUPS_PER_BLOCK, B_STATE, 2)
    return s.transpose(0, 1, 4, 2, 3).reshape(bt, B_STATE_W)


def _s5_cols_to_state(cols):
    bt = cols.shape[0]
    s = cols.reshape(bt, B_LANE_BLOCKS, 2, B_GROUPS_PER_BLOCK, B_STATE)
    return s.transpose(0, 1, 3, 4, 2).reshape(bt, B_GROUPS, B_STATE, 2)


def _rglru_kernel(xr_ref, gate_ref, cb_ref, h0_ref, cw_ref, cbias_ref, wa_ref, ba_ref, wx_ref, bx_ref,
                  nsp_ref, o_ref, hl_ref, xp_ref, gs_ref, a_ref, hs_ref, *, bt, lc):
    rows = lc * bt
    pad = (C_CONV - 1) * bt

    @pl.when(pl.program_id(0) == 0)
    def _():
        _store_lane_blocks(xp_ref, 0, cb_ref[...])
        hs_ref[0:bt, :] = h0_ref[...]

    _to_time_major(xr_ref, xp_ref, pad, bt, lc)
    _to_time_major(gate_ref, gs_ref, 0, bt, lc)
    xc = _lane_blocks(xp_ref, 0, rows) * cw_ref[0:1, :]
    for tap in range(1, C_CONV):
        xc = xc + _lane_blocks(xp_ref, tap * bt, rows) * cw_ref[tap:tap + 1, :]
    xcf = xc + cbias_ref[...]
    xb = xcf.astype(BF16)
    r = jax.nn.sigmoid(jnp.dot(xb, wa_ref[...], preferred_element_type=F32) + ba_ref[...])
    i = jax.nn.sigmoid(jnp.dot(xb, wx_ref[...], preferred_element_type=F32) + bx_ref[...])
    log_a = nsp_ref[...] * r
    a = jnp.exp(log_a)
    a_ref[...] = a
    hs_ref[bt:, :] = jnp.sqrt(-jnp.tanh(log_a) * (a * a + 1.0)) * (i * xcf)

    def step(t, carry):
        r0 = pl.multiple_of(t * bt, bt)
        hs_ref[pl.ds(r0 + bt, bt), :] = (a_ref[pl.ds(r0, bt), :] * hs_ref[pl.ds(r0, bt), :]
                                         + hs_ref[pl.ds(r0 + bt, bt), :])
        return carry

    lax.fori_loop(0, lc, step, 0)

    _store_lane_blocks(gs_ref, 0, hs_ref[bt:, :] * jax.nn.gelu(_lane_blocks(gs_ref, 0, rows)))
    _from_time_major(gs_ref, o_ref, bt, lc)
    last = hs_ref[rows:rows + bt, :]
    hl_ref[...] = last
    hs_ref[0:bt, :] = last
    _store_lane_blocks(xp_ref, 0, _lane_blocks(xp_ref, rows, pad))


def _rglru_mixer(xr, gate, conv_buf, h0, prm, bt, seq, lc, name):
    rows = lc * bt
    tok = _token_spec(bt, lc, C_WIDTH)
    shape3 = lambda x: x.reshape(bt, seq, C_WIDTH)
    full = lambda a: pl.BlockSpec(a.shape, lambda c: (0,) * a.ndim)
    names = ("conv_w", "conv_b", "wa", "ba", "wx", "bx", "nsp")
    out, hl = pl.pallas_call(
        functools.partial(_rglru_kernel, bt=bt, lc=lc),
        grid=(seq // lc,),
        in_specs=[tok, tok, full(conv_buf), full(h0)] + [full(prm[k]) for k in names],
        out_specs=[tok, pl.BlockSpec((bt, C_WIDTH), lambda c: (0, 0))],
        out_shape=[jax.ShapeDtypeStruct(shape3(xr).shape, F32), jax.ShapeDtypeStruct((bt, C_WIDTH), F32)],
        scratch_shapes=[pltpu.VMEM((C_WIDTH // LANES, rows + (C_CONV - 1) * bt, LANES), F32),
                        pltpu.VMEM((C_WIDTH // LANES, rows, LANES), F32),
                        pltpu.VMEM((rows, C_WIDTH), F32),
                        pltpu.VMEM((rows + bt, C_WIDTH), F32)],
        compiler_params=_cparams("arbitrary"),
        name=name,
    )(shape3(xr), shape3(gate), conv_buf, h0, *[prm[k] for k in names])
    return out.reshape(bt * seq, C_WIDTH), hl


def _rglru_params(conv_w, conv_b, gate_a_w, gate_a_b, gate_x_w, gate_x_b, lru_lambda):
    eye = jnp.eye(C_BLOCKS, dtype=F32)

    def block_diag(w):
        return jnp.einsum("njk,nm->njmk", w.astype(F32), eye).reshape(C_WIDTH, C_WIDTH).astype(BF16)

    row = lambda x: x.astype(F32).reshape(1, C_WIDTH)
    return {"conv_w": conv_w.astype(F32), "conv_b": row(conv_b),
            "wa": block_diag(gate_a_w), "ba": row(gate_a_b),
            "wx": block_diag(gate_x_w), "bx": row(gate_x_b),
            "nsp": row(-C_POWER * jax.nn.softplus(-lru_lambda.astype(F32)))}


def _residual_out(x_ref, acc, tail_refs):
    *route_refs, o_ref = tail_refs
    x1 = x_ref[...] + acc
    if not route_refs:
        o_ref[...] = x1
        return
    g_ref, rw_ref, rb_ref = route_refs
    o_ref[:, 0:D_MODEL] = x1
    o_ref[:, D_MODEL:] = _route_lanes(x1, g_ref[...], rw_ref[...], rb_ref[...])


def _out_proj_even_kernel(x_ref, oa_ref, ob_ref, w_ref, *tail_refs):
    acc = jnp.dot(oa_ref[...].astype(BF16), w_ref[0:A_Q_W, :], preferred_element_type=F32)
    acc = acc + jnp.dot(ob_ref[...].astype(BF16), w_ref[A_Q_W:, :], preferred_element_type=F32)
    _residual_out(x_ref, acc, tail_refs)


def _out_proj_odd_kernel(x_ref, oc_ref, o0_ref, o1_ref, o2_ref, l0_ref, l1_ref, l2_ref, w_ref, *tail_refs):
    l0, l1, l2 = l0_ref[...], l1_ref[...], l2_ref[...]
    m = jnp.maximum(jnp.maximum(l0, l1), l2)
    e0, e1, e2 = jnp.exp(l0 - m), jnp.exp(l1 - m), jnp.exp(l2 - m)
    od = (o0_ref[...] * e0 + o1_ref[...] * e1 + o2_ref[...] * e2) / (e0 + e1 + e2)
    acc = jnp.dot(oc_ref[...].astype(BF16), w_ref[0:C_WIDTH, :], preferred_element_type=F32)
    acc = acc + jnp.dot(od.astype(BF16), w_ref[C_WIDTH:, :], preferred_element_type=F32)
    _residual_out(x_ref, acc, tail_refs)


def _out_proj(kernel, x, parts, w, name, in_place=True, route_prm=None):
    n = x.shape[0]
    tm = 512
    in_specs = ([pl.BlockSpec((tm, D_MODEL), lambda i: (i, 0))]
                + [pl.BlockSpec((tm, p.shape[1]), lambda i: (i, 0)) for p in parts]
                + [pl.BlockSpec(w.shape, lambda i: (0, 0))])
    args = [x, *parts, w]
    width = D_MODEL
    if route_prm is not None:
        for key in ("g", "rw", "rb"):
            in_specs.append(pl.BlockSpec(route_prm[key].shape, lambda i: (0, 0)))
            args.append(route_prm[key])
        width = MOE_AUG_W
    return pl.pallas_call(
        kernel,
        grid=(n // tm,),
        in_specs=in_specs,
        out_specs=pl.BlockSpec((tm, width), lambda i: (i, 0)),
        out_shape=jax.ShapeDtypeStruct((n, width), F32),
        input_output_aliases={0: 0} if in_place and route_prm is None else {},
        compiler_params=_cparams("parallel"),
        name=name,
    )(*args)


MOE_TM = 512


def _ffn_norm(x, g):
    y = x * lax.rsqrt(jnp.mean(x * x, axis=-1, keepdims=True) + RMS_EPS)
    return (y * g).astype(BF16)


def _route(xb, rw, rb):
    logits = jnp.dot(xb, rw, preferred_element_type=F32) + rb
    lane = lax.broadcasted_iota(jnp.int32, logits.shape, 1).astype(F32)
    ninf = float("-inf")
    far = float(ROUTER_LANES)
    lg = jnp.where(lane < MOE_GROUPS, logits, ninf)
    gmax = jnp.max(lg, axis=1, keepdims=True)
    g_idx = jnp.min(jnp.where(lg == gmax, lane, far), axis=1, keepdims=True)
    g_w = 1.0 / jnp.sum(jnp.exp(lg - gmax), axis=1, keepdims=True)
    lane_grp = jnp.floor((lane - MOE_GROUPS) * (1.0 / MOE_PER_GROUP))
    in_grp = (lane >= MOE_GROUPS) & (lane < MOE_GROUPS + MOE_EXPERTS) & (lane_grp == g_idx)
    le = jnp.where(in_grp, logits, ninf)
    v1 = jnp.max(le, axis=1, keepdims=True)
    i1 = jnp.min(jnp.where(le == v1, lane, far), axis=1, keepdims=True)
    le2 = jnp.where(lane == i1, ninf, le)
    v2 = jnp.max(le2, axis=1, keepdims=True)
    i2 = jnp.min(jnp.where(le2 == v2, lane, far), axis=1, keepdims=True)
    e2 = jnp.exp(v2 - v1)
    w1 = g_w / (1.0 + e2)
    w2 = g_w * e2 / (1.0 + e2)
    return lane, g_idx, i1 - MOE_GROUPS, i2 - MOE_GROUPS, w1, w2


def _moe_kernel(x_ref, g_ref, rw_ref, rb_ref, wg_ref, wu_ref, wd_ref, o_ref, xn_ref, gates_ref, acc_ref):
    e = pl.program_id(1)

    @pl.when(e == 0)
    def _():
        xb0 = _ffn_norm(x_ref[...], g_ref[...])
        xn_ref[...] = xb0
        lane, _, e1, e2, w1, w2 = _route(xb0, rw_ref[...], rb_ref[...])
        gates_ref[...] = jnp.where(lane == e1, w1, 0.0) + jnp.where(lane == e2, w2, 0.0)
        acc_ref[...] = jnp.zeros_like(acc_ref)

    xb = xn_ref[...]
    gates = gates_ref[...]
    lane_i = lax.broadcasted_iota(jnp.int32, gates.shape, 1)
    gate = jnp.sum(jnp.where(lane_i == e, gates, 0.0), axis=1, keepdims=True)
    hg = jnp.dot(xb, wg_ref[0], preferred_element_type=F32)
    hu = jnp.dot(xb, wu_ref[0], preferred_element_type=F32)
    hid = jax.nn.silu(hg) * hu * gate
    acc_ref[...] += jnp.dot(hid.astype(BF16), wd_ref[0], preferred_element_type=F32)

    @pl.when(e == MOE_EXPERTS - 1)
    def _():
        o_ref[...] = x_ref[...] + acc_ref[...]


def _moe(x, prm, name):
    n = x.shape[0]
    tm = 2 * MOE_TM if n % (2 * MOE_TM) == 0 else MOE_TM
    return pl.pallas_call(
        _moe_kernel,
        grid=(n // tm, MOE_EXPERTS),
        in_specs=[pl.BlockSpec((tm, D_MODEL), lambda i, e: (i, 0)),
                  pl.BlockSpec((1, D_MODEL), lambda i, e: (0, 0)),
                  pl.BlockSpec((D_MODEL, ROUTER_LANES), lambda i, e: (0, 0)),
                  pl.BlockSpec((1, ROUTER_LANES), lambda i, e: (0, 0)),
                  pl.BlockSpec((1, D_MODEL, MOE_FF), lambda i, e: (e, 0, 0)),
                  pl.BlockSpec((1, D_MODEL, MOE_FF), lambda i, e: (e, 0, 0)),
                  pl.BlockSpec((1, MOE_FF, D_MODEL), lambda i, e: (e, 0, 0))],
        out_specs=pl.BlockSpec((tm, D_MODEL), lambda i, e: (i, 0)),
        out_shape=jax.ShapeDtypeStruct((n, D_MODEL), F32),
        scratch_shapes=[pltpu.VMEM((tm, D_MODEL), BF16),
                        pltpu.VMEM((tm, ROUTER_LANES), F32),
                        pltpu.VMEM((tm, D_MODEL), F32)],
        input_output_aliases={0: 0},
        compiler_params=_cparams("parallel", "arbitrary"),
        name=name,
    )(x, prm["g"], prm["rw"], prm["rb"], prm["wg"], prm["wu"], prm["wd"])


def _moe_params(norm_g, router_g, router_g_b, router_e, router_e_b, w_gate, w_up, w_down):
    used = MOE_GROUPS + MOE_EXPERTS
    rw = jnp.concatenate([router_g.astype(F32), router_e.astype(F32),
                          jnp.zeros((D_MODEL, ROUTER_LANES - used), F32)], axis=1)
    rb = jnp.concatenate([router_g_b.astype(F32), router_e_b.astype(F32),
                          jnp.zeros((ROUTER_LANES - used,), F32)]).reshape(1, ROUTER_LANES)
    return {"g": norm_g.astype(F32).reshape(1, D_MODEL), "rw": rw.astype(BF16), "rb": rb,
            "wg": w_gate.astype(BF16), "wu": w_up.astype(BF16), "wd": w_down.astype(BF16)}


MOE_AUG_W = D_MODEL + ROUTER_LANES
GATE_LANE0 = 1


def _route_lanes(x, g, rw, rb):
    lane, g_idx, e1, e2, w1, w2 = _route(_ffn_norm(x, g), rw, rb)
    l1 = e1 - g_idx * MOE_PER_GROUP + GATE_LANE0
    l2 = e2 - g_idx * MOE_PER_GROUP + GATE_LANE0
    return jnp.where(lane == 0.0, g_idx, 0.0) + jnp.where(lane == l1, w1, 0.0) + jnp.where(lane == l2, w2, 0.0)


def _moe_plan(group_col, n):
    tm = MOE_TM
    n_tiles_max = n // tm + MOE_GROUPS
    g = group_col.astype(jnp.int32)
    onehot = (g[:, None] == jnp.arange(MOE_GROUPS)[None, :]).astype(jnp.int32)
    counts = jnp.sum(onehot, axis=0)
    rank = jnp.sum((jnp.cumsum(onehot, axis=0) - onehot) * onehot, axis=1)
    padded = (counts + tm - 1) // tm * tm
    ends = jnp.cumsum(padded)
    starts = ends - padded
    pos = jnp.sum(onehot * starts[None, :], axis=1) + rank
    src = jnp.zeros((n_tiles_max * tm,), jnp.int32).at[pos].set(jnp.arange(n, dtype=jnp.int32))
    tile0 = jnp.arange(n_tiles_max, dtype=jnp.int32) * tm
    tile_group = jnp.minimum(jnp.sum((tile0[:, None] >= ends[None, :]).astype(jnp.int32), axis=1),
                             MOE_GROUPS - 1)
    real = jnp.clip(counts[tile_group] - (tile0 - starts[tile_group]), 0, tm)
    real = jnp.where(tile0 < ends[-1], real, 0).astype(jnp.int32)
    return src, tile_group.astype(jnp.int32), real, (ends[-1:] // tm).astype(jnp.int32)


def _moe_expert_kernel(src_ref, grp_ref, real_ref, nt_ref, xa_hbm, g_ref, pg_ref, wg_ref, wu_ref, wd_ref,
                       out_hbm, xbuf, ybuf, gsem, ssem, *, post_norm):
    del grp_ref
    tm = MOE_TM
    i = pl.program_id(0)
    n_tiles = nt_ref[0]
    slot = lax.rem(i, 2)
    sub = SUBLANES

    def row_in(tile, s, j, u):
        row = src_ref[tile * tm + j * sub + u]
        return pltpu.make_async_copy(xa_hbm.at[row >> 3, pl.ds(row & 7, 1)],
                                     xbuf.at[s, j, pl.ds(u, 1)], gsem.at[s])

    def row_out(tile, s, j, u):
        row = src_ref[tile * tm + j * sub + u]
        return pltpu.make_async_copy(ybuf.at[s, j, pl.ds(u, 1)],
                                     out_hbm.at[row >> 3, pl.ds(row & 7, 1)], ssem.at[s])

    def start_gather(tile, s):
        def body(j, c):
            for u in range(sub):
                row_in(tile, s, j, u).start()
            return c
        lax.fori_loop(0, tm // sub, body, 0)

    def wait_gather(s):
        pltpu.make_async_copy(xa_hbm.at[pl.ds(0, tm // sub)], xbuf.at[s], gsem.at[s]).wait()

    def start_scatter(tile, s):
        real = real_ref[tile]
        full = real // sub

        def body(j, c):
            for u in range(sub):
                row_out(tile, s, j, u).start()
            return c
        lax.fori_loop(0, full, body, 0)

        def tail(r, c):
            row_out(tile, s, full, r - full * sub).start()
            return c
        lax.fori_loop(full * sub, real, tail, 0)

    def wait_scatter(tile, s):
        real = real_ref[tile]
        for k in range(tm.bit_length()):
            size = 1 << k

            @pl.when((real >> k) & 1 == 1)
            def _():
                if size >= sub:
                    piece = (ybuf.at[s, pl.ds(0, size // sub)], out_hbm.at[pl.ds(0, size // sub)])
                else:
                    piece = (ybuf.at[s, 0, pl.ds(0, size)], out_hbm.at[0, pl.ds(0, size)])
                pltpu.make_async_copy(piece[0], piece[1], ssem.at[s]).wait()

    @pl.when(i < n_tiles)
    def _():
        @pl.when(i == 0)
        def _():
            start_gather(0, 0)

        @pl.when(i + 1 < n_tiles)
        def _():
            start_gather(i + 1, 1 - slot)

        wait_gather(slot)

        @pl.when(i >= 2)
        def _():
            wait_scatter(i - 2, slot)

        xa = xbuf[slot].reshape(tm, MOE_AUG_W)
        x = xa[:, 0:D_MODEL]
        route = xa[:, D_MODEL:]
        xb = _ffn_norm(x, g_ref[...])
        acc = jnp.zeros((tm, D_MODEL), F32)
        for e in range(MOE_PER_GROUP):
            gate = route[:, GATE_LANE0 + e:GATE_LANE0 + e + 1]
            hg = jnp.dot(xb, wg_ref[0, e], preferred_element_type=F32)
            hu = jnp.dot(xb, wu_ref[0, e], preferred_element_type=F32)
            hid = jax.nn.silu(hg) * hu * gate
            acc = acc + jnp.dot(hid.astype(BF16), wd_ref[0, e], preferred_element_type=F32)
        y = x + acc
        if post_norm:
            y = y * lax.rsqrt(jnp.mean(y * y, axis=-1, keepdims=True) + RMS_EPS) * pg_ref[...]
        ybuf[slot] = y.reshape(tm // sub, sub, D_MODEL)
        start_scatter(i, slot)

        @pl.when(i == n_tiles - 1)
        def _():
            wait_scatter(i, slot)

            @pl.when(i >= 1)
            def _():
                wait_scatter(i - 1, 1 - slot)


def _moe_experts(xa, plan, prm, name, post_gain=None):
    n = xa.shape[0]
    post = prm["g"] if post_gain is None else post_gain.astype(F32).reshape(1, D_MODEL)
    tm = MOE_TM
    sub = SUBLANES
    src, tile_group, real, n_tiles = plan
    grouped = lambda w: w.reshape(MOE_GROUPS, MOE_PER_GROUP, w.shape[1], w.shape[2])
    w_spec = lambda a, b: pl.BlockSpec((1, MOE_PER_GROUP, a, b), lambda i, src, grp, real, nt: (grp[i], 0, 0, 0))
    grid_spec = pltpu.PrefetchScalarGridSpec(
        num_scalar_prefetch=4,
        grid=(n // tm + MOE_GROUPS,),
        in_specs=[pl.BlockSpec(memory_space=pl.ANY),
                  pl.BlockSpec((1, D_MODEL), lambda i, *_: (0, 0)),
                  pl.BlockSpec((1, D_MODEL), lambda i, *_: (0, 0)),
                  w_spec(D_MODEL, MOE_FF), w_spec(D_MODEL, MOE_FF), w_spec(MOE_FF, D_MODEL)],
        out_specs=pl.BlockSpec(memory_space=pl.ANY),
        scratch_shapes=[pltpu.VMEM((2, tm // sub, sub, MOE_AUG_W), F32),
                        pltpu.VMEM((2, tm // sub, sub, D_MODEL), F32),
                        pltpu.SemaphoreType.DMA((2,)),
                        pltpu.SemaphoreType.DMA((2,))])
    out = pl.pallas_call(
        functools.partial(_moe_expert_kernel, post_norm=post_gain is not None),
        grid_spec=grid_spec,
        out_shape=jax.ShapeDtypeStruct((n // sub, sub, D_MODEL), F32),
        compiler_params=_cparams("arbitrary"),
        name=name,
    )(src, tile_group, real, n_tiles, xa.reshape(n // sub, sub, MOE_AUG_W), prm["g"], post,
      grouped(prm["wg"]), grouped(prm["wu"]), grouped(prm["wd"]))
    return out.reshape(n, D_MODEL)


def _final_norm_kernel(x_ref, g_ref, o_ref):
    x = x_ref[...]
    y = x * lax.rsqrt(jnp.mean(x * x, axis=-1, keepdims=True) + RMS_EPS)
    o_ref[...] = y * g_ref[...]


def _final_norm(x, g, name):
    n = x.shape[0]
    tm = 512
    return pl.pallas_call(
        _final_norm_kernel,
        grid=(n // tm,),
        in_specs=[pl.BlockSpec((tm, D_MODEL), lambda i: (i, 0)),
                  pl.BlockSpec((1, D_MODEL), lambda i: (0, 0))],
        out_specs=pl.BlockSpec((tm, D_MODEL), lambda i: (i, 0)),
        out_shape=jax.ShapeDtypeStruct((n, D_MODEL), F32),
        compiler_params=_cparams("parallel"),
        name=name,
    )(x, g.astype(F32).reshape(1, D_MODEL))


def _t5_bucket(dist):
    n = jnp.maximum(dist, 0)
    max_exact = T5_BUCKETS // 2
    nf = jnp.maximum(n, 1).astype(F32)
    large = max_exact + (jnp.log(nf / max_exact) / math.log(T5_MAX_DIST / max_exact)
                         * (T5_BUCKETS - max_exact)).astype(jnp.int32)
    return jnp.where(n < max_exact, n, jnp.minimum(large, T5_BUCKETS - 1))


def _rel_bias(table, dist):
    hit = _t5_bucket(dist)[..., None] == jnp.arange(T5_BUCKETS)
    rows = jnp.sum(jnp.where(hit[..., None], table.astype(F32), 0.0), axis=-2)
    return jnp.moveaxis(rows, -1, 0)


def _band_bias_mask(table, max_dist, dil):
    r = jnp.arange(BLK)[:, None]
    s = jnp.arange(2 * BLK)[None, :]
    dist = BLK + r - s
    valid = (dist >= 0) & (dist <= max_dist)
    bias = _rel_bias(table, dist * dil)
    later = jnp.where(valid[None], bias, NEG)
    first = jnp.where((valid & (s >= BLK))[None], bias, NEG)
    return jnp.stack([first, later])


def _swa_sample_bias_mask(table):
    s = jnp.arange(DEC_SEQ)[:, None]
    col = jnp.arange(BLK)[None, :]
    dist_c = A_WINDOW + s - col
    dist_n = s - col
    tab = table[:, :A_Q_HEADS]
    bc = jnp.where(((dist_c >= 0) & (dist_c < A_WINDOW))[None], _rel_bias(tab, dist_c), NEG)
    bn = jnp.where(((dist_n >= 0) & (col < DEC_SEQ))[None], _rel_bias(tab, dist_n), NEG)
    return bc.reshape(A_Q_HEADS * DEC_SEQ, A_WINDOW), bn.reshape(A_Q_HEADS * DEC_SEQ, BLK)


def _dil_sample_bias_mask(table):
    s = jnp.arange(DEC_SEQ)[:, None]
    col = jnp.arange(BLK)[None, :]
    bcs, bns = [], []
    for g, (win, dil) in enumerate(D_PAIRS):
        lo = A_Q_HEADS + g * D_HEADS_PER
        tab = table[:, lo:lo + D_HEADS_PER]
        dist_c = win + s - jnp.arange(win)[None, :]
        valid_c = (dist_c >= 0) & (dist_c % dil == 0) & (dist_c <= win)
        bc = jnp.where(valid_c[None], _rel_bias(tab, dist_c), NEG)
        dist_n = s - col
        valid_n = (dist_n >= 0) & (dist_n % dil == 0) & (col < DEC_SEQ)
        bn = jnp.where(valid_n[None], _rel_bias(tab, dist_n), NEG)
        bcs.append(bc.reshape(1, D_HEADS_PER * DEC_SEQ, win))
        bns.append(bn.reshape(D_HEADS_PER * DEC_SEQ, BLK))
    return bcs, jnp.stack(bns)


def _native_cache(c):
    n_layers, nb, wc = c.shape[:3]
    return c.transpose(0, 1, 3, 4, 5, 2).reshape(n_layers, nb, -1, wc)


def _logical_cache(c, heads):
    n_layers, nb, _, wc = c.shape
    return c.reshape(n_layers, nb, 2, heads, HEAD_DIM, wc).transpose(0, 1, 5, 2, 3, 4)


def kernel(x_prompt, x_sample, cache_a_kv, state_b, state_c_h, state_c_conv, cache_d_g0, cache_d_g1, cache_d_g2, rel_table, norm_mix, norm_ffn, norm_final, w_in_even, w_out_even, sinks_a, s5_a_re, s5_a_im, s5_log_dt, s5_b_re, s5_b_im, s5_c_re, s5_c_im, s5_d, s5_glu_w, s5_glu_b, w_in_odd, w_out_odd, conv_w, conv_b, gate_a_w, gate_a_b, gate_x_w, gate_x_b, lru_lambda, moe_router_g, moe_router_g_b, moe_router_e, moe_router_e_b, moe_w_gate, moe_w_up, moe_w_down):
    xp = x_prompt.astype(F32).reshape(NP_TOK, D_MODEL)
    xs = x_sample.astype(F32).reshape(NS_TOK, D_MODEL)

    cache_a = _native_cache(cache_a_kv)
    caches_d = [_native_cache(c) for c in (cache_d_g0, cache_d_g1, cache_d_g2)]

    bm_a = _band_bias_mask(rel_table[:, :A_Q_HEADS], A_WINDOW - 1, 1)
    bm_d = [_band_bias_mask(rel_table[:, A_Q_HEADS + g * D_HEADS_PER:A_Q_HEADS + (g + 1) * D_HEADS_PER],
                            win // dil, dil) for g, (win, dil) in enumerate(D_PAIRS)]
    bmc_a_s, bmn_a_s = _swa_sample_bias_mask(rel_table)
    bcs_d_s, bn_d_s = _dil_sample_bias_mask(rel_table)

    a_p, a_new, b_p, b_s = [], [], [], []
    ch_p, ch_s, cc_p, cc_s = [], [], [], []
    d_p = [[], [], []]
    d_new = [[], [], []]

    for layer in range(DEPTH):
        mp = _moe_params(norm_ffn[layer], moe_router_g[layer], moe_router_g_b[layer], moe_router_e[layer],
                         moe_router_e_b[layer], moe_w_gate[layer], moe_w_up[layer], moe_w_down[layer])
        if layer % 2 == 0:
            e = layer // 2
            w_in = w_in_even[e].astype(BF16)
            w_out = w_out_even[e].astype(BF16)
            splits = (A_Q_W, 2 * A_KV_W, B_WIDTH)
            s5p = _s5_params(s5_a_re[e], s5_a_im[e], s5_log_dt[e], s5_b_re[e], s5_b_im[e],
                             s5_c_re[e], s5_c_im[e], s5_d[e], s5_glu_w[e], s5_glu_b[e])
            sinks = sinks_a[e].astype(F32)

            q_p, kv_p, u_p = _norm_proj(xp, norm_mix[layer], w_in, splits, f"in_proj_p{layer}")
            q_s, kv_s, u_s = _norm_proj(xs, norm_mix[layer], w_in, splits, f"in_proj_s{layer}")

            oa_p = _swa_prompt(q_p, kv_p, sinks, bm_a)
            oa_s = _swa_sample(q_s, kv_s, cache_a, e, sinks, bmc_a_s, bmn_a_s)

            ob_p, hl_p = _s5_mixer(u_p, jnp.zeros((BATCH, B_STATE_W), F32), s5p, BATCH, SEQ, 64, "s5_prompt")
            ob_s, hl_s = _s5_mixer(u_s, _s5_state_to_cols(state_b[e]), s5p, DEC_BATCH, DEC_SEQ, DEC_SEQ,
                                   "s5_sample")

            xa_p = _out_proj(_out_proj_even_kernel, xp, [oa_p, ob_p], w_out, f"out_proj_p{layer}",
                             route_prm=mp)
            xs = _out_proj(_out_proj_even_kernel, xs, [oa_s, ob_s], w_out, f"out_proj_s{layer}",
                           in_place=layer > 0)

            a_p.append(kv_p.reshape(BATCH, SEQ, 2, A_KV_HEADS, HEAD_DIM)[:, SEQ - A_WINDOW:])
            a_new.append(kv_s.reshape(DEC_BATCH, DEC_SEQ, 2 * A_KV_W))
            b_p.append(_s5_cols_to_state(hl_p))
            b_s.append(_s5_cols_to_state(hl_s))
        else:
            o = layer // 2
            w_in = w_in_odd[o].astype(BF16)
            w_out = w_out_odd[o].astype(BF16)
            splits = (C_WIDTH, C_WIDTH, D_QKV_W, D_QKV_W, D_QKV_W)
            lrp = _rglru_params(conv_w[o], conv_b[o], gate_a_w[o], gate_a_b[o], gate_x_w[o], gate_x_b[o],
                                lru_lambda[o])

            xr_p, gate_p, q_p, k_p, v_p = _norm_proj(xp, norm_mix[layer], w_in, splits, f"in_proj_p{layer}")
            xr_s, gate_s, q_s, k_s, v_s = _norm_proj(xs, norm_mix[layer], w_in, splits, f"in_proj_s{layer}")

            od_p, lse_p = [], []
            for g, (win, dil) in enumerate(D_PAIRS):
                og, lg = _dilated_prompt_group(q_p, k_p, v_p, bm_d[g], g, dil)
                od_p.append(og)
                lse_p.append(lg)
            k_s4 = k_s.reshape(DEC_BATCH, DEC_SEQ, D_N_GROUPS, D_GROUP_W)
            v_s4 = v_s.reshape(DEC_BATCH, DEC_SEQ, D_N_GROUPS, D_GROUP_W)
            for g in range(D_N_GROUPS):
                d_new[g].append(jnp.concatenate([k_s4[:, :, g], v_s4[:, :, g]], axis=-1))
            if o < caches_d[0].shape[0] - 1:
                od_s, lse_s = _dilated_sample(q_s, k_s, v_s, caches_d, o, bcs_d_s, bn_d_s)
            else:
                od_s, lse_s, shifted_d = _dilated_sample_and_shift(
                    q_s, k_s, v_s, caches_d, [jnp.stack(rows) for rows in d_new], bcs_d_s, bn_d_s)

            oc_p, hc_p = _rglru_mixer(xr_p, gate_p, jnp.zeros(((C_CONV - 1) * BATCH, C_WIDTH), F32),
                                      jnp.zeros((BATCH, C_WIDTH), F32), lrp, BATCH, SEQ, 128, "rglru_prompt")
            conv_s = state_c_conv[o].astype(F32).transpose(1, 0, 2).reshape((C_CONV - 1) * DEC_BATCH, C_WIDTH)
            oc_s, hc_s = _rglru_mixer(xr_s, gate_s, conv_s, state_c_h[o].astype(F32), lrp,
                                      DEC_BATCH, DEC_SEQ, DEC_SEQ, "rglru_sample")

            xa_p = _out_proj(_out_proj_odd_kernel, xp, [oc_p] + od_p + lse_p, w_out, f"out_proj_p{layer}",
                             route_prm=mp)
            xs = _out_proj(_out_proj_odd_kernel, xs, [oc_s] + list(od_s) + list(lse_s), w_out,
                           f"out_proj_s{layer}")

            ch_p.append(hc_p)
            ch_s.append(hc_s)
            cc_p.append(xr_p.reshape(BATCH, SEQ, C_WIDTH)[:, SEQ - (C_CONV - 1):])
            cc_s.append(xr_s.reshape(DEC_BATCH, DEC_SEQ, C_WIDTH)[:, DEC_SEQ - (C_CONV - 1):])
            k_p5 = k_p.reshape(BATCH, SEQ, D_N_GROUPS, D_HEADS_PER, HEAD_DIM)
            v_p5 = v_p.reshape(BATCH, SEQ, D_N_GROUPS, D_HEADS_PER, HEAD_DIM)
            for g, (win, dil) in enumerate(D_PAIRS):
                wc = min(win, SEQ)
                d_p[g].append(jnp.stack([k_p5[:, SEQ - wc:, g], v_p5[:, SEQ - wc:, g]], axis=2))

        xp = _moe_experts(xa_p, _moe_plan(xa_p[:, D_MODEL], NP_TOK), mp, f"moe_experts_p{layer}",
                          post_gain=norm_final if layer == DEPTH - 1 else None)
        xs = _moe(xs, mp, f"moe_s{layer}")

    y_prompt = xp.reshape(BATCH, SEQ, D_MODEL)
    y_sample = _final_norm(xs, norm_final, "final_norm_s").reshape(DEC_BATCH, DEC_SEQ, D_MODEL)

    new_a = _logical_cache(_shift_cache(cache_a, jnp.stack(a_new), "shift_cache_a"), A_KV_HEADS)
    new_d = [_logical_cache(c, D_HEADS_PER) for c in shifted_d]

    return (y_prompt, y_sample,
            jnp.stack(a_p), new_a, jnp.stack(b_p), jnp.stack(b_s),
            jnp.stack(ch_p), jnp.stack(ch_s), jnp.stack(cc_p), jnp.stack(cc_s),
            jnp.stack(d_p[0]), new_d[0], jnp.stack(d_p[1]), new_d[1], jnp.stack(d_p[2]), new_d[2])
```

```python
import functools
import math

import jax
import jax.numpy as jnp
from jax import lax
from jax.experimental import pallas as pl
from jax.experimental.pallas import tpu as pltpu

F32 = jnp.float32
BF16 = jnp.bfloat16

D_MODEL = 1024
BATCH = 8
SEQ = 2048
DEPTH = 4
DEC_BATCH = 128
DEC_SEQ = 8
HEAD_DIM = 64
BLK = 128
RMS_EPS = 1e-6
NEG = -1e30

A_Q_HEADS = 8
A_KV_HEADS = 2
A_GQA = 4
A_WINDOW = 128
A_Q_W = A_Q_HEADS * HEAD_DIM
A_KV_W = A_KV_HEADS * HEAD_DIM

B_WIDTH = 512
B_GROUP_CH = 16
B_GROUPS = 32
B_STATE = 64
B_LANE_BLOCKS = 4
B_GROUPS_PER_BLOCK = B_GROUPS // B_LANE_BLOCKS
B_HALF = B_GROUPS_PER_BLOCK * B_STATE
B_STATE_W = B_LANE_BLOCKS * 2 * B_HALF

C_WIDTH = 512
C_BLOCKS = 8
C_BLOCK_W = 64
C_CONV = 4
C_POWER = 8.0

D_PAIRS = ((128, 1), (512, 4), (2048, 16))
D_N_GROUPS = 3
D_HEADS_PER = 4
D_GROUP_W = D_HEADS_PER * HEAD_DIM
D_QKV_W = D_N_GROUPS * D_GROUP_W

T5_BUCKETS = 32
T5_MAX_DIST = 2048

MOE_GROUPS = 4
MOE_PER_GROUP = 4
MOE_EXPERTS = 16
MOE_FF = 256
ROUTER_LANES = 128

NP_TOK = BATCH * SEQ
NS_TOK = DEC_BATCH * DEC_SEQ

VMEM_LIMIT_BYTES = 52 * 2 ** 20


def _cparams(*sem):
    return pltpu.CompilerParams(dimension_semantics=sem, vmem_limit_bytes=VMEM_LIMIT_BYTES)


def _nt_dot(a, b):
    return lax.dot_general(a, b, (((1,), (1,)), ((), ())), preferred_element_type=F32)


LANES = 128
SUBLANES = 8


def _norm_proj_kernel(x_ref, g_ref, w_ref, *out_refs, splits):
    x = x_ref[...]
    y = x * lax.rsqrt(jnp.mean(x * x, axis=-1, keepdims=True) + RMS_EPS)
    xn = (y * g_ref[...]).astype(BF16)
    off = 0
    for o_ref, width in zip(out_refs, splits):
        o_ref[...] = jnp.dot(xn, w_ref[:, off:off + width], preferred_element_type=F32)
        off += width


def _norm_proj(x, g, w, splits, name):
    n = x.shape[0]
    tm = 512
    return pl.pallas_call(
        functools.partial(_norm_proj_kernel, splits=splits),
        grid=(n // tm,),
        in_specs=[pl.BlockSpec((tm, D_MODEL), lambda i: (i, 0)),
                  pl.BlockSpec((1, D_MODEL), lambda i: (0, 0)),
                  pl.BlockSpec(w.shape, lambda i: (0, 0))],
        out_specs=[pl.BlockSpec((tm, s), lambda i: (i, 0)) for s in splits],
        out_shape=[jax.ShapeDtypeStruct((n, s), F32) for s in splits],
        compiler_params=_cparams("parallel"),
        name=name,
    )(x, g.reshape(1, D_MODEL), w)


BAND_NSUB = 4


def _softmax_parts(scores, sink):
    m = jnp.max(scores[0], axis=1, keepdims=True)
    for s in scores[1:]:
        m = jnp.maximum(m, jnp.max(s, axis=1, keepdims=True))
    if sink is not None:
        m = jnp.maximum(m, sink)
    ps = [jnp.exp(s - m) for s in scores]
    den = jnp.sum(ps[0], axis=1, keepdims=True)
    for p in ps[1:]:
        den = den + jnp.sum(p, axis=1, keepdims=True)
    if sink is not None:
        den = den + jnp.exp(sink - m)
    inv = 1.0 / den
    return [(p * inv).astype(BF16) for p in ps], den, m


def _band_attn_kernel(*refs, n_heads, gqa, with_sink, dil, has_prev, nsub=1):
    refs = list(refs)
    sink_ref = refs.pop(0) if with_sink else None
    q_ref, kc_ref = refs.pop(0), refs.pop(0)
    kp_ref = refs.pop(0) if has_prev else None
    vc_ref = refs.pop(0)
    vp_ref = refs.pop(0) if has_prev else None
    bm_ref = refs.pop(0)
    bm_later_ref = refs.pop(0) if nsub > 1 else bm_ref
    o_ref = refs.pop(0)
    lse_ref = refs.pop(0) if refs else None
    n_kv = n_heads // gqa
    if dil == 1:
        row_sel = [slice(s * BLK, (s + 1) * BLK) for s in range(nsub)]
    else:
        row_sel = [pl.ds(r, BLK, stride=dil) for r in range(dil)]
    bm_of = [bm_ref if (dil > 1 or s == 0) else bm_later_ref for s in range(len(row_sel))]
    sc, sp, v_cur, v_prev = [], [], [], []
    kc = vc = None
    for s, rows in enumerate(row_sel):
        q = (q_ref[rows, :] * HEAD_DIM ** -0.5).astype(BF16)
        if has_prev:
            if dil == 1 and s > 0:
                kp, vp = kc, vc
            elif dil == 1:
                kp, vp = kp_ref[...].astype(BF16), vp_ref[...].astype(BF16)
            else:
                kp, vp = kp_ref[rows, :].astype(BF16), vp_ref[rows, :].astype(BF16)
        kc = kc_ref[rows, :].astype(BF16)
        vc = vc_ref[rows, :].astype(BF16)
        for hk in range(n_kv):
            ksl = slice(hk * HEAD_DIM, (hk + 1) * HEAD_DIM)
            qs = jnp.concatenate([q[:, h * HEAD_DIM:(h + 1) * HEAD_DIM]
                                  for h in range(hk * gqa, (hk + 1) * gqa)], axis=0)
            sc.append(_nt_dot(qs, kc[:, ksl]))
            v_cur.append(vc[:, ksl])
            if has_prev:
                sp.append(_nt_dot(qs, kp[:, ksl]))
                v_prev.append(vp[:, ksl])
    n_units = len(row_sel)
    bias_c = jnp.concatenate([bm[0, h, :, BLK:2 * BLK] for bm in bm_of for h in range(n_heads)], axis=0)
    scores = [jnp.concatenate(sc, axis=0) + bias_c]
    if has_prev:
        bias_p = jnp.concatenate([bm[0, h, :, 0:BLK] for bm in bm_of for h in range(n_heads)], axis=0)
        scores.append(jnp.concatenate(sp, axis=0) + bias_p)
    sink = None
    if with_sink:
        sink = jnp.concatenate([jnp.full((BLK, 1), sink_ref[h], F32) for h in range(n_heads)] * n_units,
                               axis=0)
    ps, den, m = _softmax_parts(scores, sink)
    lse = m + jnp.log(den) if lse_ref is not None else None
    unit_rows = gqa * BLK
    for r, rows in enumerate(row_sel):
        outs = []
        for hk in range(n_kv):
            u = r * n_kv + hk
            usl = slice(u * unit_rows, (u + 1) * unit_rows)
            o = jnp.dot(ps[0][usl, :], v_cur[u], preferred_element_type=F32)
            if has_prev:
                o = o + jnp.dot(ps[1][usl, :], v_prev[u], preferred_element_type=F32)
            outs += [o[j * BLK:(j + 1) * BLK, :] for j in range(gqa)]
        o_ref[rows, :] = jnp.concatenate(outs, axis=1)
        if lse_ref is not None:
            base = r * n_heads * BLK
            lse_ref[rows, :] = jnp.concatenate(
                [jnp.broadcast_to(lse[base + h * BLK:base + (h + 1) * BLK, :], (BLK, HEAD_DIM))
                 for h in range(n_heads)], axis=1)


def _swa_prompt(q, kv, sinks, bm):
    nsub = BAND_NSUB
    nstep = SEQ // (BLK * nsub)
    row = lambda b, i: b * nstep + i
    prev = lambda b, i: (b * nstep + i) * nsub - jnp.minimum(i, 1)
    bm_spec = lambda pick: pl.BlockSpec((1, A_Q_HEADS, BLK, 2 * BLK), lambda b, i: (pick(i), 0, 0, 0))
    return pl.pallas_call(
        functools.partial(_band_attn_kernel, n_heads=A_Q_HEADS, gqa=A_GQA, with_sink=True, dil=1,
                          has_prev=True, nsub=nsub),
        grid=(BATCH, nstep),
        in_specs=[pl.BlockSpec(memory_space=pltpu.SMEM),
                  pl.BlockSpec((BLK * nsub, A_Q_W), lambda b, i: (row(b, i), 0)),
                  pl.BlockSpec((BLK * nsub, A_KV_W), lambda b, i: (row(b, i), 0)),
                  pl.BlockSpec((BLK, A_KV_W), lambda b, i: (prev(b, i), 0)),
                  pl.BlockSpec((BLK * nsub, A_KV_W), lambda b, i: (row(b, i), 1)),
                  pl.BlockSpec((BLK, A_KV_W), lambda b, i: (prev(b, i), 1)),
                  bm_spec(lambda i: jnp.minimum(i, 1)), bm_spec(lambda i: 1)],
        out_specs=pl.BlockSpec((BLK * nsub, A_Q_W), lambda b, i: (row(b, i), 0)),
        out_shape=jax.ShapeDtypeStruct((NP_TOK, A_Q_W), F32),
        compiler_params=_cparams("parallel", "parallel"),
        name="swa_prompt",
    )(sinks, q, kv, kv, kv, kv, bm, bm)


def _dilated_prompt_group(q, k, v, bm, g, dil):
    nsub = BAND_NSUB if dil == 1 else 1
    rows = BLK * dil * nsub
    nchunk = SEQ // rows
    has_prev = SEQ // dil > BLK
    pair = LANES // HEAD_DIM if dil > 1 else D_HEADS_PER
    npair = D_HEADS_PER // pair
    width = pair * HEAD_DIM
    row = lambda b, i: b * nchunk + i
    prev_rows = rows // nsub
    prev = lambda b, i: (b * nchunk + i) * nsub - jnp.minimum(i, 1)
    cur_spec = pl.BlockSpec((rows, width), lambda b, i, p: (row(b, i), g * npair + p))
    prev_spec = pl.BlockSpec((prev_rows, width), lambda b, i, p: (prev(b, i), g * npair + p))
    out_spec = pl.BlockSpec((rows, width), lambda b, i, p: (row(b, i), p))
    bm_spec = lambda pick: pl.BlockSpec((1, pair, BLK, 2 * BLK), lambda b, i, p: (pick(i), p, 0, 0))
    bm_specs, bm_args = [bm_spec(lambda i: jnp.minimum(i, 1))], [bm]
    if nsub > 1:
        bm_specs.append(bm_spec(lambda i: 1))
        bm_args.append(bm)
    if has_prev:
        in_specs, args = [cur_spec, cur_spec, prev_spec, cur_spec, prev_spec] + bm_specs, (q, k, k, v, v, *bm_args)
    else:
        in_specs, args = [cur_spec, cur_spec, cur_spec] + bm_specs, (q, k, v, *bm_args)
    return pl.pallas_call(
        functools.partial(_band_attn_kernel, n_heads=pair, gqa=1, with_sink=False, dil=dil,
                          has_prev=has_prev, nsub=nsub),
        grid=(BATCH, nchunk, npair),
        in_specs=in_specs,
        out_specs=[out_spec, out_spec],
        out_shape=[jax.ShapeDtypeStruct((NP_TOK, D_GROUP_W), F32)] * 2,
        compiler_params=_cparams("parallel", "parallel", "parallel"),
        name=f"dilated_prompt_g{g}",
    )(*args)


SWA_SAMPLE_BB = 8


def _pad_rows(x, rows):
    return jnp.concatenate([x, jnp.zeros((rows - x.shape[0], x.shape[1]), x.dtype)], axis=0)


def _swa_sample_kernel(sink_ref, q_ref, kv_ref, c_ref, bmc_ref, bmn_ref, o_ref):
    for bb in range(SWA_SAMPLE_BB):
        rs = slice(bb * DEC_SEQ, (bb + 1) * DEC_SEQ)
        q = (q_ref[rs, :] * HEAD_DIM ** -0.5).astype(BF16)
        kvn = _pad_rows(kv_ref[rs, :], BLK).astype(BF16)
        cache = c_ref[0, bb].astype(BF16)
        outs = [None] * A_Q_HEADS
        for hk in range(A_KV_HEADS):
            heads = range(hk * A_GQA, (hk + 1) * A_GQA)
            qs = jnp.concatenate([q[:, h * HEAD_DIM:(h + 1) * HEAD_DIM] for h in heads], axis=0)
            ksl = slice(hk * HEAD_DIM, (hk + 1) * HEAD_DIM)
            vsl = slice(A_KV_W + hk * HEAD_DIM, A_KV_W + (hk + 1) * HEAD_DIM)
            brow = slice(hk * A_GQA * DEC_SEQ, (hk + 1) * A_GQA * DEC_SEQ)
            s1 = jnp.dot(qs, cache[ksl, :], preferred_element_type=F32) + bmc_ref[brow, :]
            s2 = _nt_dot(qs, kvn[:, ksl]) + bmn_ref[brow, :]
            sink = jnp.concatenate(
                [jnp.full((DEC_SEQ, 1), sink_ref[h], F32) for h in heads], axis=0)
            (p1, p2), den, _ = _softmax_parts([s1, s2], sink)
            o = _nt_dot(p1, cache[vsl, :]) + jnp.dot(p2, kvn[:, vsl], preferred_element_type=F32)
            for j, h in enumerate(heads):
                outs[h] = o[j * DEC_SEQ:(j + 1) * DEC_SEQ, :]
        o_ref[rs, :] = jnp.concatenate(outs, axis=1)


def _swa_sample(q, kv, cache_all, layer, sinks, bmc, bmn):
    bb = SWA_SAMPLE_BB
    return pl.pallas_call(
        _swa_sample_kernel,
        grid=(DEC_BATCH // bb,),
        in_specs=[pl.BlockSpec(memory_space=pltpu.SMEM),
                  pl.BlockSpec((bb * DEC_SEQ, A_Q_W), lambda i: (i, 0)),
                  pl.BlockSpec((bb * DEC_SEQ, 2 * A_KV_W), lambda i: (i, 0)),
                  pl.BlockSpec((1, bb, 2 * A_KV_W, A_WINDOW), lambda i: (layer, i, 0, 0)),
                  pl.BlockSpec(bmc.shape, lambda i: (0, 0)),
                  pl.BlockSpec(bmn.shape, lambda i: (0, 0))],
        out_specs=pl.BlockSpec((bb * DEC_SEQ, A_Q_W), lambda i: (i, 0)),
        out_shape=jax.ShapeDtypeStruct((NS_TOK, A_Q_W), F32),
        compiler_params=_cparams("parallel"),
        name="swa_sample",
    )(sinks, q, kv, cache_all, bmc, bmn)


def _dil_sample_kernel(q_ref, k_ref, v_ref, c0_ref, c1_ref, c2_ref,
                       bc0_ref, bc1_ref, bc2_ref, bn_ref,
                       o0_ref, o1_ref, o2_ref, l0_ref, l1_ref, l2_ref):
    q = q_ref[...] * HEAD_DIM ** -0.5
    k = k_ref[...]
    v = v_ref[...]
    nrow = D_HEADS_PER * DEC_SEQ
    row_head = lax.broadcasted_iota(jnp.int32, (nrow, D_GROUP_W), 0) // DEC_SEQ
    lane_head = lax.broadcasted_iota(jnp.int32, (nrow, D_GROUP_W), 1) // HEAD_DIM
    head_mask = row_head == lane_head
    out_lane_head = lax.broadcasted_iota(jnp.int32, (DEC_SEQ, D_GROUP_W), 1) // HEAD_DIM
    groups = ((c0_ref, bc0_ref, o0_ref, l0_ref), (c1_ref, bc1_ref, o1_ref, l1_ref),
              (c2_ref, bc2_ref, o2_ref, l2_ref))
    for g, (c_ref, bc_ref, o_ref, l_ref) in enumerate(groups):
        gsl = slice(g * D_GROUP_W, (g + 1) * D_GROUP_W)
        qbd = jnp.where(head_mask, jnp.concatenate([q[:, gsl]] * D_HEADS_PER, axis=0), 0.0).astype(BF16)
        kn = _pad_rows(k[:, gsl], BLK).astype(BF16)
        vn = _pad_rows(v[:, gsl], BLK).astype(BF16)
        kt = c_ref[0, 0, 0:D_GROUP_W, :].astype(BF16)
        vt = c_ref[0, 0, D_GROUP_W:2 * D_GROUP_W, :].astype(BF16)
        s1 = jnp.dot(qbd, kt, preferred_element_type=F32) + bc_ref[0]
        s2 = _nt_dot(qbd, kn) + bn_ref[g]
        (p1, p2), den, m = _softmax_parts([s1, s2], None)
        of = _nt_dot(p1, vt) + jnp.dot(p2, vn, preferred_element_type=F32)
        lse = m + jnp.log(den)
        og = jnp.zeros((DEC_SEQ, D_GROUP_W), F32)
        lg = jnp.zeros((DEC_SEQ, D_GROUP_W), F32)
        for h in range(D_HEADS_PER):
            rs = slice(h * DEC_SEQ, (h + 1) * DEC_SEQ)
            sel = out_lane_head == h
            og = og + jnp.where(sel, of[rs, :], 0.0)
            lg = lg + jnp.where(sel, lse[rs, :], 0.0)
        o_ref[...] = og
        l_ref[...] = lg


def _shift_block(c_ref, n_ref, o_ref, i):
    width, wc = c_ref.shape[2], c_ref.shape[3]
    lane = lax.broadcasted_iota(jnp.int32, (width, BLK), 1)
    shifted = pltpu.roll(c_ref[0, i], wc - DEC_SEQ, 1)
    new_t = jnp.concatenate([jnp.zeros((BLK - DEC_SEQ, width), F32), n_ref[0, i]], axis=0).T
    if wc > BLK:
        o_ref[0, i, :, 0:wc - BLK] = shifted[:, 0:wc - BLK]
    o_ref[0, i, :, wc - BLK:wc] = jnp.where(lane >= BLK - DEC_SEQ, new_t, shifted[:, wc - BLK:wc])


def _shift_kernel(c_ref, n_ref, o_ref, *, bb):
    for i in range(bb):
        _shift_block(c_ref, n_ref, o_ref, i)


def _dilated_sample(q, k, v, caches, layer, bcs, bn):
    tok_spec = pl.BlockSpec((DEC_SEQ, D_QKV_W), lambda b: (b, 0))
    out_spec = pl.BlockSpec((DEC_SEQ, D_GROUP_W), lambda b: (b, 0))
    outs = pl.pallas_call(
        _dil_sample_kernel,
        grid=(DEC_BATCH,),
        in_specs=[tok_spec, tok_spec, tok_spec]
        + [pl.BlockSpec((1, 1, 2 * D_GROUP_W, win), lambda b: (layer, b, 0, 0)) for win, _ in D_PAIRS]
        + [pl.BlockSpec(bc.shape, lambda b: (0, 0, 0)) for bc in bcs]
        + [pl.BlockSpec(bn.shape, lambda b: (0, 0, 0))],
        out_specs=[out_spec] * 6,
        out_shape=[jax.ShapeDtypeStruct((NS_TOK, D_GROUP_W), F32)] * 6,
        compiler_params=_cparams("parallel"),
        name="dilated_sample",
    )(q, k, v, *caches, *bcs, bn)
    return outs[:3], outs[3:]


def _dil_sample_shift_kernel(q_ref, k_ref, v_ref, c0_ref, c1_ref, c2_ref, n0_ref, n1_ref, n2_ref,
                             bc0_ref, bc1_ref, bc2_ref, bn_ref,
                             o0_ref, o1_ref, o2_ref, l0_ref, l1_ref, l2_ref, s0_ref, s1_ref, s2_ref):
    for c_ref, n_ref, s_ref in ((c0_ref, n0_ref, s0_ref), (c1_ref, n1_ref, s1_ref), (c2_ref, n2_ref, s2_ref)):
        _shift_block(c_ref, n_ref, s_ref, 0)

    @pl.when(pl.program_id(0) == pl.num_programs(0) - 1)
    def _():
        _dil_sample_kernel(q_ref, k_ref, v_ref, c0_ref, c1_ref, c2_ref, bc0_ref, bc1_ref, bc2_ref, bn_ref,
                           o0_ref, o1_ref, o2_ref, l0_ref, l1_ref, l2_ref)


def _dilated_sample_and_shift(q, k, v, caches, new_rows, bcs, bn):
    n_layers = caches[0].shape[0]
    last = n_layers - 1
    tok_row = lambda l, b: jnp.where(l == last, b, 0)
    tok_spec = pl.BlockSpec((DEC_SEQ, D_QKV_W), lambda l, b: (tok_row(l, b), 0))
    out_spec = pl.BlockSpec((DEC_SEQ, D_GROUP_W), lambda l, b: (tok_row(l, b), 0))
    cache_specs = [pl.BlockSpec((1, 1, 2 * D_GROUP_W, win), lambda l, b: (l, b, 0, 0)) for win, _ in D_PAIRS]
    new_spec = pl.BlockSpec((1, 1, DEC_SEQ, 2 * D_GROUP_W), lambda l, b: (l, b, 0, 0))
    outs = pl.pallas_call(
        _dil_sample_shift_kernel,
        grid=(n_layers, DEC_BATCH),
        in_specs=[tok_spec, tok_spec, tok_spec] + cache_specs + [new_spec] * 3
        + [pl.BlockSpec(bc.shape, lambda l, b: (0, 0, 0)) for bc in bcs]
        + [pl.BlockSpec(bn.shape, lambda l, b: (0, 0, 0))],
        out_specs=[out_spec] * 6 + cache_specs,
        out_shape=[jax.ShapeDtypeStruct((NS_TOK, D_GROUP_W), F32)] * 6
        + [jax.ShapeDtypeStruct(c.shape, c.dtype) for c in caches],
        compiler_params=_cparams("arbitrary", "arbitrary"),
        name="dilated_sample_shift",
    )(q, k, v, *caches, *new_rows, *bcs, bn)
    return outs[:3], outs[3:6], outs[6:]


def _shift_cache(cache, new_rows, name):
    n_layers, _, width, wc = cache.shape
    bb = max(1, (4 * 2 ** 20) // (wc * width * 4))
    return pl.pallas_call(
        functools.partial(_shift_kernel, bb=bb),
        grid=(n_layers, DEC_BATCH // bb),
        in_specs=[pl.BlockSpec((1, bb, width, wc), lambda l, i: (l, i, 0, 0)),
                  pl.BlockSpec((1, bb, DEC_SEQ, width), lambda l, i: (l, i, 0, 0))],
        out_specs=pl.BlockSpec((1, bb, width, wc), lambda l, i: (l, i, 0, 0)),
        out_shape=jax.ShapeDtypeStruct(cache.shape, cache.dtype),
        compiler_params=_cparams("parallel", "parallel"),
        name=name,
    )(cache, new_rows)


def _to_time_major(src_ref, dst_ref, off, bt, lc):
    for b in range(bt):
        x = src_ref[b]
        for j in range(dst_ref.shape[0]):
            dst_ref[j, pl.ds(off + b, lc, stride=bt), :] = x[:, j * LANES:(j + 1) * LANES]


def _from_time_major(src_ref, dst_ref, bt, lc):
    for b in range(bt):
        dst_ref[b] = jnp.concatenate(
            [src_ref[j, pl.ds(b, lc, stride=bt), :] for j in range(src_ref.shape[0])], axis=1)


def _lane_blocks(ref, r0, nrows):
    return jnp.concatenate([ref[j, r0:r0 + nrows, :] for j in range(ref.shape[0])], axis=1)


def _store_lane_blocks(ref, r0, x):
    for j in range(ref.shape[0]):
        ref[j, r0:r0 + x.shape[0], :] = x[:, j * LANES:(j + 1) * LANES]


def _token_spec(bt, lc, width):
    return pl.BlockSpec((bt, lc, width), lambda c: (0, c, 0))


def _s5_kernel(u_ref, h0_ref, bm_ref, cm_ref, lam_ref, d_ref, gw_ref, gb_ref, o_ref, hl_ref,
               us_ref, hs_ref, *, bt, lc):
    rows = lc * bt
    blk_w = 2 * B_HALF

    @pl.when(pl.program_id(0) == 0)
    def _():
        hs_ref[0:bt, :] = h0_ref[...]

    _to_time_major(u_ref, us_ref, 0, bt, lc)
    u = _lane_blocks(us_ref, 0, rows)
    ub = u.astype(BF16)
    for j in range(B_LANE_BLOCKS):
        cols = jnp.dot(ub[:, j * LANES:(j + 1) * LANES], bm_ref[j], preferred_element_type=F32)
        hs_ref[bt:, j * blk_w:j * blk_w + B_HALF] = cols[:, :B_HALF]
        hs_ref[bt:, j * blk_w + B_HALF:(j + 1) * blk_w] = cols[:, :B_HALF] + cols[:, B_HALF:]

    def step(t, carry):
        r0 = pl.multiple_of(t * bt, bt)
        for j in range(B_LANE_BLOCKS):
            re = slice(j * blk_w, j * blk_w + B_HALF)
            im = slice(j * blk_w + B_HALF, (j + 1) * blk_w)
            lr = lam_ref[2 * j:2 * j + 1, :]
            li = lam_ref[2 * j + 1:2 * j + 2, :]
            pr = hs_ref[pl.ds(r0, bt), re]
            pi = hs_ref[pl.ds(r0, bt), im]
            hs_ref[pl.ds(r0 + bt, bt), re] = lr * pr - li * pi + hs_ref[pl.ds(r0 + bt, bt), re]
            hs_ref[pl.ds(r0 + bt, bt), im] = lr * pi + li * pr + hs_ref[pl.ds(r0 + bt, bt), im]
        return carry

    lax.fori_loop(0, lc, step, 0)

    ys = []
    for j in range(B_LANE_BLOCKS):
        h_re = hs_ref[bt:, j * blk_w:j * blk_w + B_HALF]
        h_im = hs_ref[bt:, j * blk_w + B_HALF:(j + 1) * blk_w]
        lhs = jnp.concatenate([h_re + h_im, h_im], axis=1).astype(BF16)
        ys.append(jnp.dot(lhs, cm_ref[j], preferred_element_type=F32))
    y = jax.nn.gelu(jnp.concatenate(ys, axis=1) + d_ref[...] * u)
    z = jnp.dot(y.astype(BF16), gw_ref[...], preferred_element_type=F32) + gb_ref[...]
    _store_lane_blocks(us_ref, 0, y * jax.nn.sigmoid(z))
    _from_time_major(us_ref, o_ref, bt, lc)
    last = hs_ref[rows:rows + bt, :]
    hl_ref[...] = last
    hs_ref[0:bt, :] = last


def _s5_mixer(u, h0, prm, bt, seq, lc, name):
    tok = _token_spec(bt, lc, B_WIDTH)
    u_in = u.reshape(bt, seq, B_WIDTH)
    full = lambda a: pl.BlockSpec(a.shape, lambda c: (0,) * a.ndim)
    out, hl = pl.pallas_call(
        functools.partial(_s5_kernel, bt=bt, lc=lc),
        grid=(seq // lc,),
        in_specs=[tok, full(h0), full(prm["bm"]), full(prm["cm"]), full(prm["lam"]), full(prm["d"]),
                  full(prm["glu_w"]), full(prm["glu_b"])],
        out_specs=[tok, pl.BlockSpec((bt, B_STATE_W), lambda c: (0, 0))],
        out_shape=[jax.ShapeDtypeStruct(u_in.shape, F32), jax.ShapeDtypeStruct((bt, B_STATE_W), F32)],
        scratch_shapes=[pltpu.VMEM((B_WIDTH // LANES, lc * bt, LANES), F32),
                        pltpu.VMEM((lc * bt + bt, B_STATE_W), F32)],
        compiler_params=_cparams("arbitrary"),
        name=name,
    )(u_in, h0, prm["bm"], prm["cm"], prm["lam"], prm["d"], prm["glu_w"], prm["glu_b"])
    return out.reshape(bt * seq, B_WIDTH), hl


def _s5_params(a_re, a_im, log_dt, b_re, b_im, c_re, c_im, d, glu_w, glu_b):
    lam = lax.complex(a_re.astype(F32), a_im.astype(F32))
    dt = jnp.exp(log_dt.astype(F32))[:, None]
    lam_bar = jnp.exp(lam * dt)
    b_bar = ((lam_bar - 1.0) / lam)[..., None] * lax.complex(b_re.astype(F32), b_im.astype(F32))
    nb, gb = B_LANE_BLOCKS, B_GROUPS_PER_BLOCK
    eye = jnp.eye(gb, dtype=F32)

    def in_mat(part):
        p = part.reshape(nb, gb, B_STATE, B_GROUP_CH)
        return jnp.einsum("jgnc,gh->jgchn", p, eye).reshape(nb, gb * B_GROUP_CH, gb * B_STATE)

    def out_mat(part):
        p = part.reshape(nb, gb, B_GROUP_CH, B_STATE)
        return jnp.einsum("jgcn,gh->jgnhc", p, eye).reshape(nb, gb * B_STATE, gb * B_GROUP_CH)

    bm = jnp.concatenate([in_mat(b_bar.real), in_mat(b_bar.imag - b_bar.real)], axis=2).astype(BF16)
    c_re32, c_im32 = c_re.astype(F32), c_im.astype(F32)
    cm = jnp.concatenate([out_mat(c_re32), out_mat(-(c_re32 + c_im32))], axis=1).astype(BF16)
    lam_rows = jnp.stack([lam_bar.real.reshape(nb, B_HALF), lam_bar.imag.reshape(nb, B_HALF)],
                         axis=1).reshape(2 * nb, B_HALF)
    return {"bm": bm, "cm": cm, "lam": lam_rows, "d": d.astype(F32).reshape(1, B_WIDTH),
            "glu_w": glu_w.astype(BF16), "glu_b": glu_b.astype(F32).reshape(1, B_WIDTH)}


def _s5_state_to_cols(state):
    bt = state.shape[0]
    s = state.astype(F32).reshape(bt, B_LANE_BLOCKS, B_GROUPS_PER_BLOCK, B_STATE, 2)
    return s.transpose(0, 1, 4, 2, 3).reshape(bt, B_STATE_W)


def _s5_cols_to_state(cols):
    bt = cols.shape[0]
    s = cols.reshape(bt, B_LANE_BLOCKS, 2, B_GROUPS_PER_BLOCK, B_STATE)
    return s.transpose(0, 1, 3, 4, 2).reshape(bt, B_GROUPS, B_STATE, 2)


def _rglru_kernel(xr_ref, gate_ref, cb_ref, h0_ref, cw_ref, cbias_ref, wa_ref, ba_ref, wx_ref, bx_ref,
                  nsp_ref, o_ref, hl_ref, xp_ref, gs_ref, a_ref, hs_ref, *, bt, lc):
    rows = lc * bt
    pad = (C_CONV - 1) * bt

    @pl.when(pl.program_id(0) == 0)
    def _():
        _store_lane_blocks(xp_ref, 0, cb_ref[...])
        hs_ref[0:bt, :] = h0_ref[...]

    _to_time_major(xr_ref, xp_ref, pad, bt, lc)
    _to_time_major(gate_ref, gs_ref, 0, bt, lc)
    xc = _lane_blocks(xp_ref, 0, rows) * cw_ref[0:1, :]
    for tap in range(1, C_CONV):
        xc = xc + _lane_blocks(xp_ref, tap * bt, rows) * cw_ref[tap:tap + 1, :]
    xcf = xc + cbias_ref[...]
    xb = xcf.astype(BF16)
    r = jax.nn.sigmoid(jnp.dot(xb, wa_ref[...], preferred_element_type=F32) + ba_ref[...])
    i = jax.nn.sigmoid(jnp.dot(xb, wx_ref[...], preferred_element_type=F32) + bx_ref[...])
    log_a = nsp_ref[...] * r
    a = jnp.exp(log_a)
    a_ref[...] = a
    hs_ref[bt:, :] = jnp.sqrt(-jnp.tanh(log_a) * (a * a + 1.0)) * (i * xcf)

    def step(t, carry):
        r0 = pl.multiple_of(t * bt, bt)
        hs_ref[pl.ds(r0 + bt, bt), :] = (a_ref[pl.ds(r0, bt), :] * hs_ref[pl.ds(r0, bt), :]
                                         + hs_ref[pl.ds(r0 + bt, bt), :])
        return carry

    lax.fori_loop(0, lc, step, 0)

    _store_lane_blocks(gs_ref, 0, hs_ref[bt:, :] * jax.nn.gelu(_lane_blocks(gs_ref, 0, rows)))
    _from_time_major(gs_ref, o_ref, bt, lc)
    last = hs_ref[rows:rows + bt, :]
    hl_ref[...] = last
    hs_ref[0:bt, :] = last
    _store_lane_blocks(xp_ref, 0, _lane_blocks(xp_ref, rows, pad))


def _rglru_mixer(xr, gate, conv_buf, h0, prm, bt, seq, lc, name):
    rows = lc * bt
    tok = _token_spec(bt, lc, C_WIDTH)
    shape3 = lambda x: x.reshape(bt, seq, C_WIDTH)
    full = lambda a: pl.BlockSpec(a.shape, lambda c: (0,) * a.ndim)
    names = ("conv_w", "conv_b", "wa", "ba", "wx", "bx", "nsp")
    out, hl = pl.pallas_call(
        functools.partial(_rglru_kernel, bt=bt, lc=lc),
        grid=(seq // lc,),
        in_specs=[tok, tok, full(conv_buf), full(h0)] + [full(prm[k]) for k in names],
        out_specs=[tok, pl.BlockSpec((bt, C_WIDTH), lambda c: (0, 0))],
        out_shape=[jax.ShapeDtypeStruct(shape3(xr).shape, F32), jax.ShapeDtypeStruct((bt, C_WIDTH), F32)],
        scratch_shapes=[pltpu.VMEM((C_WIDTH // LANES, rows + (C_CONV - 1) * bt, LANES), F32),
                        pltpu.VMEM((C_WIDTH // LANES, rows, LANES), F32),
                        pltpu.VMEM((rows, C_WIDTH), F32),
                        pltpu.VMEM((rows + bt, C_WIDTH), F32)],
        compiler_params=_cparams("arbitrary"),
        name=name,
    )(shape3(xr), shape3(gate), conv_buf, h0, *[prm[k] for k in names])
    return out.reshape(bt * seq, C_WIDTH), hl


def _rglru_params(conv_w, conv_b, gate_a_w, gate_a_b, gate_x_w, gate_x_b, lru_lambda):
    eye = jnp.eye(C_BLOCKS, dtype=F32)

    def block_diag(w):
        return jnp.einsum("njk,nm->njmk", w.astype(F32), eye).reshape(C_WIDTH, C_WIDTH).astype(BF16)

    row = lambda x: x.astype(F32).reshape(1, C_WIDTH)
    return {"conv_w": conv_w.astype(F32), "conv_b": row(conv_b),
            "wa": block_diag(gate_a_w), "ba": row(gate_a_b),
            "wx": block_diag(gate_x_w), "bx": row(gate_x_b),
            "nsp": row(-C_POWER * jax.nn.softplus(-lru_lambda.astype(F32)))}


def _residual_out(x_ref, acc, tail_refs):
    *route_refs, o_ref = tail_refs
    x1 = x_ref[...] + acc
    if not route_refs:
        o_ref[...] = x1
        return
    g_ref, rw_ref, rb_ref = route_refs
    o_ref[:, 0:D_MODEL] = x1
    o_ref[:, D_MODEL:] = _route_lanes(x1, g_ref[...], rw_ref[...], rb_ref[...])


def _out_proj_even_kernel(x_ref, oa_ref, ob_ref, w_ref, *tail_refs):
    acc = jnp.dot(oa_ref[...].astype(BF16), w_ref[0:A_Q_W, :], preferred_element_type=F32)
    acc = acc + jnp.dot(ob_ref[...].astype(BF16), w_ref[A_Q_W:, :], preferred_element_type=F32)
    _residual_out(x_ref, acc, tail_refs)


def _out_proj_odd_kernel(x_ref, oc_ref, o0_ref, o1_ref, o2_ref, l0_ref, l1_ref, l2_ref, w_ref, *tail_refs):
    l0, l1, l2 = l0_ref[...], l1_ref[...], l2_ref[...]
    m = jnp.maximum(jnp.maximum(l0, l1), l2)
    e0, e1, e2 = jnp.exp(l0 - m), jnp.exp(l1 - m), jnp.exp(l2 - m)
    od = (o0_ref[...] * e0 + o1_ref[...] * e1 + o2_ref[...] * e2) / (e0 + e1 + e2)
    acc = jnp.dot(oc_ref[...].astype(BF16), w_ref[0:C_WIDTH, :], preferred_element_type=F32)
    acc = acc + jnp.dot(od.astype(BF16), w_ref[C_WIDTH:, :], preferred_element_type=F32)
    _residual_out(x_ref, acc, tail_refs)


def _out_proj(kernel, x, parts, w, name, in_place=True, route_prm=None):
    n = x.shape[0]
    tm = 512
    in_specs = ([pl.BlockSpec((tm, D_MODEL), lambda i: (i, 0))]
                + [pl.BlockSpec((tm, p.shape[1]), lambda i: (i, 0)) for p in parts]
                + [pl.BlockSpec(w.shape, lambda i: (0, 0))])
    args = [x, *parts, w]
    width = D_MODEL
    if route_prm is not None:
        for key in ("g", "rw", "rb"):
            in_specs.append(pl.BlockSpec(route_prm[key].shape, lambda i: (0, 0)))
            args.append(route_prm[key])
        width = MOE_AUG_W
    return pl.pallas_call(
        kernel,
        grid=(n // tm,),
        in_specs=in_specs,
        out_specs=pl.BlockSpec((tm, width), lambda i: (i, 0)),
        out_shape=jax.ShapeDtypeStruct((n, width), F32),
        input_output_aliases={0: 0} if in_place and route_prm is None else {},
        compiler_params=_cparams("parallel"),
        name=name,
    )(*args)


MOE_TM = 512


def _ffn_norm(x, g):
    y = x * lax.rsqrt(jnp.mean(x * x, axis=-1, keepdims=True) + RMS_EPS)
    return (y * g).astype(BF16)


def _route(xb, rw, rb):
    logits = jnp.dot(xb, rw, preferred_element_type=F32) + rb
    lane = lax.broadcasted_iota(jnp.int32, logits.shape, 1).astype(F32)
    ninf = float("-inf")
    far = float(ROUTER_LANES)
    lg = jnp.where(lane < MOE_GROUPS, logits, ninf)
    gmax = jnp.max(lg, axis=1, keepdims=True)
    g_idx = jnp.min(jnp.where(lg == gmax, lane, far), axis=1, keepdims=True)
    g_w = 1.0 / jnp.sum(jnp.exp(lg - gmax), axis=1, keepdims=True)
    lane_grp = jnp.floor((lane - MOE_GROUPS) * (1.0 / MOE_PER_GROUP))
    in_grp = (lane >= MOE_GROUPS) & (lane < MOE_GROUPS + MOE_EXPERTS) & (lane_grp == g_idx)
    le = jnp.where(in_grp, logits, ninf)
    v1 = jnp.max(le, axis=1, keepdims=True)
    i1 = jnp.min(jnp.where(le == v1, lane, far), axis=1, keepdims=True)
    le2 = jnp.where(lane == i1, ninf, le)
    v2 = jnp.max(le2, axis=1, keepdims=True)
    i2 = jnp.min(jnp.where(le2 == v2, lane, far), axis=1, keepdims=True)
    e2 = jnp.exp(v2 - v1)
    w1 = g_w / (1.0 + e2)
    w2 = g_w * e2 / (1.0 + e2)
    return lane, g_idx, i1 - MOE_GROUPS, i2 - MOE_GROUPS, w1, w2


def _moe_kernel(x_ref, g_ref, rw_ref, rb_ref, wg_ref, wu_ref, wd_ref, o_ref, xn_ref, gates_ref, acc_ref):
    e = pl.program_id(1)

    @pl.when(e == 0)
    def _():
        xb0 = _ffn_norm(x_ref[...], g_ref[...])
        xn_ref[...] = xb0
        lane, _, e1, e2, w1, w2 = _route(xb0, rw_ref[...], rb_ref[...])
        gates_ref[...] = jnp.where(lane == e1, w1, 0.0) + jnp.where(lane == e2, w2, 0.0)
        acc_ref[...] = jnp.zeros_like(acc_ref)

    xb = xn_ref[...]
    gates = gates_ref[...]
    lane_i = lax.broadcasted_iota(jnp.int32, gates.shape, 1)
    gate = jnp.sum(jnp.where(lane_i == e, gates, 0.0), axis=1, keepdims=True)
    hg = jnp.dot(xb, wg_ref[0], preferred_element_type=F32)
    hu = jnp.dot(xb, wu_ref[0], preferred_element_type=F32)
    hid = jax.nn.silu(hg) * hu * gate
    acc_ref[...] += jnp.dot(hid.astype(BF16), wd_ref[0], preferred_element_type=F32)

    @pl.when(e == MOE_EXPERTS - 1)
    def _():
        o_ref[...] = x_ref[...] + acc_ref[...]


def _moe(x, prm, name):
    n = x.shape[0]
    tm = 2 * MOE_TM if n % (2 * MOE_TM) == 0 else MOE_TM
    return pl.pallas_call(
        _moe_kernel,
        grid=(n // tm, MOE_EXPERTS),
        in_specs=[pl.BlockSpec((tm, D_MODEL), lambda i, e: (i, 0)),
                  pl.BlockSpec((1, D_MODEL), lambda i, e: (0, 0)),
                  pl.BlockSpec((D_MODEL, ROUTER_LANES), lambda i, e: (0, 0)),
                  pl.BlockSpec((1, ROUTER_LANES), lambda i, e: (0, 0)),
                  pl.BlockSpec((1, D_MODEL, MOE_FF), lambda i, e: (e, 0, 0)),
                  pl.BlockSpec((1, D_MODEL, MOE_FF), lambda i, e: (e, 0, 0)),
                  pl.BlockSpec((1, MOE_FF, D_MODEL), lambda i, e: (e, 0, 0))],
        out_specs=pl.BlockSpec((tm, D_MODEL), lambda i, e: (i, 0)),
        out_shape=jax.ShapeDtypeStruct((n, D_MODEL), F32),
        scratch_shapes=[pltpu.VMEM((tm, D_MODEL), BF16),
                        pltpu.VMEM((tm, ROUTER_LANES), F32),
                        pltpu.VMEM((tm, D_MODEL), F32)],
        input_output_aliases={0: 0},
        compiler_params=_cparams("parallel", "arbitrary"),
        name=name,
    )(x, prm["g"], prm["rw"], prm["rb"], prm["wg"], prm["wu"], prm["wd"])


def _moe_params(norm_g, router_g, router_g_b, router_e, router_e_b, w_gate, w_up, w_down):
    used = MOE_GROUPS + MOE_EXPERTS
    rw = jnp.concatenate([router_g.astype(F32), router_e.astype(F32),
                          jnp.zeros((D_MODEL, ROUTER_LANES - used), F32)], axis=1)
    rb = jnp.concatenate([router_g_b.astype(F32), router_e_b.astype(F32),
                          jnp.zeros((ROUTER_LANES - used,), F32)]).reshape(1, ROUTER_LANES)
    return {"g": norm_g.astype(F32).reshape(1, D_MODEL), "rw": rw.astype(BF16), "rb": rb,
            "wg": w_gate.astype(BF16), "wu": w_up.astype(BF16), "wd": w_down.astype(BF16)}


MOE_AUG_W = D_MODEL + ROUTER_LANES
GATE_LANE0 = 1


def _route_lanes(x, g, rw, rb):
    lane, g_idx, e1, e2, w1, w2 = _route(_ffn_norm(x, g), rw, rb)
    l1 = e1 - g_idx * MOE_PER_GROUP + GATE_LANE0
    l2 = e2 - g_idx * MOE_PER_GROUP + GATE_LANE0
    return jnp.where(lane == 0.0, g_idx, 0.0) + jnp.where(lane == l1, w1, 0.0) + jnp.where(lane == l2, w2, 0.0)


def _moe_plan(group_col, n):
    tm = MOE_TM
    n_tiles_max = n // tm + MOE_GROUPS
    g = group_col.astype(jnp.int32)
    onehot = (g[:, None] == jnp.arange(MOE_GROUPS)[None, :]).astype(jnp.int32)
    counts = jnp.sum(onehot, axis=0)
    rank = jnp.sum((jnp.cumsum(onehot, axis=0) - onehot) * onehot, axis=1)
    padded = (counts + tm - 1) // tm * tm
    ends = jnp.cumsum(padded)
    starts = ends - padded
    pos = jnp.sum(onehot * starts[None, :], axis=1) + rank
    src = jnp.zeros((n_tiles_max * tm,), jnp.int32).at[pos].set(jnp.arange(n, dtype=jnp.int32))
    tile0 = jnp.arange(n_tiles_max, dtype=jnp.int32) * tm
    tile_group = jnp.minimum(jnp.sum((tile0[:, None] >= ends[None, :]).astype(jnp.int32), axis=1),
                             MOE_GROUPS - 1)
    real = jnp.clip(counts[tile_group] - (tile0 - starts[tile_group]), 0, tm)
    real = jnp.where(tile0 < ends[-1], real, 0).astype(jnp.int32)
    return src, tile_group.astype(jnp.int32), real, (ends[-1:] // tm).astype(jnp.int32)


def _moe_expert_kernel(src_ref, grp_ref, real_ref, nt_ref, xa_hbm, g_ref, pg_ref, wg_ref, wu_ref, wd_ref,
                       out_hbm, xbuf, ybuf, gsem, ssem, *, post_norm):
    del grp_ref
    tm = MOE_TM
    i = pl.program_id(0)
    n_tiles = nt_ref[0]
    slot = lax.rem(i, 2)
    sub = SUBLANES

    def row_in(tile, s, j, u):
        row = src_ref[tile * tm + j * sub + u]
        return pltpu.make_async_copy(xa_hbm.at[row >> 3, pl.ds(row & 7, 1)],
                                     xbuf.at[s, j, pl.ds(u, 1)], gsem.at[s])

    def row_out(tile, s, j, u):
        row = src_ref[tile * tm + j * sub + u]
        return pltpu.make_async_copy(ybuf.at[s, j, pl.ds(u, 1)],
                                     out_hbm.at[row >> 3, pl.ds(row & 7, 1)], ssem.at[s])

    def start_gather(tile, s):
        def body(j, c):
            for u in range(sub):
                row_in(tile, s, j, u).start(priority=u % 2)
            return c
        lax.fori_loop(0, tm // sub, body, 0)

    def wait_gather(s):
        pltpu.make_async_copy(xa_hbm.at[pl.ds(0, tm // sub)], xbuf.at[s], gsem.at[s]).wait()

    def start_scatter(tile, s):
        real = real_ref[tile]
        full = real // sub

        def body(j, c):
            for u in range(sub):
                row_out(tile, s, j, u).start(priority=u % 2)
            return c
        lax.fori_loop(0, full, body, 0)

        def tail(r, c):
            row_out(tile, s, full, r - full * sub).start()
            return c
        lax.fori_loop(full * sub, real, tail, 0)

    def wait_scatter(tile, s):
        real = real_ref[tile]
        for k in range(tm.bit_length()):
            size = 1 << k

            @pl.when((real >> k) & 1 == 1)
            def _():
                if size >= sub:
                    piece = (ybuf.at[s, pl.ds(0, size // sub)], out_hbm.at[pl.ds(0, size // sub)])
                else:
                    piece = (ybuf.at[s, 0, pl.ds(0, size)], out_hbm.at[0, pl.ds(0, size)])
                pltpu.make_async_copy(piece[0], piece[1], ssem.at[s]).wait()

    @pl.when(i < n_tiles)
    def _():
        @pl.when(i == 0)
        def _():
            start_gather(0, 0)

        @pl.when(i + 1 < n_tiles)
        def _():
            start_gather(i + 1, 1 - slot)

        wait_gather(slot)

        @pl.when(i >= 2)
        def _():
            wait_scatter(i - 2, slot)

        xa = xbuf[slot].reshape(tm, MOE_AUG_W)
        x = xa[:, 0:D_MODEL]
        route = xa[:, D_MODEL:]
        xb = _ffn_norm(x, g_ref[...])
        acc = jnp.zeros((tm, D_MODEL), F32)
        for e in range(MOE_PER_GROUP):
            gate = route[:, GATE_LANE0 + e:GATE_LANE0 + e + 1]
            hg = jnp.dot(xb, wg_ref[0, e], preferred_element_type=F32)
            hu = jnp.dot(xb, wu_ref[0, e], preferred_element_type=F32)
            hid = jax.nn.silu(hg) * hu * gate
            acc = acc + jnp.dot(hid.astype(BF16), wd_ref[0, e], preferred_element_type=F32)
        y = x + acc
        if post_norm:
            y = y * lax.rsqrt(jnp.mean(y * y, axis=-1, keepdims=True) + RMS_EPS) * pg_ref[...]
        ybuf[slot] = y.reshape(tm // sub, sub, D_MODEL)
        start_scatter(i, slot)

        @pl.when(i == n_tiles - 1)
        def _():
            wait_scatter(i, slot)

            @pl.when(i >= 1)
            def _():
                wait_scatter(i - 1, 1 - slot)


def _moe_experts(xa, plan, prm, name, post_gain=None):
    n = xa.shape[0]
    post = prm["g"] if post_gain is None else post_gain.astype(F32).reshape(1, D_MODEL)
    tm = MOE_TM
    sub = SUBLANES
    src, tile_group, real, n_tiles = plan
    grouped = lambda w: w.reshape(MOE_GROUPS, MOE_PER_GROUP, w.shape[1], w.shape[2])
    w_spec = lambda a, b: pl.BlockSpec((1, MOE_PER_GROUP, a, b), lambda i, src, grp, real, nt: (grp[i], 0, 0, 0))
    grid_spec = pltpu.PrefetchScalarGridSpec(
        num_scalar_prefetch=4,
        grid=(n // tm + MOE_GROUPS,),
        in_specs=[pl.BlockSpec(memory_space=pl.ANY),
                  pl.BlockSpec((1, D_MODEL), lambda i, *_: (0, 0)),
                  pl.BlockSpec((1, D_MODEL), lambda i, *_: (0, 0)),
                  w_spec(D_MODEL, MOE_FF), w_spec(D_MODEL, MOE_FF), w_spec(MOE_FF, D_MODEL)],
        out_specs=pl.BlockSpec(memory_space=pl.ANY),
        scratch_shapes=[pltpu.VMEM((2, tm // sub, sub, MOE_AUG_W), F32),
                        pltpu.VMEM((2, tm // sub, sub, D_MODEL), F32),
                        pltpu.SemaphoreType.DMA((2,)),
                        pltpu.SemaphoreType.DMA((2,))])
    out = pl.pallas_call(
        functools.partial(_moe_expert_kernel, post_norm=post_gain is not None),
        grid_spec=grid_spec,
        out_shape=jax.ShapeDtypeStruct((n // sub, sub, D_MODEL), F32),
        compiler_params=_cparams("arbitrary"),
        name=name,
    )(src, tile_group, real, n_tiles, xa.reshape(n // sub, sub, MOE_AUG_W), prm["g"], post,
      grouped(prm["wg"]), grouped(prm["wu"]), grouped(prm["wd"]))
    return out.reshape(n, D_MODEL)


def _final_norm_kernel(x_ref, g_ref, o_ref):
    x = x_ref[...]
    y = x * lax.rsqrt(jnp.mean(x * x, axis=-1, keepdims=True) + RMS_EPS)
    o_ref[...] = y * g_ref[...]


def _final_norm(x, g, name):
    n = x.shape[0]
    tm = 512
    return pl.pallas_call(
        _final_norm_kernel,
        grid=(n // tm,),
        in_specs=[pl.BlockSpec((tm, D_MODEL), lambda i: (i, 0)),
                  pl.BlockSpec((1, D_MODEL), lambda i: (0, 0))],
        out_specs=pl.BlockSpec((tm, D_MODEL), lambda i: (i, 0)),
        out_shape=jax.ShapeDtypeStruct((n, D_MODEL), F32),
        compiler_params=_cparams("parallel"),
        name=name,
    )(x, g.astype(F32).reshape(1, D_MODEL))


def _t5_bucket(dist):
    n = jnp.maximum(dist, 0)
    max_exact = T5_BUCKETS // 2
    nf = jnp.maximum(n, 1).astype(F32)
    large = max_exact + (jnp.log(nf / max_exact) / math.log(T5_MAX_DIST / max_exact)
                         * (T5_BUCKETS - max_exact)).astype(jnp.int32)
    return jnp.where(n < max_exact, n, jnp.minimum(large, T5_BUCKETS - 1))


def _rel_bias(table, dist):
    hit = _t5_bucket(dist)[..., None] == jnp.arange(T5_BUCKETS)
    rows = jnp.sum(jnp.where(hit[..., None], table.astype(F32), 0.0), axis=-2)
    return jnp.moveaxis(rows, -1, 0)


def _band_bias_mask(table, max_dist, dil):
    r = jnp.arange(BLK)[:, None]
    s = jnp.arange(2 * BLK)[None, :]
    dist = BLK + r - s
    valid = (dist >= 0) & (dist <= max_dist)
    bias = _rel_bias(table, dist * dil)
    later = jnp.where(valid[None], bias, NEG)
    first = jnp.where((valid & (s >= BLK))[None], bias, NEG)
    return jnp.stack([first, later])


def _swa_sample_bias_mask(table):
    s = jnp.arange(DEC_SEQ)[:, None]
    col = jnp.arange(BLK)[None, :]
    dist_c = A_WINDOW + s - col
    dist_n = s - col
    tab = table[:, :A_Q_HEADS]
    bc = jnp.where(((dist_c >= 0) & (dist_c < A_WINDOW))[None], _rel_bias(tab, dist_c), NEG)
    bn = jnp.where(((dist_n >= 0) & (col < DEC_SEQ))[None], _rel_bias(tab, dist_n), NEG)
    return bc.reshape(A_Q_HEADS * DEC_SEQ, A_WINDOW), bn.reshape(A_Q_HEADS * DEC_SEQ, BLK)


def _dil_sample_bias_mask(table):
    s = jnp.arange(DEC_SEQ)[:, None]
    col = jnp.arange(BLK)[None, :]
    bcs, bns = [], []
    for g, (win, dil) in enumerate(D_PAIRS):
        lo = A_Q_HEADS + g * D_HEADS_PER
        tab = table[:, lo:lo + D_HEADS_PER]
        dist_c = win + s - jnp.arange(win)[None, :]
        valid_c = (dist_c >= 0) & (dist_c % dil == 0) & (dist_c <= win)
        bc = jnp.where(valid_c[None], _rel_bias(tab, dist_c), NEG)
        dist_n = s - col
        valid_n = (dist_n >= 0) & (dist_n % dil == 0) & (col < DEC_SEQ)
        bn = jnp.where(valid_n[None], _rel_bias(tab, dist_n), NEG)
        bcs.append(bc.reshape(1, D_HEADS_PER * DEC_SEQ, win))
        bns.append(bn.reshape(D_HEADS_PER * DEC_SEQ, BLK))
    return bcs, jnp.stack(bns)


def _native_cache(c):
    n_layers, nb, wc = c.shape[:3]
    return c.transpose(0, 1, 3, 4, 5, 2).reshape(n_layers, nb, -1, wc)


def _logical_cache(c, heads):
    n_layers, nb, _, wc = c.shape
    return c.reshape(n_layers, nb, 2, heads, HEAD_DIM, wc).transpose(0, 1, 5, 2, 3, 4)


def kernel(x_prompt, x_sample, cache_a_kv, state_b, state_c_h, state_c_conv, cache_d_g0, cache_d_g1, cache_d_g2, rel_table, norm_mix, norm_ffn, norm_final, w_in_even, w_out_even, sinks_a, s5_a_re, s5_a_im, s5_log_dt, s5_b_re, s5_b_im, s5_c_re, s5_c_im, s5_d, s5_glu_w, s5_glu_b, w_in_odd, w_out_odd, conv_w, conv_b, gate_a_w, gate_a_b, gate_x_w, gate_x_b, lru_lambda, moe_router_g, moe_router_g_b, moe_router_e, moe_router_e_b, moe_w_gate, moe_w_up, moe_w_down):
    xp = x_prompt.astype(F32).reshape(NP_TOK, D_MODEL)
    xs = x_sample.astype(F32).reshape(NS_TOK, D_MODEL)

    cache_a = _native_cache(cache_a_kv)
    caches_d = [_native_cache(c) for c in (cache_d_g0, cache_d_g1, cache_d_g2)]

    bm_a = _band_bias_mask(rel_table[:, :A_Q_HEADS], A_WINDOW - 1, 1)
    bm_d = [_band_bias_mask(rel_table[:, A_Q_HEADS + g * D_HEADS_PER:A_Q_HEADS + (g + 1) * D_HEADS_PER],
                            win // dil, dil) for g, (win, dil) in enumerate(D_PAIRS)]
    bmc_a_s, bmn_a_s = _swa_sample_bias_mask(rel_table)
    bcs_d_s, bn_d_s = _dil_sample_bias_mask(rel_table)

    a_p, a_new, b_p, b_s = [], [], [], []
    ch_p, ch_s, cc_p, cc_s = [], [], [], []
    d_p = [[], [], []]
    d_new = [[], [], []]

    for layer in range(DEPTH):
        mp = _moe_params(norm_ffn[layer], moe_router_g[layer], moe_router_g_b[layer], moe_router_e[layer],
                         moe_router_e_b[layer], moe_w_gate[layer], moe_w_up[layer], moe_w_down[layer])
        if layer % 2 == 0:
            e = layer // 2
            w_in = w_in_even[e].astype(BF16)
            w_out = w_out_even[e].astype(BF16)
            splits = (A_Q_W, 2 * A_KV_W, B_WIDTH)
            s5p = _s5_params(s5_a_re[e], s5_a_im[e], s5_log_dt[e], s5_b_re[e], s5_b_im[e],
                             s5_c_re[e], s5_c_im[e], s5_d[e], s5_glu_w[e], s5_glu_b[e])
            sinks = sinks_a[e].astype(F32)

            q_p, kv_p, u_p = _norm_proj(xp, norm_mix[layer], w_in, splits, f"in_proj_p{layer}")
            q_s, kv_s, u_s = _norm_proj(xs, norm_mix[layer], w_in, splits, f"in_proj_s{layer}")

            oa_p = _swa_prompt(q_p, kv_p, sinks, bm_a)
            oa_s = _swa_sample(q_s, kv_s, cache_a, e, sinks, bmc_a_s, bmn_a_s)

            ob_p, hl_p = _s5_mixer(u_p, jnp.zeros((BATCH, B_STATE_W), F32), s5p, BATCH, SEQ, 64, "s5_prompt")
            ob_s, hl_s = _s5_mixer(u_s, _s5_state_to_cols(state_b[e]), s5p, DEC_BATCH, DEC_SEQ, DEC_SEQ,
                                   "s5_sample")

            xa_p = _out_proj(_out_proj_even_kernel, xp, [oa_p, ob_p], w_out, f"out_proj_p{layer}",
                             route_prm=mp)
            xs = _out_proj(_out_proj_even_kernel, xs, [oa_s, ob_s], w_out, f"out_proj_s{layer}",
                           in_place=layer > 0)

            a_p.append(kv_p.reshape(BATCH, SEQ, 2, A_KV_HEADS, HEAD_DIM)[:, SEQ - A_WINDOW:])
            a_new.append(kv_s.reshape(DEC_BATCH, DEC_SEQ, 2 * A_KV_W))
            b_p.append(_s5_cols_to_state(hl_p))
            b_s.append(_s5_cols_to_state(hl_s))
        else:
            o = layer // 2
            w_in = w_in_odd[o].astype(BF16)
            w_out = w_out_odd[o].astype(BF16)
            splits = (C_WIDTH, C_WIDTH, D_QKV_W, D_QKV_W, D_QKV_W)
            lrp = _rglru_params(conv_w[o], conv_b[o], gate_a_w[o], gate_a_b[o], gate_x_w[o], gate_x_b[o],
                                lru_lambda[o])

            xr_p, gate_p, q_p, k_p, v_p = _norm_proj(xp, norm_mix[layer], w_in, splits, f"in_proj_p{layer}")
            xr_s, gate_s, q_s, k_s, v_s = _norm_proj(xs, norm_mix[layer], w_in, splits, f"in_proj_s{layer}")

            od_p, lse_p = [], []
            for g, (win, dil) in enumerate(D_PAIRS):
                og, lg = _dilated_prompt_group(q_p, k_p, v_p, bm_d[g], g, dil)
                od_p.append(og)
                lse_p.append(lg)
            k_s4 = k_s.reshape(DEC_BATCH, DEC_SEQ, D_N_GROUPS, D_GROUP_W)
            v_s4 = v_s.reshape(DEC_BATCH, DEC_SEQ, D_N_GROUPS, D_GROUP_W)
            for g in range(D_N_GROUPS):
                d_new[g].append(jnp.concatenate([k_s4[:, :, g], v_s4[:, :, g]], axis=-1))
            if o < caches_d[0].shape[0] - 1:
                od_s, lse_s = _dilated_sample(q_s, k_s, v_s, caches_d, o, bcs_d_s, bn_d_s)
            else:
                od_s, lse_s, shifted_d = _dilated_sample_and_shift(
                    q_s, k_s, v_s, caches_d, [jnp.stack(rows) for rows in d_new], bcs_d_s, bn_d_s)

            oc_p, hc_p = _rglru_mixer(xr_p, gate_p, jnp.zeros(((C_CONV - 1) * BATCH, C_WIDTH), F32),
                                      jnp.zeros((BATCH, C_WIDTH), F32), lrp, BATCH, SEQ, 128, "rglru_prompt")
            conv_s = state_c_conv[o].astype(F32).transpose(1, 0, 2).reshape((C_CONV - 1) * DEC_BATCH, C_WIDTH)
            oc_s, hc_s = _rglru_mixer(xr_s, gate_s, conv_s, state_c_h[o].astype(F32), lrp,
                                      DEC_BATCH, DEC_SEQ, DEC_SEQ, "rglru_sample")

            xa_p = _out_proj(_out_proj_odd_kernel, xp, [oc_p] + od_p + lse_p, w_out, f"out_proj_p{layer}",
                             route_prm=mp)
            xs = _out_proj(_out_proj_odd_kernel, xs, [oc_s] + list(od_s) + list(lse_s), w_out,
                           f"out_proj_s{layer}")

            ch_p.append(hc_p)
            ch_s.append(hc_s)
            cc_p.append(xr_p.reshape(BATCH, SEQ, C_WIDTH)[:, SEQ - (C_CONV - 1):])
            cc_s.append(xr_s.reshape(DEC_BATCH, DEC_SEQ, C_WIDTH)[:, DEC_SEQ - (C_CONV - 1):])
            k_p5 = k_p.reshape(BATCH, SEQ, D_N_GROUPS, D_HEADS_PER, HEAD_DIM)
            v_p5 = v_p.reshape(BATCH, SEQ, D_N_GROUPS, D_HEADS_PER, HEAD_DIM)
            for g, (win, dil) in enumerate(D_PAIRS):
                wc = min(win, SEQ)
                d_p[g].append(jnp.stack([k_p5[:, SEQ - wc:, g], v_p5[:, SEQ - wc:, g]], axis=2))

        xp = _moe_experts(xa_p, _moe_plan(xa_p[:, D_MODEL], NP_TOK), mp, f"moe_experts_p{layer}",
                          post_gain=norm_final if layer == DEPTH - 1 else None)
        xs = _moe(xs, mp, f"moe_s{layer}")

    y_prompt = xp.reshape(BATCH, SEQ, D_MODEL)
    y_sample = _final_norm(xs, norm_final, "final_norm_s").reshape(DEC_BATCH, DEC_SEQ, D_MODEL)

    new_a = _logical_cache(_shift_cache(cache_a, jnp.stack(a_new), "shift_cache_a"), A_KV_HEADS)
    new_d = [_logical_cache(c, D_HEADS_PER) for c in shifted_d]

    return (y_prompt, y_sample,
            jnp.stack(a_p), new_a, jnp.stack(b_p), jnp.stack(b_s),
            jnp.stack(ch_p), jnp.stack(ch_s), jnp.stack(cc_p), jnp.stack(cc_s),
            jnp.stack(d_p[0]), new_d[0], jnp.stack(d_p[1]), new_d[1], jnp.stack(d_p[2]), new_d[2])
```
